```python
import math
import jax
import jax.numpy as jnp
from jax import lax
import numpy as np

D_MODEL = 1024
BATCH = 2
SEQ = 8192
DEPTH = 2
DEC_BATCH = 128
DEC_SEQ = 4
PAST_LEN = 16384
PAGE_SIZE = 128

N_A_LAYERS = (DEPTH + 1) // 2
N_C_LAYERS = DEPTH // 2
N_PAGES = PAST_LEN // PAGE_SIZE
N_POOL = (DEC_BATCH * N_PAGES * 5) // 4

EPS = 1e-6
HG_HEADS = 4
HG_DK = 128
HG_DV = 128
HG_QK = HG_HEADS * HG_DK
HG_V = HG_HEADS * HG_DV
HG_CHUNK = 32
SSM_HEADS = 16
SSM_HEAD_DIM = 64
SSM_D_INNER = SSM_HEADS * SSM_HEAD_DIM
SSM_GROUPS = 2
SSM_STATE = 128
CONV_W = 4
CONV_DIM = SSM_D_INNER + 2 * SSM_GROUPS * SSM_STATE
SSM_CHUNK = 64
AB_SIZES = [HG_QK, HG_QK, HG_V, HG_V, SSM_D_INNER, CONV_DIM, SSM_HEADS]
D_IN_AB = sum(AB_SIZES)
D_MIX_AB = HG_V + SSM_D_INNER
MLA_HEADS = 16
MLA_Q_LORA = 256
MLA_KV_LORA = 256
MLA_NOPE = 64
MLA_ROPE = 32
MLA_V = 64
MLA_QBLOCK = 128
MLA_SCALE = (MLA_NOPE + MLA_ROPE) ** -0.5
ROPE_THETA = 10000.0
D_IN_C = MLA_Q_LORA + MLA_KV_LORA + MLA_ROPE
N_GROUPS = 4
EXPERTS_PER_GROUP = 8
N_EXPERTS = N_GROUPS * EXPERTS_PER_GROUP
TOP_K = 2
D_EXPERT = 256

kernel_name = 'hgrn2_mamba2_mla_hiermoe_step'


def rmsnorm(x, w):
    xf = x.astype(jnp.float32)
    return xf * lax.rsqrt(jnp.mean(xf * xf, axis=-1, keepdims=True) + EPS) * w


def split_points(sizes):
    return np.cumsum(sizes)[:-1].tolist()


def apply_rope(x, pos):
    half = MLA_ROPE // 2
    inv = ROPE_THETA ** (-jnp.arange(half, dtype=jnp.float32) / half)
    ang = pos.astype(jnp.float32)[:, None] * inv[None, :]
    cos = jnp.cos(ang)[None, :, None, :]
    sin = jnp.sin(ang)[None, :, None, :]
    x1, x2 = x[..., :half], x[..., half:]
    return jnp.concatenate([x1 * cos - x2 * sin, x1 * sin + x2 * cos], axis=-1)


def hgrn2_scan(q, k, v, logf, s0):
    q, k, v, logf = (a.astype(jnp.float32) for a in (q, k, v, logf))
    b_, t_, h_, dk = q.shape
    dv = v.shape[-1]
    c = math.gcd(t_, HG_CHUNK)
    n = t_ // c
    causal = jnp.tril(jnp.ones((c, c), dtype=bool))[None, :, :, None, None]

    def chunks(a):
        return jnp.moveaxis(a.reshape(b_, n, c, *a.shape[2:]), 1, 0)

    def step(s, inp):
        qc, kc, vc, lf = inp
        cum = jnp.cumsum(lf, axis=1)
        decay = jnp.exp(jnp.where(causal, cum[:, :, None] - cum[:, None, :], -jnp.inf))
        scores = jnp.einsum('bthk,btshk->btsh', qc, decay * kc[:, None])
        o = jnp.einsum('btsh,bshv->bthv', scores, vc) + jnp.einsum('bthk,bhkv->bthv', qc * jnp.exp(cum), s)
        last = cum[:, -1]
        s = s * jnp.exp(last)[..., None] + jnp.einsum('bshk,bshv->bhkv', kc * jnp.exp(last[:, None] - cum), vc)
        return s, o

    s_fin, o = lax.scan(step, s0.astype(jnp.float32), (chunks(q), chunks(k), chunks(v), chunks(logf)))
    return jnp.moveaxis(o, 0, 1).reshape(b_, t_, h_, dv), s_fin


def ssd_scan(x, dt, a, bm, cm, h0):
    x, dt, bm, cm = (z.astype(jnp.float32) for z in (x, dt, bm, cm))
    b_, t_, h_, p_ = x.shape
    rep = h_ // bm.shape[2]
    bh = jnp.repeat(bm, rep, axis=2)
    ch = jnp.repeat(cm, rep, axis=2)
    c = math.gcd(t_, SSM_CHUNK)
    n = t_ // c
    causal = jnp.tril(jnp.ones((c, c), dtype=bool))[None, :, :, None]

    def chunks(z):
        return jnp.moveaxis(z.reshape(b_, n, c, *z.shape[2:]), 1, 0)

    def step(h, inp):
        xc, dtc, bc, cc = inp
        cum = jnp.cumsum(dtc * a, axis=1)
        seg = jnp.exp(jnp.where(causal, cum[:, :, None] - cum[:, None, :], -jnp.inf))
        scores = jnp.einsum('bthn,bshn->btsh', cc, bc) * seg
        y = jnp.einsum('btsh,bshp->bthp', scores, dtc[..., None] * xc)
        y = y + jnp.einsum('bthn,bhpn->bthp', cc, h) * jnp.exp(cum)[..., None]
        last = cum[:, -1]
        w = jnp.exp(last[:, None] - cum) * dtc
        h = h * jnp.exp(last)[:, :, None, None] + jnp.einsum('bshn,bshp->bhpn', bc, w[..., None] * xc)
        return h, y

    h_fin, y = lax.scan(step, h0.astype(jnp.float32), (chunks(x), chunks(dt), chunks(bh), chunks(ch)))
    return jnp.moveaxis(y, 0, 1).reshape(b_, t_, h_, p_), h_fin


def causal_conv(xbc, buf, w, bias):
    t_ = xbc.shape[1]
    xp = jnp.concatenate([buf.astype(xbc.dtype), xbc], axis=1)
    out = xp[:, 0:t_] * w[0]
    for i in range(1, CONV_W):
        out = out + xp[:, i:i + t_] * w[i]
    return jax.nn.silu(out + bias), xp[:, t_:]


def ab_mixer(h, lb, s_hg, s_ssm, conv_buf, w_in, w_out, hg_norm, conv_w, conv_b, dt_bias, a_log, d_skip, ssm_norm):
    b_, t_, _ = h.shape
    proj = h @ w_in
    q, f, i_in, g, z, xbc, dt = jnp.split(proj, split_points(AB_SIZES), axis=-1)
    q = jax.nn.silu(q).reshape(b_, t_, HG_HEADS, HG_DK)
    fl = f.astype(jnp.float32).reshape(b_, t_, HG_HEADS, HG_DK)
    lbh = lb.reshape(HG_HEADS, HG_DK)
    logf = jnp.log(lbh + (1.0 - lbh) * jax.nn.sigmoid(fl))
    k = (1.0 - lbh) * jax.nn.sigmoid(-fl)
    v = i_in.reshape(b_, t_, HG_HEADS, HG_DV)
    o_hg, s_hg_new = hgrn2_scan(q, k, v, logf, s_hg)
    o_hg = rmsnorm(o_hg, hg_norm.reshape(HG_HEADS, HG_DV)) * jax.nn.silu(g.astype(jnp.float32)).reshape(b_, t_, HG_HEADS, HG_DV)
    o_hg = o_hg.reshape(b_, t_, HG_V)
    xbc, conv_new = causal_conv(xbc, conv_buf, conv_w, conv_b)
    xs, bm, cm = jnp.split(xbc, split_points([SSM_D_INNER, SSM_GROUPS * SSM_STATE, SSM_GROUPS * SSM_STATE]), axis=-1)
    dt = jax.nn.softplus(dt.astype(jnp.float32) + dt_bias)
    a = -jnp.exp(a_log.astype(jnp.float32))
    xs = xs.reshape(b_, t_, SSM_HEADS, SSM_HEAD_DIM)
    y, s_ssm_new = ssd_scan(xs, dt, a, bm.reshape(b_, t_, SSM_GROUPS, SSM_STATE),
                            cm.reshape(b_, t_, SSM_GROUPS, SSM_STATE), s_ssm)
    y = (y + d_skip[:, None] * xs).reshape(b_, t_, SSM_D_INNER) * jax.nn.silu(z.astype(jnp.float32))
    y = rmsnorm(y.reshape(b_, t_, SSM_GROUPS, SSM_D_INNER // SSM_GROUPS),
                ssm_norm.reshape(SSM_GROUPS, SSM_D_INNER // SSM_GROUPS)).reshape(b_, t_, SSM_D_INNER)
    out = jnp.concatenate([o_hg, y], axis=-1) @ w_out
    return out, s_hg_new, s_ssm_new, conv_new


def mla_project(h, pos, w_in, q_norm, kv_norm, w_uq):
    b_, t_, _ = h.shape
    cq, ckv, kr = jnp.split(h @ w_in, split_points([MLA_Q_LORA, MLA_KV_LORA, MLA_ROPE]), axis=-1)
    cq = rmsnorm(cq, q_norm)
    ckv = rmsnorm(ckv, kv_norm)
    qf = (cq @ w_uq).reshape(b_, t_, MLA_HEADS, MLA_NOPE + MLA_ROPE)
    q_nope = qf[..., :MLA_NOPE]
    q_rope = apply_rope(qf[..., MLA_NOPE:], pos)
    k_rope = apply_rope(kr[:, :, None, :], pos)[:, :, 0]
    return q_nope, q_rope, ckv, k_rope


def mla_prompt(h, w_in, q_norm, kv_norm, w_uq, w_uk, w_uv, w_out):
    b_, t_, _ = h.shape
    pos = jnp.arange(t_, dtype=jnp.int32)
    q_nope, q_rope, ckv, kr = mla_project(h, pos, w_in, q_norm, kv_norm, w_uq)
    k_nope = jnp.einsum('btl,lhd->bthd', ckv, w_uk)
    v = jnp.einsum('btl,lhv->bthv', ckv, w_uv)
    qb = math.gcd(t_, MLA_QBLOCK)

    def block(i):
        s0 = i * qb
        qn = lax.dynamic_slice_in_dim(q_nope, s0, qb, axis=1)
        qr = lax.dynamic_slice_in_dim(q_rope, s0, qb, axis=1)
        s = (jnp.einsum('bqhd,bkhd->bhqk', qn, k_nope) + jnp.einsum('bqhr,bkr->bhqk', qr, kr)) * MLA_SCALE
        mask = pos[None, :] <= (s0 + jnp.arange(qb, dtype=jnp.int32))[:, None]
        p = jax.nn.softmax(jnp.where(mask, s.astype(jnp.float32), -jnp.inf), axis=-1)
        return jnp.einsum('bhqk,bkhv->bqhv', p, v)

    o = lax.map(block, jnp.arange(t_ // qb, dtype=jnp.int32))
    o = jnp.moveaxis(o, 0, 1).reshape(b_, t_, MLA_HEADS * MLA_V)
    return o @ w_out, ckv, kr


def mla_sample(h, cache_ckv, cache_krope, page_table, layer_c, w_in, q_norm, kv_norm, w_uq, w_uk, w_uv, w_out):
    b_, t_, _ = h.shape
    pos = PAST_LEN + jnp.arange(t_, dtype=jnp.int32)
    q_nope, q_rope, ckv, kr = mla_project(h, pos, w_in, q_norm, kv_norm, w_uq)
    q_lat = jnp.einsum('bthd,lhd->bthl', q_nope, w_uk)
    causal = jnp.tril(jnp.ones((t_, t_), dtype=bool))[None]

    def one_seq(args):
        ql, qr, pages, c_new, r_new = args
        past_c = cache_ckv[layer_c, pages].reshape(-1, MLA_KV_LORA)
        past_r = cache_krope[layer_c, pages].reshape(-1, MLA_ROPE)
        s_past = jnp.einsum('thl,sl->hts', ql, past_c) + jnp.einsum('thr,sr->hts', qr, past_r)
        s_new = jnp.einsum('thl,sl->hts', ql, c_new) + jnp.einsum('thr,sr->hts', qr, r_new)
        s_new = jnp.where(causal, s_new.astype(jnp.float32) * MLA_SCALE, -jnp.inf)
        s = jnp.concatenate([s_past.astype(jnp.float32) * MLA_SCALE, s_new], axis=-1)
        p = jax.nn.softmax(s, axis=-1)
        n_past = past_c.shape[0]
        return jnp.einsum('hts,sl->thl', p[..., :n_past], past_c) + jnp.einsum('hts,sl->thl', p[..., n_past:], c_new)

    o_lat = lax.map(one_seq, (q_lat, q_rope, page_table, ckv, kr))
    o = jnp.einsum('bthl,lhv->bthv', o_lat, w_uv).reshape(b_, t_, MLA_HEADS * MLA_V)
    return o @ w_out, ckv, kr


def hier_moe(h, w_rg, b_rg, w_re, b_re, w_gate, w_up, w_down):
    b_, t_, d_ = h.shape
    x = h.reshape(-1, d_)
    g_logits = (x @ w_rg + b_rg).astype(jnp.float32)
    g_prob = jax.nn.softmax(g_logits, axis=-1)
    g_sel = jnp.argmax(g_logits, axis=-1)
    g_w = jnp.take_along_axis(g_prob, g_sel[:, None], axis=-1)
    e_logits = (x @ w_re + b_re).astype(jnp.float32).reshape(-1, N_GROUPS, EXPERTS_PER_GROUP)
    e_logits = jnp.take_along_axis(e_logits, g_sel[:, None, None], axis=1)[:, 0]
    top_w, top_i = lax.top_k(jax.nn.softmax(e_logits, axis=-1), TOP_K)
    top_w = top_w / jnp.sum(top_w, axis=-1, keepdims=True) * g_w
    e_idx = g_sel[:, None] * EXPERTS_PER_GROUP + top_i
    gates = jnp.einsum('nk,nke->ne', top_w, jax.nn.one_hot(e_idx, N_EXPERTS, dtype=jnp.float32))
    hid = jax.nn.silu(jnp.einsum('nd,edf->nef', x, w_gate)) * jnp.einsum('nd,edf->nef', x, w_up)
    out = jnp.einsum('nef,efd->nd', hid * gates[:, :, None], w_down)
    return out.reshape(b_, t_, d_)


def setup_inputs(seed: int = 0) -> dict:
    key = jax.random.key(seed)
    ks = iter(jax.random.split(key, 48))
    f32 = jnp.float32

    def nrm(shape, scale):
        return jax.random.normal(next(ks), shape, f32) * scale

    def gain(shape):
        return 1.0 + nrm(shape, 0.02)

    x_prompt = nrm((BATCH, SEQ, D_MODEL), 1.0)
    x_sample = nrm((DEC_BATCH, DEC_SEQ, D_MODEL), 1.0)
    state_hgrn = nrm((N_A_LAYERS, DEC_BATCH, HG_HEADS, HG_DK, HG_DV), 0.5)
    state_ssm = nrm((N_A_LAYERS, DEC_BATCH, SSM_HEADS, SSM_HEAD_DIM, SSM_STATE), 0.1)
    state_conv = nrm((N_A_LAYERS, DEC_BATCH, CONV_W - 1, CONV_DIM), 1.0)
    cache_ckv = nrm((N_C_LAYERS, N_POOL, PAGE_SIZE, MLA_KV_LORA), 1.0)
    cache_krope = nrm((N_C_LAYERS, N_POOL, PAGE_SIZE, MLA_ROPE), 1.0)
    page_table = jax.random.permutation(next(ks), N_POOL)[: DEC_BATCH * N_PAGES].reshape(DEC_BATCH, N_PAGES).astype(jnp.int32)
    norm_mix = gain((DEPTH, D_MODEL))
    norm_ffn = gain((DEPTH, D_MODEL))
    norm_final = gain((D_MODEL,))
    w_in_ab = nrm((N_A_LAYERS, D_MODEL, D_IN_AB), D_MODEL ** -0.5)
    w_out_ab = nrm((N_A_LAYERS, D_MIX_AB, D_MODEL), D_MIX_AB ** -0.5)
    hgrn_lb = nrm((N_A_LAYERS + 1, HG_QK), 0.1)
    hgrn_norm = gain((N_A_LAYERS, HG_V))
    conv_w = nrm((N_A_LAYERS, CONV_W, CONV_DIM), CONV_W ** -0.5)
    conv_b = nrm((N_A_LAYERS, CONV_DIM), 0.02)
    dt0 = jnp.exp(jax.random.uniform(next(ks), (N_A_LAYERS, SSM_HEADS), f32, math.log(1e-3), math.log(1e-1)))
    dt_bias = dt0 + jnp.log(-jnp.expm1(-dt0))
    a_log = jnp.log(jax.random.uniform(next(ks), (N_A_LAYERS, SSM_HEADS), f32, 1.0, 16.0))
    d_skip = 1.0 + nrm((N_A_LAYERS, SSM_HEADS), 0.1)
    ssm_norm = gain((N_A_LAYERS, SSM_D_INNER))
    w_in_c = nrm((N_C_LAYERS, D_MODEL, D_IN_C), D_MODEL ** -0.5)
    q_norm = gain((N_C_LAYERS, MLA_Q_LORA))
    kv_norm = gain((N_C_LAYERS, MLA_KV_LORA))
    w_uq = nrm((N_C_LAYERS, MLA_Q_LORA, MLA_HEADS * (MLA_NOPE + MLA_ROPE)), MLA_Q_LORA ** -0.5)
    w_uk = nrm((N_C_LAYERS, MLA_KV_LORA, MLA_HEADS, MLA_NOPE), MLA_KV_LORA ** -0.5)
    w_uv = nrm((N_C_LAYERS, MLA_KV_LORA, MLA_HEADS, MLA_V), MLA_KV_LORA ** -0.5)
    w_out_c = nrm((N_C_LAYERS, MLA_HEADS * MLA_V, D_MODEL), (MLA_HEADS * MLA_V) ** -0.5)
    w_route_group = nrm((DEPTH, D_MODEL, N_GROUPS), D_MODEL ** -0.5)
    b_route_group = nrm((DEPTH, N_GROUPS), 0.01)
    w_route_expert = nrm((DEPTH, D_MODEL, N_EXPERTS), D_MODEL ** -0.5)
    b_route_expert = nrm((DEPTH, N_EXPERTS), 0.01)
    w_gate = nrm((DEPTH, N_EXPERTS, D_MODEL, D_EXPERT), D_MODEL ** -0.5)
    w_up = nrm((DEPTH, N_EXPERTS, D_MODEL, D_EXPERT), D_MODEL ** -0.5)
    w_down = nrm((DEPTH, N_EXPERTS, D_EXPERT, D_MODEL), D_EXPERT ** -0.5)
    return {'x_prompt': x_prompt, 'x_sample': x_sample, 'state_hgrn': state_hgrn, 'state_ssm': state_ssm,
            'state_conv': state_conv, 'cache_ckv': cache_ckv, 'cache_krope': cache_krope, 'page_table': page_table,
            'norm_mix': norm_mix, 'norm_ffn': norm_ffn, 'norm_final': norm_final,
            'w_in_ab': w_in_ab, 'w_out_ab': w_out_ab, 'hgrn_lb': hgrn_lb, 'hgrn_norm': hgrn_norm,
            'conv_w': conv_w, 'conv_b': conv_b, 'dt_bias': dt_bias, 'a_log': a_log, 'd_skip': d_skip,
            'ssm_norm': ssm_norm, 'w_in_c': w_in_c, 'q_norm': q_norm, 'kv_norm': kv_norm, 'w_uq': w_uq,
            'w_uk': w_uk, 'w_uv': w_uv, 'w_out_c': w_out_c, 'w_route_group': w_route_group,
            'b_route_group': b_route_group, 'w_route_expert': w_route_expert, 'b_route_expert': b_route_expert,
            'w_gate': w_gate, 'w_up': w_up, 'w_down': w_down}


def reference(x_prompt, x_sample, state_hgrn, state_ssm, state_conv, cache_ckv, cache_krope, page_table,
              norm_mix, norm_ffn, norm_final, w_in_ab, w_out_ab, hgrn_lb, hgrn_norm, conv_w, conv_b,
              dt_bias, a_log, d_skip, ssm_norm, w_in_c, q_norm, kv_norm, w_uq, w_uk, w_uv, w_out_c,
              w_route_group, b_route_group, w_route_expert, b_route_expert, w_gate, w_up, w_down):
    lb_all = jnp.cumsum(jax.nn.softmax(hgrn_lb.astype(jnp.float32), axis=0), axis=0)[:N_A_LAYERS]
    bp = x_prompt.shape[0]
    hp, hs = x_prompt, x_sample
    hg_p, hg_s, ssm_p, ssm_s, cv_p, cv_s = [], [], [], [], [], []
    ckv_p, ckv_s, kr_p, kr_s = [], [], [], []
    for layer in range(DEPTH):
        j = layer // 2
        up = rmsnorm(hp, norm_mix[layer])
        us = rmsnorm(hs, norm_mix[layer])
        if layer % 2 == 0:
            wts = (w_in_ab[j], w_out_ab[j], hgrn_norm[j], conv_w[j], conv_b[j], dt_bias[j], a_log[j], d_skip[j], ssm_norm[j])
            mp, a1, a2, a3 = ab_mixer(up, lb_all[j],
                                      jnp.zeros((bp, HG_HEADS, HG_DK, HG_DV), jnp.float32),
                                      jnp.zeros((bp, SSM_HEADS, SSM_HEAD_DIM, SSM_STATE), jnp.float32),
                                      jnp.zeros((bp, CONV_W - 1, CONV_DIM), up.dtype), *wts)
            ms, b1, b2, b3 = ab_mixer(us, lb_all[j], state_hgrn[j], state_ssm[j], state_conv[j], *wts)
            hg_p.append(a1); ssm_p.append(a2); cv_p.append(a3)
            hg_s.append(b1); ssm_s.append(b2); cv_s.append(b3)
        else:
            wts = (w_in_c[j], q_norm[j], kv_norm[j], w_uq[j], w_uk[j], w_uv[j], w_out_c[j])
            mp, c1, c2 = mla_prompt(up, *wts)
            ms, d1, d2 = mla_sample(us, cache_ckv, cache_krope, page_table, j, *wts)
            ckv_p.append(c1); kr_p.append(c2)
            ckv_s.append(d1); kr_s.append(d2)
        hp = hp + mp
        hs = hs + ms
        moe_w = (w_route_group[layer], b_route_group[layer], w_route_expert[layer], b_route_expert[layer],
                 w_gate[layer], w_up[layer], w_down[layer])
        hp = hp + hier_moe(rmsnorm(hp, norm_ffn[layer]), *moe_w)
        hs = hs + hier_moe(rmsnorm(hs, norm_ffn[layer]), *moe_w)
    y_prompt = rmsnorm(hp, norm_final)
    y_sample = rmsnorm(hs, norm_final)
    hgrn_prompt = jnp.stack(hg_p)
    hgrn_sample = jnp.stack(hg_s)
    ssm_prompt = jnp.stack(ssm_p)
    ssm_sample = jnp.stack(ssm_s)
    conv_prompt = jnp.stack(cv_p)
    conv_sample = jnp.stack(cv_s)
    ckv_prompt = jnp.stack(ckv_p)
    ckv_sample = jnp.stack(ckv_s)
    krope_prompt = jnp.stack(kr_p)
    krope_sample = jnp.stack(kr_s)
    return (y_prompt, y_sample, hgrn_prompt, hgrn_sample, ssm_prompt, ssm_sample, conv_prompt, conv_sample,
            ckv_prompt, ckv_sample, krope_prompt, krope_sample)
```

```python
import functools
import math

import jax
import jax.numpy as jnp
import numpy as np
from jax import lax
from jax.experimental import pallas as pl
from jax.experimental.pallas import tpu as pltpu

F32 = jnp.float32
BF16 = jnp.bfloat16

EPS = 1e-6
D_MODEL = 1024
HG_HEADS = 4
HG_D = 128
HG_W = HG_HEADS * HG_D
SSM_HEADS = 16
SSM_P = 64
SSM_N = 128
SSM_GROUPS = 2
SSM_INNER = SSM_HEADS * SSM_P
CONV_W = 4
CONV_DIM = SSM_INNER + 2 * SSM_GROUPS * SSM_N
LANES = 128
SUBLANES = 8
VMEM_LIMIT = 48 * 1024 * 1024
NEG_BIG = -1e30


def _cparams(*sem):
    return pltpu.CompilerParams(dimension_semantics=sem, vmem_limit_bytes=VMEM_LIMIT)


def _dot(a, b):
    return jnp.dot(a, b, preferred_element_type=F32)


def _dot_nt(a, b):
    return lax.dot_general(a, b, (((1,), (1,)), ((), ())), preferred_element_type=F32)


def _dot_tn(a, b):
    return lax.dot_general(a, b, (((0,), (0,)), ((), ())), preferred_element_type=F32)


def _split3(x):
    hi = x.astype(BF16)
    r1 = x - hi.astype(F32)
    mid = r1.astype(BF16)
    lo = (r1 - mid.astype(F32)).astype(BF16)
    return hi, mid, lo


def _silu(x):
    return x * (1.0 / (1.0 + jnp.exp(-x)))


def _sigmoid(x):
    return 1.0 / (1.0 + jnp.exp(-x))


def _norm_matmul_kernel(x_ref, g_ref, w_ref, *out_refs, splits, normalize):
    x = x_ref[...].astype(F32)
    if normalize:
        ms = jnp.mean(x * x, axis=-1, keepdims=True)
        x = x * lax.rsqrt(ms + EPS) * g_ref[...]
    h = x.astype(BF16)
    for (a, b), o_ref in zip(splits, out_refs):
        o_ref[...] = _dot(h, w_ref[:, a:b]).astype(o_ref.dtype)


def norm_matmul(x, gain, w, splits, out_dtypes, *, tm, normalize=True):
    n, k = x.shape
    assert n % tm == 0
    kern = functools.partial(_norm_matmul_kernel, splits=tuple(splits), normalize=normalize)
    out_shape = [jax.ShapeDtypeStruct((n, b - a), dt) for (a, b), dt in zip(splits, out_dtypes)]
    out_specs = [pl.BlockSpec((tm, b - a), lambda i: (i, 0)) for (a, b) in splits]
    return pl.pallas_call(
        kern,
        grid=(n // tm,),
        in_specs=[pl.BlockSpec((tm, k), lambda i: (i, 0)),
                  pl.BlockSpec((1, k), lambda i: (0, 0)),
                  pl.BlockSpec(w.shape, lambda i: (0, 0))],
        out_specs=out_specs,
        out_shape=out_shape,
        compiler_params=_cparams("parallel"),
        name="norm_matmul",
    )(x, gain.reshape(1, k), w)


def _matmul_residual_kernel(*refs, n_pairs):
    res_ref = refs[0]
    o_ref = refs[1 + 2 * n_pairs]
    acc = res_ref[...]
    for j in range(n_pairs):
        a_ref, w_ref = refs[1 + 2 * j], refs[2 + 2 * j]
        acc = acc + _dot(a_ref[...].astype(BF16), w_ref[...])
    o_ref[...] = acc


def matmul_residual(res, pairs, *, tm):
    n, d = res.shape
    assert n % tm == 0
    in_specs = [pl.BlockSpec((tm, d), lambda i: (i, 0))]
    args = [res]
    for a, w in pairs:
        in_specs.append(pl.BlockSpec((tm, a.shape[1]), lambda i: (i, 0)))
        in_specs.append(pl.BlockSpec(w.shape, lambda i: (0, 0)))
        args += [a, w]
    return pl.pallas_call(
        functools.partial(_matmul_residual_kernel, n_pairs=len(pairs)),
        grid=(n // tm,),
        in_specs=in_specs,
        out_specs=pl.BlockSpec((tm, d), lambda i: (i, 0)),
        out_shape=jax.ShapeDtypeStruct((n, d), F32),
        compiler_params=_cparams("parallel"),
        name="matmul_residual",
    )(*args)


def _hgrn_level_halves(c):
    halves = []
    b = c // 2
    while b >= 1:
        halves.append(b)
        b //= 2
    return halves


@functools.lru_cache(maxsize=None)
def _hgrn_consts(c):
    t = np.arange(c)
    rows = [t[None, :] <= t[:, None], t[None, :] > t[:, None]]
    masks = []
    for b in _hgrn_level_halves(c):
        blk = t // b
        st = blk * b
        en = st + b - 1
        odd = blk % 2 == 1
        even = ~odd
        rows.append(odd[:, None] & (t[None, :] >= st[:, None]) & (t[None, :] <= t[:, None]))
        rows.append(even[:, None] & (t[None, :] > t[:, None]) & (t[None, :] <= en[:, None]))
        masks.append((t[:, None] // (2 * b) == t[None, :] // (2 * b)) & odd[:, None] & even[None, :])
    masks.append(np.eye(c, dtype=bool))
    rows.append(np.ones((SUBLANES, c), dtype=bool))
    dg = np.concatenate(rows, axis=0).astype(np.float32)
    dg3 = np.concatenate([dg, dg, dg], axis=1)
    mk = np.stack(masks).astype(np.float32)
    return dg3, mk


def _hgrn_kernel(q_ref, f_ref, i_ref, g_ref, lb_ref, nw_ref, dg_ref, mk_ref, s0_ref,
                 o_ref, sfin_ref, s_scr, *, c, t_valid, n_chunks):
    ci = pl.program_id(1)

    @pl.when(ci == 0)
    def _():
        s_scr[...] = s0_ref[...]

    lb = lb_ref[...]
    fl = f_ref[...]
    sig = _sigmoid(fl)
    logf = jnp.log(lb + (1.0 - lb) * sig)
    kk = (1.0 - lb) * (1.0 - sig)
    if t_valid < c:
        row = lax.broadcasted_iota(jnp.int32, (c, HG_W), 0)
        live = row < t_valid
        logf = jnp.where(live, logf, 0.0)
        kk = jnp.where(live, kk, 0.0)
    qa = _silu(q_ref[...])
    vv = i_ref[...].astype(BF16)

    lf3 = jnp.concatenate(_split3(logf), axis=0)
    ex = jnp.exp(_dot(dg_ref[...], lf3))
    halves = _hgrn_level_halves(c)
    n_lv = len(halves)
    e_cum = ex[0:c]
    e_rev = ex[c:2 * c]
    q_state = (qa * e_cum).astype(BF16)
    k_state = (kk * e_rev).astype(BF16)
    q_lv = [(qa * ex[(2 + 2 * l) * c:(3 + 2 * l) * c]).astype(BF16) for l in range(n_lv)]
    k_lv = [(kk * ex[(3 + 2 * l) * c:(4 + 2 * l) * c]).astype(BF16) for l in range(n_lv)]
    q_lv.append(qa.astype(BF16))
    k_lv.append(kk.astype(BF16))
    ones_t = jnp.ones((3 * c, HG_D), BF16)
    gate = _silu(g_ref[...])
    nw = nw_ref[...]

    for h in range(HG_HEADS):
        sl = slice(h * HG_D, (h + 1) * HG_D)
        sc = jnp.zeros((c, c), F32)
        for l in range(n_lv + 1):
            sc = sc + mk_ref[l] * _dot_nt(q_lv[l][:, sl], k_lv[l][:, sl])
        s_h = s_scr[h]
        o_h = _dot(sc.astype(BF16), vv[:, sl]) + _dot(q_state[:, sl], s_h.astype(BF16))
        dec = jnp.exp(_dot_tn(lf3[:, sl], ones_t))
        s_scr[h] = s_h * dec + _dot_tn(k_state[:, sl], vv[:, sl])
        ms = jnp.mean(o_h * o_h, axis=-1, keepdims=True)
        o_h = o_h * lax.rsqrt(ms + EPS) * nw[:, sl] * gate[:, sl]
        o_ref[:, sl] = o_h.astype(o_ref.dtype)

    @pl.when(ci == n_chunks - 1)
    def _():
        sfin_ref[...] = s_scr[...]


def hgrn_scan(q, f, i_in, g, lb, norm_w, s0, *, c, t_valid):
    b, t, _ = q.shape
    n_chunks = t // c
    assert t % c == 0 and (t_valid == c or n_chunks == 1)
    dg3, mk = _hgrn_consts(c)
    dg3 = jnp.asarray(dg3, BF16)
    mk = jnp.asarray(mk, F32)
    tok = pl.BlockSpec((None, c, HG_W), lambda bi, ci: (bi, ci, 0))
    st = pl.BlockSpec((None, HG_HEADS, HG_D, HG_D), lambda bi, ci: (bi, 0, 0, 0))
    row = pl.BlockSpec((1, HG_W), lambda bi, ci: (0, 0))
    kern = functools.partial(_hgrn_kernel, c=c, t_valid=t_valid, n_chunks=n_chunks)
    return pl.pallas_call(
        kern,
        grid=(b, n_chunks),
        in_specs=[tok, tok, tok, tok, row, row,
                  pl.BlockSpec(dg3.shape, lambda bi, ci: (0, 0)),
                  pl.BlockSpec(mk.shape, lambda bi, ci: (0, 0, 0)),
                  st],
        out_specs=[tok, st],
        out_shape=[jax.ShapeDtypeStruct((b, t, HG_W), BF16),
                   jax.ShapeDtypeStruct((b, HG_HEADS, HG_D, HG_D), F32)],
        scratch_shapes=[pltpu.VMEM((HG_HEADS, HG_D, HG_D), F32)],
        compiler_params=_cparams("parallel", "arbitrary"),
        name="hgrn_scan",
    )(q, f, i_in, g, lb.reshape(1, HG_W), norm_w.reshape(1, HG_W), dg3, mk, s0)


CONV_HEAD = 8


@functools.lru_cache(maxsize=None)
def _ssd_consts(c):
    t = np.arange(c)
    tri = (t[None, :] <= t[:, None]).astype(np.float32)
    rev = (t[None, :] > t[:, None]).astype(np.float32)
    ones = np.ones((SUBLANES, c), np.float32)
    tg = np.concatenate([tri, rev, ones], axis=0)
    tg3 = np.concatenate([tg, tg, tg], axis=1)
    u3 = np.concatenate([tri.T, tri.T, tri.T], axis=0)
    return tg3, u3


def _ssd_kernel(z_ref, xbc_ref, dt_ref, cbuf_ref, cw_ref, cb_ref, dtb_ref, alog_ref, dsk_ref, nw_ref,
                tg_ref, u_ref, h0_ref,
                y_ref, cnew_ref, hfin_ref, xp_scr, h_scr, y_scr, *, c, t_valid, n_chunks):
    ci = pl.program_id(1)

    @pl.when(ci == 0)
    def _():
        h_scr[...] = h0_ref[...]
        xp_scr[CONV_HEAD - (CONV_W - 1):CONV_HEAD, :] = cbuf_ref[...]

    xp_scr[CONV_HEAD:CONV_HEAD + c, :] = xbc_ref[...]
    conv = cb_ref[...]
    for i in range(CONV_W):
        off = CONV_HEAD - (CONV_W - 1) + i
        conv = conv + xp_scr[off:off + c, :] * cw_ref[i:i + 1, :]
    conv = _silu(conv)

    @pl.when(ci == n_chunks - 1)
    def _():
        cnew_ref[...] = xp_scr[CONV_HEAD + t_valid - (CONV_W - 1):CONV_HEAD + t_valid, :]

    xp_scr[CONV_HEAD - (CONV_W - 1):CONV_HEAD, :] = xp_scr[CONV_HEAD + c - (CONV_W - 1):CONV_HEAD + c, :]

    xs = conv[:, :SSM_INNER]
    dt_raw = dt_ref[...] + dtb_ref[...]
    dt = jnp.maximum(dt_raw, 0.0) + jnp.log(1.0 + jnp.exp(-jnp.abs(dt_raw)))
    if t_valid < c:
        row = lax.broadcasted_iota(jnp.int32, (c, LANES), 0)
        dt = jnp.where(row < t_valid, dt, 0.0)
    a = -jnp.exp(alog_ref[...])
    da = dt * a
    da3 = jnp.concatenate(_split3(da), axis=0)
    xx = _dot(tg_ref[...], da3)
    cum = xx[0:c]
    e_cum = jnp.exp(cum)
    w_all = jnp.exp(xx[c:2 * c]) * dt
    e_last = jnp.exp(xx[2 * c:2 * c + 1])
    cum_t = _dot_tn(da3, u_ref[...])

    tril = lax.broadcasted_iota(jnp.int32, (c, c), 0) >= lax.broadcasted_iota(jnp.int32, (c, c), 1)
    heads_per_group = SSM_HEADS // SSM_GROUPS
    for g in range(SSM_GROUPS):
        bm = conv[:, SSM_INNER + g * SSM_N:SSM_INNER + (g + 1) * SSM_N]
        cm = conv[:, SSM_INNER + (SSM_GROUPS + g) * SSM_N:SSM_INNER + (SSM_GROUPS + g + 1) * SSM_N]
        bm_b = bm.astype(BF16)
        cb = _dot_nt(cm.astype(BF16), bm_b)
        for hh in range(heads_per_group):
            h = g * heads_per_group + hh
            x_h = xs[:, h * SSM_P:(h + 1) * SSM_P]
            diff = cum[:, h:h + 1] - cum_t[h:h + 1, :]
            seg = jnp.exp(jnp.where(tril, diff, NEG_BIG))
            xd = (dt[:, h:h + 1] * x_h).astype(BF16)
            cme = (cm * e_cum[:, h:h + 1]).astype(BF16)
            h_h = h_scr[h]
            y_h = _dot((cb * seg).astype(BF16), xd) + _dot_nt(cme, h_h.astype(BF16))
            xw = (w_all[:, h:h + 1] * x_h).astype(BF16)
            h_scr[h] = h_h * e_last[:, h:h + 1] + _dot_tn(xw, bm_b)
            y_scr[:, h * SSM_P:(h + 1) * SSM_P] = y_h

    y = (y_scr[...] + dsk_ref[...] * xs) * _silu(z_ref[...])
    gw = SSM_INNER // SSM_GROUPS
    for g in range(SSM_GROUPS):
        yg = y[:, g * gw:(g + 1) * gw]
        ms = jnp.mean(yg * yg, axis=-1, keepdims=True)
        y_ref[:, g * gw:(g + 1) * gw] = (yg * lax.rsqrt(ms + EPS) * nw_ref[:, g * gw:(g + 1) * gw]).astype(y_ref.dtype)

    @pl.when(ci == n_chunks - 1)
    def _():
        hfin_ref[...] = h_scr[...]


def ssd_scan(z, xbc, dt, conv_buf, h0, conv_w, conv_b, dt_bias, a_log, d_skip, norm_w, *, c, t_valid):
    b, t, _ = z.shape
    n_chunks = t // c
    assert t % c == 0 and (t_valid == c or n_chunks == 1)
    tg3, u3 = _ssd_consts(c)
    tg3 = jnp.asarray(tg3, BF16)
    u3 = jnp.asarray(u3, BF16)

    def pad_heads(v):
        return jnp.pad(v.astype(F32), (0, LANES - SSM_HEADS)).reshape(1, LANES)

    def tok(w):
        return pl.BlockSpec((None, c, w), lambda bi, ci: (bi, ci, 0))

    def const(shape):
        return pl.BlockSpec(shape, lambda bi, ci: (0,) * len(shape))

    cst = pl.BlockSpec((None, CONV_W - 1, CONV_DIM), lambda bi, ci: (bi, 0, 0))
    hst = pl.BlockSpec((None, SSM_HEADS, SSM_P, SSM_N), lambda bi, ci: (bi, 0, 0, 0))
    kern = functools.partial(_ssd_kernel, c=c, t_valid=t_valid, n_chunks=n_chunks)
    return pl.pallas_call(
        kern,
        grid=(b, n_chunks),
        in_specs=[tok(SSM_INNER), tok(CONV_DIM), tok(LANES), cst,
                  const((CONV_W, CONV_DIM)), const((1, CONV_DIM)), const((1, LANES)), const((1, LANES)),
                  const((1, SSM_INNER)), const((1, SSM_INNER)), const(tg3.shape), const(u3.shape), hst],
        out_specs=[tok(SSM_INNER), cst, hst],
        out_shape=[jax.ShapeDtypeStruct((b, t, SSM_INNER), BF16),
                   jax.ShapeDtypeStruct((b, CONV_W - 1, CONV_DIM), F32),
                   jax.ShapeDtypeStruct((b, SSM_HEADS, SSM_P, SSM_N), F32)],
        scratch_shapes=[pltpu.VMEM((CONV_HEAD + c, CONV_DIM), F32),
                        pltpu.VMEM((SSM_HEADS, SSM_P, SSM_N), F32),
                        pltpu.VMEM((c, SSM_INNER), F32)],
        compiler_params=_cparams("parallel", "arbitrary"),
        name="ssd_scan",
    )(z, xbc, dt, conv_buf, conv_w, conv_b.reshape(1, CONV_DIM), pad_heads(dt_bias), pad_heads(a_log),
      jnp.repeat(d_skip.astype(F32), SSM_P).reshape(1, SSM_INNER), norm_w.reshape(1, SSM_INNER), tg3, u3, h0)


N_GROUPS = 4
EXPERTS_PER_GROUP = 8
N_EXPERTS = N_GROUPS * EXPERTS_PER_GROUP
TOP_K = 2
D_EXPERT = 256
MOE_TM = 256


def _router_kernel(x_ref, g_ref, w_ref, b_ref, xn_ref, rw_ref, ri_ref):
    x = x_ref[...]
    ms = jnp.mean(x * x, axis=-1, keepdims=True)
    xn = x * lax.rsqrt(ms + EPS) * g_ref[...]
    xn_ref[...] = xn.astype(xn_ref.dtype)
    logits = jnp.dot(xn, w_ref[...], precision=lax.Precision.HIGHEST, preferred_element_type=F32) + b_ref[...]
    lane = lax.broadcasted_iota(jnp.int32, logits.shape, 1)
    is_g = (lane >= N_EXPERTS) & (lane < N_EXPERTS + N_GROUPS)
    gl = jnp.where(is_g, logits, NEG_BIG)
    gmax = jnp.max(gl, axis=-1, keepdims=True)
    g_sel = jnp.min(jnp.where(gl == gmax, lane, 4 * LANES), axis=-1, keepdims=True) - N_EXPERTS
    g_w = 1.0 / jnp.sum(jnp.where(is_g, jnp.exp(gl - gmax), 0.0), axis=-1, keepdims=True)
    lo = g_sel * EXPERTS_PER_GROUP
    in_grp = (lane >= lo) & (lane < lo + EXPERTS_PER_GROUP)
    el = jnp.where(in_grp, logits, NEG_BIG)
    emax = jnp.max(el, axis=-1, keepdims=True)
    ee = jnp.where(in_grp, jnp.exp(el - emax), 0.0)
    p = ee / jnp.sum(ee, axis=-1, keepdims=True)
    p = jnp.where(in_grp, p, -1.0)
    p1 = jnp.max(p, axis=-1, keepdims=True)
    i1 = jnp.min(jnp.where(p == p1, lane, 4 * LANES), axis=-1, keepdims=True)
    p_rest = jnp.where(lane == i1, -1.0, p)
    p2 = jnp.max(p_rest, axis=-1, keepdims=True)
    i2 = jnp.min(jnp.where(p_rest == p2, lane, 4 * LANES), axis=-1, keepdims=True)
    w1 = p1 / (p1 + p2) * g_w
    w2 = p2 / (p1 + p2) * g_w
    rw_ref[...] = jnp.where(lane == 0, w1, jnp.where(lane == 1, w2, 0.0))
    ri_ref[...] = jnp.where(lane == 0, i1, jnp.where(lane == 1, i2, 0))


def moe_router(x, gain, w_rg, b_rg, w_re, b_re, *, tm):
    n, d = x.shape
    w = jnp.zeros((d, LANES), F32).at[:, :N_EXPERTS].set(w_re).at[:, N_EXPERTS:N_EXPERTS + N_GROUPS].set(w_rg)
    b = jnp.zeros((1, LANES), F32).at[0, :N_EXPERTS].set(b_re).at[0, N_EXPERTS:N_EXPERTS + N_GROUPS].set(b_rg)
    return pl.pallas_call(
        _router_kernel,
        grid=(n // tm,),
        in_specs=[pl.BlockSpec((tm, d), lambda i: (i, 0)),
                  pl.BlockSpec((1, d), lambda i: (0, 0)),
                  pl.BlockSpec((d, LANES), lambda i: (0, 0)),
                  pl.BlockSpec((1, LANES), lambda i: (0, 0))],
        out_specs=[pl.BlockSpec((tm, d), lambda i: (i, 0)),
                   pl.BlockSpec((tm, LANES), lambda i: (i, 0)),
                   pl.BlockSpec((tm, LANES), lambda i: (i, 0))],
        out_shape=[jax.ShapeDtypeStruct((n, d), BF16),
                   jax.ShapeDtypeStruct((n, LANES), F32),
                   jax.ShapeDtypeStruct((n, LANES), jnp.int32)],
        compiler_params=_cparams("parallel"),
        name="moe_router",
    )(x, gain.reshape(1, d), w, b)


def _moe_ffn_kernel(te_ref, nu_ref, x_ref, gw_ref, wg_ref, wu_ref, wd_ref, y_ref):
    i = pl.program_id(0)

    @pl.when(i < nu_ref[0])
    def _():
        x = x_ref[...]
        a = _dot(x, wg_ref[...].astype(BF16))
        b = _dot(x, wu_ref[...].astype(BF16))
        hid = (_silu(a) * b * gw_ref[...]).astype(BF16)
        y_ref[...] = _dot(hid, wd_ref[...].astype(BF16)).astype(y_ref.dtype)

    @pl.when(i >= nu_ref[0])
    def _():
        y_ref[...] = jnp.zeros_like(y_ref)


def moe_ffn(tile_expert, n_used, x_sorted, gate_sorted, w_gate, w_up, w_down):
    r, d = x_sorted.shape
    n_tiles = r // MOE_TM
    grid_spec = pltpu.PrefetchScalarGridSpec(
        num_scalar_prefetch=2,
        grid=(n_tiles,),
        in_specs=[pl.BlockSpec((MOE_TM, d), lambda i, te, nu: (i, 0)),
                  pl.BlockSpec((MOE_TM, 1), lambda i, te, nu: (i, 0)),
                  pl.BlockSpec((None, d, D_EXPERT), lambda i, te, nu: (te[i], 0, 0)),
                  pl.BlockSpec((None, d, D_EXPERT), lambda i, te, nu: (te[i], 0, 0)),
                  pl.BlockSpec((None, D_EXPERT, d), lambda i, te, nu: (te[i], 0, 0))],
        out_specs=pl.BlockSpec((MOE_TM, d), lambda i, te, nu: (i, 0)),
    )
    return pl.pallas_call(
        _moe_ffn_kernel,
        grid_spec=grid_spec,
        out_shape=jax.ShapeDtypeStruct((r, d), BF16),
        compiler_params=_cparams("arbitrary"),
        name="moe_ffn",
    )(tile_expert, n_used, x_sorted, gate_sorted, w_gate, w_up, w_down)


def _moe_plan(eid):
    n = eid.shape[0]
    flat = eid.reshape(-1)
    onehot = (flat[:, None] == jnp.arange(N_EXPERTS, dtype=jnp.int32)[None, :]).astype(jnp.int32)
    csum = jnp.cumsum(onehot, axis=0)
    counts = csum[-1]
    rank = jnp.sum((csum - 1) * onehot, axis=1)
    tiles = (counts + MOE_TM - 1) // MOE_TM
    tile_end = jnp.cumsum(tiles)
    tile_start = tile_end - tiles
    dest = tile_start[flat] * MOE_TM + rank
    n_rows = TOP_K * n + N_EXPERTS * MOE_TM
    n_tiles = n_rows // MOE_TM
    src_token = jnp.zeros((n_rows,), jnp.int32).at[dest].set(jnp.arange(TOP_K * n, dtype=jnp.int32) // TOP_K)
    n_used = tile_end[-1]
    t_idx = jnp.minimum(jnp.arange(n_tiles, dtype=jnp.int32), n_used - 1)
    tile_expert = jnp.sum((t_idx[:, None] >= tile_end[None, :]).astype(jnp.int32), axis=1)
    return dest.reshape(n, TOP_K), src_token, tile_expert.astype(jnp.int32), n_used.reshape(1).astype(jnp.int32)


def hier_moe_block(x, gain, w_rg, b_rg, w_re, b_re, w_gate, w_up, w_down):
    n, d = x.shape
    xn, rw, ri = moe_router(x, gain, w_rg, b_rg, w_re, b_re, tm=512)
    eid = ri[:, :TOP_K]
    gate = rw[:, :TOP_K]
    dest, src_token, tile_expert, n_used = _moe_plan(eid)
    n_rows = src_token.shape[0]
    x_sorted = jnp.take(xn, src_token, axis=0)
    gate_sorted = jnp.zeros((n_rows,), F32).at[dest.reshape(-1)].set(gate.reshape(-1)).reshape(n_rows, 1)
    y_sorted = moe_ffn(tile_expert, n_used, x_sorted, gate_sorted, w_gate, w_up, w_down)
    y = jnp.take(y_sorted, dest.reshape(-1), axis=0).reshape(n, TOP_K, d).astype(F32)
    return x + y[:, 0] + y[:, 1]


MLA_HEADS = 16
MLA_LORA = 256
MLA_NOPE = 64
MLA_ROPE = 32
MLA_V = 64
MLA_SCALE = (MLA_NOPE + MLA_ROPE) ** -0.5
ROPE_THETA = 10000.0
HEAD_PAD = 128
ROPE_AT = MLA_NOPE
Q_DEC = 384


def _rope_tables(pos):
    half = MLA_ROPE // 2
    inv = ROPE_THETA ** (-jnp.arange(half, dtype=F32) / half)
    ang = pos.astype(F32)[:, None] * inv[None, :]
    cos, sin = jnp.cos(ang), jnp.sin(ang)
    n = pos.shape[0]
    ones = jnp.ones((n, MLA_NOPE), F32)
    zeros_n = jnp.zeros((n, MLA_NOPE), F32)
    zeros_p = jnp.zeros((n, HEAD_PAD - MLA_NOPE - MLA_ROPE), F32)
    ctab = jnp.concatenate([ones, cos, cos, zeros_p], axis=1)
    stab = jnp.concatenate([zeros_n, sin, sin, zeros_p], axis=1)
    return ctab, stab


def _mla_weights(w_in, w_uq, w_uk, w_uv):
    d = w_in.shape[0]
    half = MLA_ROPE // 2
    w_kr = w_in[:, 2 * MLA_LORA:]
    zl = jnp.zeros((d, ROPE_AT), F32)
    zr = jnp.zeros((d, HEAD_PAD - ROPE_AT - MLA_ROPE), F32)
    kr_a = jnp.concatenate([zl, w_kr, zr], axis=1)
    kr_b = jnp.concatenate([zl, -w_kr[:, half:], w_kr[:, :half], zr], axis=1)
    w_in_p = jnp.concatenate([w_in[:, :2 * MLA_LORA], kr_a, kr_b], axis=1).astype(BF16)
    wq = w_uq.reshape(MLA_LORA, MLA_HEADS, MLA_NOPE + MLA_ROPE)
    nope, x1, x2 = wq[..., :MLA_NOPE], wq[..., MLA_NOPE:MLA_NOPE + half], wq[..., MLA_NOPE + half:]
    zp = jnp.zeros((MLA_LORA, MLA_HEADS, HEAD_PAD - MLA_NOPE - MLA_ROPE), F32)
    wq_a = jnp.concatenate([nope, x1, x2, zp], axis=-1).reshape(MLA_LORA, MLA_HEADS * HEAD_PAD).astype(BF16)
    wq_b = jnp.concatenate([jnp.zeros_like(nope), -x2, x1, zp], axis=-1).reshape(MLA_LORA, MLA_HEADS * HEAD_PAD).astype(BF16)
    zk = jnp.zeros((MLA_LORA, MLA_HEADS, HEAD_PAD - MLA_NOPE), F32)
    wuk_p = jnp.concatenate([w_uk, zk], axis=-1).reshape(MLA_LORA, MLA_HEADS * HEAD_PAD).astype(BF16)
    wuv = w_uv.reshape(MLA_LORA, MLA_HEADS * MLA_V).astype(BF16)
    absorb = jnp.transpose(w_uk, (1, 2, 0))
    sel = jnp.zeros((MLA_ROPE, Q_DEC - MLA_LORA), F32).at[jnp.arange(MLA_ROPE), jnp.arange(MLA_ROPE)].set(1.0)
    top = jnp.concatenate([absorb, jnp.zeros((MLA_HEADS, MLA_NOPE, Q_DEC - MLA_LORA), F32)], axis=-1)
    mid = jnp.broadcast_to(jnp.concatenate([jnp.zeros((MLA_ROPE, MLA_LORA), F32), sel], axis=-1)[None],
                           (MLA_HEADS, MLA_ROPE, Q_DEC))
    bot = jnp.zeros((MLA_HEADS, HEAD_PAD - MLA_NOPE - MLA_ROPE, Q_DEC), F32)
    w_dec = jnp.concatenate([top, mid, bot], axis=1).astype(BF16)
    return w_in_p, wq_a, wq_b, wuk_p, wuv, w_dec


def _mla_q_kernel(cq_ref, ckv_ref, kra_ref, krb_ref, ct_ref, st_ref, qn_ref, kvn_ref, wa_ref, wb_ref,
                  q_ref, ckvn_ref, krot_ref):
    cq = cq_ref[...]
    ms = jnp.mean(cq * cq, axis=-1, keepdims=True)
    cqn = (cq * lax.rsqrt(ms + EPS) * qn_ref[...]).astype(BF16)
    ckv = ckv_ref[...]
    ms2 = jnp.mean(ckv * ckv, axis=-1, keepdims=True)
    ckvn_ref[...] = ckv * lax.rsqrt(ms2 + EPS) * kvn_ref[...]
    ct = ct_ref[...]
    st = st_ref[...]
    krot_ref[...] = kra_ref[...] * ct + krb_ref[...] * st
    for h in range(MLA_HEADS):
        sl = slice(h * HEAD_PAD, (h + 1) * HEAD_PAD)
        qh = _dot(cqn, wa_ref[:, sl]) * ct + _dot(cqn, wb_ref[:, sl]) * st
        q_ref[:, sl] = (qh * MLA_SCALE).astype(q_ref.dtype)


def mla_q(cq, ckv, kr_a, kr_b, ctab, stab, q_norm, kv_norm, wq_a, wq_b, *, tm):
    n = cq.shape[0]
    hw = MLA_HEADS * HEAD_PAD

    def tok(w):
        return pl.BlockSpec((tm, w), lambda i: (i, 0))

    def const(shape):
        return pl.BlockSpec(shape, lambda i: (0, 0))

    return pl.pallas_call(
        _mla_q_kernel,
        grid=(n // tm,),
        in_specs=[tok(MLA_LORA), tok(MLA_LORA), tok(HEAD_PAD), tok(HEAD_PAD), tok(HEAD_PAD), tok(HEAD_PAD),
                  const((1, MLA_LORA)), const((1, MLA_LORA)), const(wq_a.shape), const(wq_b.shape)],
        out_specs=[tok(hw), tok(MLA_LORA), tok(HEAD_PAD)],
        out_shape=[jax.ShapeDtypeStruct((n, hw), BF16),
                   jax.ShapeDtypeStruct((n, MLA_LORA), F32),
                   jax.ShapeDtypeStruct((n, HEAD_PAD), F32)],
        compiler_params=_cparams("parallel"),
        name="mla_q",
    )(cq, ckv, kr_a, kr_b, ctab, stab, q_norm.reshape(1, MLA_LORA), kv_norm.reshape(1, MLA_LORA), wq_a, wq_b)


def _mla_kv_kernel(ckvn_ref, krot_ref, wk_ref, wv_ref, k_ref, v_ref):
    c = ckvn_ref[...].astype(BF16)
    krot = krot_ref[...]
    for h in range(MLA_HEADS):
        sl = slice(h * HEAD_PAD, (h + 1) * HEAD_PAD)
        k_ref[:, sl] = (_dot(c, wk_ref[:, sl]) + krot).astype(k_ref.dtype)
    v_ref[...] = _dot(c, wv_ref[...]).astype(v_ref.dtype)


def mla_kv(ckvn, krot, wuk_p, wuv, *, n, tm):
    hw = MLA_HEADS * HEAD_PAD
    vw = MLA_HEADS * MLA_V
    return pl.pallas_call(
        _mla_kv_kernel,
        grid=(n // tm,),
        in_specs=[pl.BlockSpec((tm, MLA_LORA), lambda i: (i, 0)),
                  pl.BlockSpec((tm, HEAD_PAD), lambda i: (i, 0)),
                  pl.BlockSpec(wuk_p.shape, lambda i: (0, 0)),
                  pl.BlockSpec(wuv.shape, lambda i: (0, 0))],
        out_specs=[pl.BlockSpec((tm, hw), lambda i: (i, 0)),
                   pl.BlockSpec((tm, vw), lambda i: (i, 0))],
        out_shape=[jax.ShapeDtypeStruct((n, hw), BF16),
                   jax.ShapeDtypeStruct((n, vw), BF16)],
        compiler_params=_cparams("parallel"),
        name="mla_kv",
    )(ckvn, krot, wuk_p, wuv)


def _flash_kernel(qi_ref, ki_ref, q_ref, k_ref, v_ref, o_ref, m_scr, l_scr, acc_scr, *, tq, tk):
    p_idx = pl.program_id(2)
    qi = qi_ref[p_idx]
    ki = ki_ref[p_idx]

    @pl.when(ki == 0)
    def _():
        m_scr[...] = jnp.full_like(m_scr, NEG_BIG)
        l_scr[...] = jnp.zeros_like(l_scr)
        acc_scr[...] = jnp.zeros_like(acc_scr)

    def step(masked):
        for hh in range(2):
            q = q_ref[:, hh * HEAD_PAD:(hh + 1) * HEAD_PAD]
            k = k_ref[:, hh * HEAD_PAD:(hh + 1) * HEAD_PAD]
            s = _dot_nt(q, k)
            if masked:
                row = lax.broadcasted_iota(jnp.int32, (tq, tk), 0)
                col = lax.broadcasted_iota(jnp.int32, (tq, tk), 1)
                s = jnp.where(row >= col, s, NEG_BIG)
            m_old = m_scr[hh]
            m_new = jnp.maximum(m_old, jnp.max(s, axis=-1, keepdims=True))
            alpha = jnp.exp(m_old - m_new)
            p = jnp.exp(s - m_new)
            l_scr[hh] = alpha * l_scr[hh] + jnp.sum(p, axis=-1, keepdims=True)
            m_scr[hh] = m_new
            pv = _dot(p.astype(BF16), v_ref[:, hh * MLA_V:(hh + 1) * MLA_V])
            acc_scr[hh] = alpha * acc_scr[hh] + pv

    @pl.when(ki < qi)
    def _():
        step(False)

    @pl.when(ki == qi)
    def _():
        step(True)
        for hh in range(2):
            o_ref[:, hh * MLA_V:(hh + 1) * MLA_V] = (acc_scr[hh] / l_scr[hh]).astype(o_ref.dtype)


def flash_attention(q, k, v, *, tq):
    b, t, _ = q.shape
    tk = tq
    nq = t // tq
    pairs = [(i, j) for i in range(nq) for j in range(i + 1)]
    qi = jnp.asarray([p[0] for p in pairs], jnp.int32)
    ki = jnp.asarray([p[1] for p in pairs], jnp.int32)
    n_hp = MLA_HEADS // 2
    grid_spec = pltpu.PrefetchScalarGridSpec(
        num_scalar_prefetch=2,
        grid=(b, n_hp, len(pairs)),
        in_specs=[pl.BlockSpec((None, tq, 2 * HEAD_PAD), lambda bi, hp, p, qi, ki: (bi, qi[p], hp)),
                  pl.BlockSpec((None, tk, 2 * HEAD_PAD), lambda bi, hp, p, qi, ki: (bi, ki[p], hp)),
                  pl.BlockSpec((None, tk, 2 * MLA_V), lambda bi, hp, p, qi, ki: (bi, ki[p], hp))],
        out_specs=pl.BlockSpec((None, tq, 2 * MLA_V), lambda bi, hp, p, qi, ki: (bi, qi[p], hp)),
        scratch_shapes=[pltpu.VMEM((2, tq, 1), F32), pltpu.VMEM((2, tq, 1), F32),
                        pltpu.VMEM((2, tq, MLA_V), F32)],
    )
    return pl.pallas_call(
        functools.partial(_flash_kernel, tq=tq, tk=tk),
        grid_spec=grid_spec,
        out_shape=jax.ShapeDtypeStruct((b, t, MLA_HEADS * MLA_V), BF16),
        compiler_params=_cparams("parallel", "parallel", "arbitrary"),
        name="mla_flash",
    )(qi, ki, q, k, v)


PAGE = 128
DEC_PB = 16
NEW_PAD = 8


def _q_dec_kernel(q_ref, w_ref, o_ref):
    for h in range(MLA_HEADS):
        o_ref[:, h * Q_DEC:(h + 1) * Q_DEC] = _dot(q_ref[:, h * HEAD_PAD:(h + 1) * HEAD_PAD], w_ref[h]).astype(o_ref.dtype)


def mla_q_dec(q, w_dec):
    n = q.shape[0]
    return pl.pallas_call(
        _q_dec_kernel,
        grid=(1,),
        in_specs=[pl.BlockSpec(q.shape, lambda i: (0, 0)), pl.BlockSpec(w_dec.shape, lambda i: (0, 0, 0))],
        out_specs=pl.BlockSpec((n, MLA_HEADS * Q_DEC), lambda i: (0, 0)),
        out_shape=jax.ShapeDtypeStruct((n, MLA_HEADS * Q_DEC), BF16),
        compiler_params=_cparams("arbitrary"),
        name="mla_q_dec",
    )(q, w_dec)


def _decode_kernel(pt_ref, q_ref, cnew_ref, rnew_ref, cache_c, cache_r, o_ref, cbuf, rbuf, sem,
                   *, layer, n_pages, t_new):
    b = pl.program_id(0)
    n_blk = n_pages // DEC_PB
    rows = q_ref.shape[0]

    def page_copies(j, slot, i):
        pg = pt_ref[b, j * DEC_PB + i]
        cc = pltpu.make_async_copy(cache_c.at[layer, pg], cbuf.at[slot, pl.ds(i * PAGE, PAGE)], sem.at[0, slot])
        cr = pltpu.make_async_copy(cache_r.at[layer, pg], rbuf.at[slot, pl.ds(i * PAGE, PAGE)], sem.at[1, slot])
        return cc, cr

    def start_block(j, slot):
        def body(i, carry):
            cc, cr = page_copies(j, slot, i)
            cc.start()
            cr.start()
            return carry
        lax.fori_loop(0, DEC_PB, body, 0)

    def wait_block(j, slot):
        def body(i, carry):
            cc, cr = page_copies(j, slot, i)
            cc.wait()
            cr.wait()
            return carry
        lax.fori_loop(0, DEC_PB, body, 0)

    q = q_ref[...]
    ql = q[:, :MLA_LORA]
    qr = q[:, MLA_LORA:MLA_LORA + MLA_ROPE]
    start_block(0, 0)

    def block(j, carry):
        m, l, acc = carry
        slot = j % 2

        @pl.when(j + 1 < n_blk)
        def _():
            start_block(j + 1, 1 - slot)

        wait_block(j, slot)
        kc = cbuf[slot].astype(BF16)
        kr = rbuf[slot].astype(BF16)
        s = _dot_nt(ql, kc) + _dot_nt(qr, kr)
        m_new = jnp.maximum(m, jnp.max(s, axis=-1, keepdims=True))
        alpha = jnp.exp(m - m_new)
        p = jnp.exp(s - m_new)
        l = alpha * l + jnp.sum(p, axis=-1, keepdims=True)
        acc = alpha * acc + _dot(p.astype(BF16), kc)
        return m_new, l, acc

    init = (jnp.full((rows, 1), NEG_BIG, F32), jnp.zeros((rows, 1), F32), jnp.zeros((rows, MLA_LORA), F32))
    m, l, acc = lax.fori_loop(0, n_blk, block, init)

    cn = cnew_ref[...].astype(BF16)
    rn = rnew_ref[...].astype(BF16)
    s = _dot_nt(ql, cn) + _dot_nt(qr, rn)
    t_row = lax.broadcasted_iota(jnp.int32, s.shape, 0) // MLA_HEADS
    col = lax.broadcasted_iota(jnp.int32, s.shape, 1)
    s = jnp.where((col <= t_row) & (col < t_new), s, NEG_BIG)
    m_new = jnp.maximum(m, jnp.max(s, axis=-1, keepdims=True))
    alpha = jnp.exp(m - m_new)
    p = jnp.exp(s - m_new)
    l = alpha * l + jnp.sum(p, axis=-1, keepdims=True)
    acc = alpha * acc + _dot(p.astype(BF16), cn)
    o_ref[...] = (acc / l).astype(o_ref.dtype)


def mla_decode(page_table, q_dec, c_new, r_new, cache_ckv, cache_krope, *, layer, t_new):
    b, rows, _ = q_dec.shape
    n_pages = page_table.shape[1]
    assert n_pages % DEC_PB == 0
    grid_spec = pltpu.PrefetchScalarGridSpec(
        num_scalar_prefetch=1,
        grid=(b,),
        in_specs=[pl.BlockSpec((None, rows, Q_DEC), lambda i, pt: (i, 0, 0)),
                  pl.BlockSpec((None, NEW_PAD, MLA_LORA), lambda i, pt: (i, 0, 0)),
                  pl.BlockSpec((None, NEW_PAD, MLA_ROPE), lambda i, pt: (i, 0, 0)),
                  pl.BlockSpec(memory_space=pl.ANY),
                  pl.BlockSpec(memory_space=pl.ANY)],
        out_specs=pl.BlockSpec((None, rows, MLA_LORA), lambda i, pt: (i, 0, 0)),
        scratch_shapes=[pltpu.VMEM((2, DEC_PB * PAGE, MLA_LORA), F32),
                        pltpu.VMEM((2, DEC_PB * PAGE, MLA_ROPE), F32),
                        pltpu.SemaphoreType.DMA((2, 2))],
    )
    return pl.pallas_call(
        functools.partial(_decode_kernel, layer=layer, n_pages=n_pages, t_new=t_new),
        grid_spec=grid_spec,
        out_shape=jax.ShapeDtypeStruct((b, rows, MLA_LORA), BF16),
        compiler_params=_cparams("arbitrary"),
        name="mla_decode",
    )(page_table, q_dec, c_new, r_new, cache_ckv, cache_krope)


def _sample_v_kernel(o_ref, w_ref, y_ref):
    for h in range(MLA_HEADS):
        y_ref[:, h * MLA_V:(h + 1) * MLA_V] = _dot(o_ref[:, h * MLA_LORA:(h + 1) * MLA_LORA],
                                                   w_ref[:, h * MLA_V:(h + 1) * MLA_V]).astype(y_ref.dtype)


def mla_sample_v(o_lat, wuv):
    n = o_lat.shape[0]
    return pl.pallas_call(
        _sample_v_kernel,
        grid=(1,),
        in_specs=[pl.BlockSpec(o_lat.shape, lambda i: (0, 0)), pl.BlockSpec(wuv.shape, lambda i: (0, 0))],
        out_specs=pl.BlockSpec((n, MLA_HEADS * MLA_V), lambda i: (0, 0)),
        out_shape=jax.ShapeDtypeStruct((n, MLA_HEADS * MLA_V), BF16),
        compiler_params=_cparams("arbitrary"),
        name="mla_sample_v",
    )(o_lat, wuv)


def _rmsnorm_kernel(x_ref, g_ref, o_ref):
    x = x_ref[...]
    ms = jnp.mean(x * x, axis=-1, keepdims=True)
    o_ref[...] = x * lax.rsqrt(ms + EPS) * g_ref[...]


def rmsnorm_rows(x, gain, *, tm):
    n, d = x.shape
    return pl.pallas_call(
        _rmsnorm_kernel,
        grid=(n // tm,),
        in_specs=[pl.BlockSpec((tm, d), lambda i: (i, 0)), pl.BlockSpec((1, d), lambda i: (0, 0))],
        out_specs=pl.BlockSpec((tm, d), lambda i: (i, 0)),
        out_shape=jax.ShapeDtypeStruct((n, d), F32),
        compiler_params=_cparams("parallel"),
        name="rmsnorm",
    )(x, gain.reshape(1, d))


HG_CHUNK = 64
SSD_CHUNK = 128
SAMPLE_PAD = 8
ROW_TILE = 256


def _ab_layer(x, n_p, bp, tp, bs, ts, norm_w, lb, st_hg, st_ssm, st_conv, w_in, w_out, hg_norm, conv_w, conv_b,
              dt_bias, a_log, d_skip, ssm_norm):
    sizes = [HG_W, HG_W, HG_W, HG_W, SSM_INNER, CONV_DIM, SSM_HEADS]
    offs = np.concatenate([[0], np.cumsum(sizes)])
    w_pad = jnp.pad(w_in, ((0, 0), (0, LANES - SSM_HEADS))).astype(BF16)
    splits = [(int(offs[j]), int(offs[j + 1])) for j in range(6)] + [(int(offs[6]), int(offs[6]) + LANES)]
    q, f, i_in, g, z, xbc, dt = norm_matmul(x, norm_w, w_pad, splits, [F32] * 7, tm=ROW_TILE)

    def grp(a, prompt):
        if prompt:
            return a[:n_p].reshape(bp, tp, a.shape[1])
        a = a[n_p:].reshape(bs, ts, a.shape[1])
        return jnp.pad(a, ((0, 0), (0, SAMPLE_PAD - ts), (0, 0)))

    outs = []
    for prompt in (True, False):
        if prompt:
            b_, c_hg, c_ssd, tv_hg, tv_ssd = bp, HG_CHUNK, SSD_CHUNK, HG_CHUNK, SSD_CHUNK
            s_hg = jnp.zeros((bp, HG_HEADS, HG_D, HG_D), F32)
            s_ssm = jnp.zeros((bp, SSM_HEADS, SSM_P, SSM_N), F32)
            s_conv = jnp.zeros((bp, CONV_W - 1, CONV_DIM), F32)
        else:
            b_, c_hg, c_ssd, tv_hg, tv_ssd = bs, SAMPLE_PAD, SAMPLE_PAD, ts, ts
            s_hg, s_ssm, s_conv = st_hg, st_ssm, st_conv
        o_hg, hg_new = hgrn_scan(grp(q, prompt), grp(f, prompt), grp(i_in, prompt), grp(g, prompt), lb, hg_norm,
                                 s_hg, c=c_hg, t_valid=tv_hg)
        y, conv_new, ssm_new = ssd_scan(grp(z, prompt), grp(xbc, prompt), grp(dt, prompt), s_conv, s_ssm,
                                        conv_w, conv_b, dt_bias, a_log, d_skip, ssm_norm, c=c_ssd, t_valid=tv_ssd)
        if not prompt:
            o_hg, y = o_hg[:, :ts], y[:, :ts]
        outs.append((o_hg.reshape(-1, HG_W), y.reshape(-1, SSM_INNER), hg_new, ssm_new, conv_new))
    o_hg = jnp.concatenate([outs[0][0], outs[1][0]], axis=0)
    y = jnp.concatenate([outs[0][1], outs[1][1]], axis=0)
    w_out_b = w_out.astype(BF16)
    x = matmul_residual(x, [(o_hg, w_out_b[:HG_W]), (y, w_out_b[HG_W:])], tm=ROW_TILE)
    return x, outs[0][2:], outs[1][2:]


def _mla_layer(x, n_p, bp, tp, bs, ts, past_len, norm_w, cache_ckv, cache_krope, page_table, layer_c,
               w_in, q_norm, kv_norm, w_uq, w_uk, w_uv, w_out):
    w_in_p, wq_a, wq_b, wuk_p, wuv, w_dec = _mla_weights(w_in, w_uq, w_uk, w_uv)
    splits = [(0, MLA_LORA), (MLA_LORA, 2 * MLA_LORA), (2 * MLA_LORA, 2 * MLA_LORA + HEAD_PAD),
              (2 * MLA_LORA + HEAD_PAD, 2 * MLA_LORA + 2 * HEAD_PAD)]
    cq, ckv, kr_a, kr_b = norm_matmul(x, norm_w, w_in_p, splits, [F32] * 4, tm=ROW_TILE)
    pos = jnp.concatenate([jnp.tile(jnp.arange(tp, dtype=jnp.int32), bp),
                           jnp.tile(past_len + jnp.arange(ts, dtype=jnp.int32), bs)])
    ctab, stab = _rope_tables(pos)
    q, ckvn, krot = mla_q(cq, ckv, kr_a, kr_b, ctab, stab, q_norm, kv_norm, wq_a, wq_b, tm=ROW_TILE)
    krope = krot[:, ROPE_AT:ROPE_AT + MLA_ROPE]
    k_p, v_p = mla_kv(ckvn, krot, wuk_p, wuv, n=n_p, tm=ROW_TILE)
    hw = MLA_HEADS * HEAD_PAD
    o_p = flash_attention(q[:n_p].reshape(bp, tp, hw), k_p.reshape(bp, tp, hw),
                          v_p.reshape(bp, tp, MLA_HEADS * MLA_V), tq=512)
    o_p = o_p.reshape(n_p, MLA_HEADS * MLA_V)
    q_dec = mla_q_dec(q[n_p:], w_dec).reshape(bs, ts * MLA_HEADS, Q_DEC)
    c_new = jnp.pad(ckvn[n_p:].reshape(bs, ts, MLA_LORA), ((0, 0), (0, NEW_PAD - ts), (0, 0)))
    r_new = jnp.pad(krope[n_p:].reshape(bs, ts, MLA_ROPE), ((0, 0), (0, NEW_PAD - ts), (0, 0)))
    o_lat = mla_decode(page_table, q_dec, c_new, r_new, cache_ckv, cache_krope, layer=layer_c, t_new=ts)
    o_s = mla_sample_v(o_lat.reshape(bs * ts, MLA_HEADS * MLA_LORA), wuv)
    o = jnp.concatenate([o_p, o_s], axis=0)
    x = matmul_residual(x, [(o, w_out.astype(BF16))], tm=ROW_TILE)
    return x, (ckvn[:n_p].reshape(bp, tp, MLA_LORA), krope[:n_p].reshape(bp, tp, MLA_ROPE)), \
        (ckvn[n_p:].reshape(bs, ts, MLA_LORA), krope[n_p:].reshape(bs, ts, MLA_ROPE))


def kernel(x_prompt, x_sample, state_hgrn, state_ssm, state_conv, cache_ckv, cache_krope, page_table,
           norm_mix, norm_ffn, norm_final, w_in_ab, w_out_ab, hgrn_lb, hgrn_norm, conv_w, conv_b,
           dt_bias, a_log, d_skip, ssm_norm, w_in_c, q_norm, kv_norm, w_uq, w_uk, w_uv, w_out_c,
           w_route_group, b_route_group, w_route_expert, b_route_expert, w_gate, w_up, w_down):
    bp, tp, d = x_prompt.shape
    bs, ts, _ = x_sample.shape
    n_p = bp * tp
    depth = norm_mix.shape[0]
    n_a = w_in_ab.shape[0]
    past_len = page_table.shape[1] * cache_ckv.shape[2]
    lb_all = jnp.cumsum(jax.nn.softmax(hgrn_lb.astype(F32), axis=0), axis=0)[:n_a]
    x = jnp.concatenate([x_prompt.reshape(n_p, d), x_sample.reshape(bs * ts, d)], axis=0)
    a_p, a_s, c_p, c_s = [], [], [], []
    for layer in range(depth):
        j = layer // 2
        if layer % 2 == 0:
            x, sp, ss = _ab_layer(x, n_p, bp, tp, bs, ts, norm_mix[layer], lb_all[j], state_hgrn[j], state_ssm[j],
                                  state_conv[j], w_in_ab[j], w_out_ab[j], hgrn_norm[j], conv_w[j], conv_b[j],
                                  dt_bias[j], a_log[j], d_skip[j], ssm_norm[j])
            a_p.append(sp)
            a_s.append(ss)
        else:
            x, cp, cs = _mla_layer(x, n_p, bp, tp, bs, ts, past_len, norm_mix[layer], cache_ckv, cache_krope,
                                   page_table, j, w_in_c[j], q_norm[j], kv_norm[j], w_uq[j], w_uk[j], w_uv[j],
                                   w_out_c[j])
            c_p.append(cp)
            c_s.append(cs)
        x = hier_moe_block(x, norm_ffn[layer], w_route_group[layer], b_route_group[layer], w_route_expert[layer],
                           b_route_expert[layer], w_gate[layer], w_up[layer], w_down[layer])
    y = rmsnorm_rows(x, norm_final, tm=ROW_TILE)

    def stack(items, k):
        return jnp.stack([it[k] for it in items])

    return (y[:n_p].reshape(bp, tp, d), y[n_p:].reshape(bs, ts, d),
            stack(a_p, 0), stack(a_s, 0), stack(a_p, 1), stack(a_s, 1), stack(a_p, 2), stack(a_s, 2),
            stack(c_p, 0), stack(c_s, 0), stack(c_p, 1), stack(c_s, 1))
```

```python
import functools
import math

import jax
import jax.numpy as jnp
import numpy as np
from jax import lax
from jax.experimental import pallas as pl
from jax.experimental.pallas import tpu as pltpu

F32 = jnp.float32
BF16 = jnp.bfloat16

EPS = 1e-6
D_MODEL = 1024
HG_HEADS = 4
HG_D = 128
HG_W = HG_HEADS * HG_D
SSM_HEADS = 16
SSM_P = 64
SSM_N = 128
SSM_GROUPS = 2
SSM_INNER = SSM_HEADS * SSM_P
CONV_W = 4
CONV_DIM = SSM_INNER + 2 * SSM_GROUPS * SSM_N
LANES = 128
SUBLANES = 8
VMEM_LIMIT = 48 * 1024 * 1024
NEG_BIG = -1e30


def _cparams(*sem):
    return pltpu.CompilerParams(dimension_semantics=sem, vmem_limit_bytes=VMEM_LIMIT)


def _dot(a, b):
    return jnp.dot(a, b, preferred_element_type=F32)


def _dot_nt(a, b):
    return lax.dot_general(a, b, (((1,), (1,)), ((), ())), preferred_element_type=F32)


def _dot_tn(a, b):
    return lax.dot_general(a, b, (((0,), (0,)), ((), ())), preferred_element_type=F32)


def _split3(x):
    hi = x.astype(BF16)
    r1 = x - hi.astype(F32)
    mid = r1.astype(BF16)
    lo = (r1 - mid.astype(F32)).astype(BF16)
    return hi, mid, lo


def _silu(x):
    return x * (1.0 / (1.0 + jnp.exp(-x)))


def _sigmoid(x):
    return 1.0 / (1.0 + jnp.exp(-x))


def _norm_matmul_kernel(x_ref, g_ref, w_ref, *out_refs, splits, normalize):
    x = x_ref[...].astype(F32)
    if normalize:
        ms = jnp.mean(x * x, axis=-1, keepdims=True)
        x = x * lax.rsqrt(ms + EPS) * g_ref[...]
    h = x.astype(BF16)
    for (a, b), o_ref in zip(splits, out_refs):
        o_ref[...] = _dot(h, w_ref[:, a:b]).astype(o_ref.dtype)


def norm_matmul(x, gain, w, splits, out_dtypes, *, tm, normalize=True):
    n, k = x.shape
    assert n % tm == 0
    kern = functools.partial(_norm_matmul_kernel, splits=tuple(splits), normalize=normalize)
    out_shape = [jax.ShapeDtypeStruct((n, b - a), dt) for (a, b), dt in zip(splits, out_dtypes)]
    out_specs = [pl.BlockSpec((tm, b - a), lambda i: (i, 0)) for (a, b) in splits]
    return pl.pallas_call(
        kern,
        grid=(n // tm,),
        in_specs=[pl.BlockSpec((tm, k), lambda i: (i, 0)),
                  pl.BlockSpec((1, k), lambda i: (0, 0)),
                  pl.BlockSpec(w.shape, lambda i: (0, 0))],
        out_specs=out_specs,
        out_shape=out_shape,
        compiler_params=_cparams("parallel"),
        name="norm_matmul",
    )(x, gain.reshape(1, k), w)


def _matmul_residual_kernel(*refs, transposed):
    n_pairs = len(transposed)
    res_ref = refs[0]
    o_ref = refs[1 + 2 * n_pairs]
    acc = res_ref[...]
    for j in range(n_pairs):
        a_ref, w_ref = refs[1 + 2 * j], refs[2 + 2 * j]
        a = a_ref[...].astype(BF16)
        acc = acc + (_dot_tn(a, w_ref[...]) if transposed[j] else _dot(a, w_ref[...]))
    o_ref[...] = acc


def matmul_residual(res, pairs, *, tm):
    n, d = res.shape
    assert n % tm == 0
    in_specs = [pl.BlockSpec((tm, d), lambda i: (i, 0))]
    args = [res]
    for a, w, tr in pairs:
        if tr:
            in_specs.append(pl.BlockSpec((a.shape[0], tm), lambda i: (0, i)))
        else:
            in_specs.append(pl.BlockSpec((tm, a.shape[1]), lambda i: (i, 0)))
        in_specs.append(pl.BlockSpec(w.shape, lambda i: (0, 0)))
        args += [a, w]
    return pl.pallas_call(
        functools.partial(_matmul_residual_kernel, transposed=tuple(bool(p[2]) for p in pairs)),
        grid=(n // tm,),
        in_specs=in_specs,
        out_specs=pl.BlockSpec((tm, d), lambda i: (i, 0)),
        out_shape=jax.ShapeDtypeStruct((n, d), F32),
        compiler_params=_cparams("parallel"),
        name="matmul_residual",
    )(*args)


def _hgrn_level_halves(c):
    halves = []
    b = c // 2
    while b >= 1:
        halves.append(b)
        b //= 2
    return halves


@functools.lru_cache(maxsize=None)
def _hgrn_consts(c):
    t = np.arange(c)
    rows = [t[None, :] <= t[:, None], t[None, :] > t[:, None]]
    masks = []
    for b in _hgrn_level_halves(c):
        blk = t // b
        st = blk * b
        en = st + b - 1
        odd = blk % 2 == 1
        even = ~odd
        rows.append(odd[:, None] & (t[None, :] >= st[:, None]) & (t[None, :] <= t[:, None]))
        rows.append(even[:, None] & (t[None, :] > t[:, None]) & (t[None, :] <= en[:, None]))
        masks.append((t[:, None] // (2 * b) == t[None, :] // (2 * b)) & odd[:, None] & even[None, :])
    masks.append(np.eye(c, dtype=bool))
    rows.append(np.ones((SUBLANES, c), dtype=bool))
    dg = np.concatenate(rows, axis=0).astype(np.float32)
    dg3 = np.concatenate([dg, dg, dg], axis=1)
    mk = np.stack(masks).astype(np.float32)
    return dg3, mk


def _hgrn_kernel(q_ref, f_ref, i_ref, g_ref, lb_ref, nw_ref, dg_ref, mk_ref, s0_ref,
                 o_ref, sfin_ref, s_scr, *, c, t_valid, n_chunks):
    ci = pl.program_id(1)

    @pl.when(ci == 0)
    def _():
        s_scr[...] = s0_ref[...]

    lb = lb_ref[...]
    fl = f_ref[...]
    sig = _sigmoid(fl)
    logf = jnp.log(lb + (1.0 - lb) * sig)
    kk = (1.0 - lb) * (1.0 - sig)
    if t_valid < c:
        row = lax.broadcasted_iota(jnp.int32, (c, HG_W), 0)
        live = row < t_valid
        logf = jnp.where(live, logf, 0.0)
        kk = jnp.where(live, kk, 0.0)
    qa = _silu(q_ref[...])
    vv = i_ref[...].astype(BF16)

    lf3 = jnp.concatenate(_split3(logf), axis=0)
    ex = jnp.exp(_dot(dg_ref[...], lf3))
    halves = _hgrn_level_halves(c)
    n_lv = len(halves)
    e_cum = ex[0:c]
    e_rev = ex[c:2 * c]
    q_state = (qa * e_cum).astype(BF16)
    k_state = (kk * e_rev).astype(BF16)
    q_lv = [(qa * ex[(2 + 2 * l) * c:(3 + 2 * l) * c]).astype(BF16) for l in range(n_lv)]
    k_lv = [(kk * ex[(3 + 2 * l) * c:(4 + 2 * l) * c]).astype(BF16) for l in range(n_lv)]
    q_lv.append(qa.astype(BF16))
    k_lv.append(kk.astype(BF16))
    ones_t = jnp.ones((3 * c, HG_D), BF16)
    gate = _silu(g_ref[...])
    nw = nw_ref[...]

    for h in range(HG_HEADS):
        sl = slice(h * HG_D, (h + 1) * HG_D)
        sc = jnp.zeros((c, c), F32)
        for l in range(n_lv + 1):
            sc = sc + mk_ref[l] * _dot_nt(q_lv[l][:, sl], k_lv[l][:, sl])
        s_h = s_scr[h]
        o_h = _dot(sc.astype(BF16), vv[:, sl]) + _dot(q_state[:, sl], s_h.astype(BF16))
        dec = jnp.exp(_dot_tn(lf3[:, sl], ones_t))
        s_scr[h] = s_h * dec + _dot_tn(k_state[:, sl], vv[:, sl])
        ms = jnp.mean(o_h * o_h, axis=-1, keepdims=True)
        o_h = o_h * lax.rsqrt(ms + EPS) * nw[:, sl] * gate[:, sl]
        o_ref[:, sl] = o_h.astype(o_ref.dtype)

    @pl.when(ci == n_chunks - 1)
    def _():
        sfin_ref[...] = s_scr[...]


def hgrn_scan(q, f, i_in, g, lb, norm_w, s0, *, b, t, c, t_valid):
    n_chunks = t // c
    assert t % c == 0 and (t_valid == c or n_chunks == 1) and q.shape[0] >= b * t
    dg3, mk = _hgrn_consts(c)
    dg3 = jnp.asarray(dg3, BF16)
    mk = jnp.asarray(mk, F32)
    tok = pl.BlockSpec((c, HG_W), lambda bi, ci: (bi * n_chunks + ci, 0))
    st = pl.BlockSpec((None, HG_HEADS, HG_D, HG_D), lambda bi, ci: (bi, 0, 0, 0))
    row = pl.BlockSpec((1, HG_W), lambda bi, ci: (0, 0))
    kern = functools.partial(_hgrn_kernel, c=c, t_valid=t_valid, n_chunks=n_chunks)
    return pl.pallas_call(
        kern,
        grid=(b, n_chunks),
        in_specs=[tok, tok, tok, tok, row, row,
                  pl.BlockSpec(dg3.shape, lambda bi, ci: (0, 0)),
                  pl.BlockSpec(mk.shape, lambda bi, ci: (0, 0, 0)),
                  st],
        out_specs=[tok, st],
        out_shape=[jax.ShapeDtypeStruct((b * t, HG_W), BF16),
                   jax.ShapeDtypeStruct((b, HG_HEADS, HG_D, HG_D), F32)],
        scratch_shapes=[pltpu.VMEM((HG_HEADS, HG_D, HG_D), F32)],
        compiler_params=_cparams("parallel", "arbitrary"),
        name="hgrn_scan",
    )(q, f, i_in, g, lb.reshape(1, HG_W), norm_w.reshape(1, HG_W), dg3, mk, s0)


CONV_HEAD = 8


@functools.lru_cache(maxsize=None)
def _ssd_consts(c):
    t = np.arange(c)
    tri = (t[None, :] <= t[:, None]).astype(np.float32)
    rev = (t[None, :] > t[:, None]).astype(np.float32)
    ones = np.ones((SUBLANES, c), np.float32)
    tg = np.concatenate([tri, rev, ones], axis=0)
    tg3 = np.concatenate([tg, tg, tg], axis=1)
    u3 = np.concatenate([tri.T, tri.T, tri.T], axis=0)
    return tg3, u3


def _ssd_kernel(z_ref, xbc_ref, dt_ref, cbuf_ref, cw_ref, cb_ref, dtb_ref, alog_ref, dsk_ref, nw_ref,
                tg_ref, u_ref, h0_ref,
                y_ref, cnew_ref, hfin_ref, xp_scr, h_scr, y_scr, *, c, t_valid, n_chunks):
    ci = pl.program_id(1)

    @pl.when(ci == 0)
    def _():
        h_scr[...] = h0_ref[...]
        xp_scr[CONV_HEAD - (CONV_W - 1):CONV_HEAD, :] = cbuf_ref[...]

    xp_scr[CONV_HEAD:CONV_HEAD + c, :] = xbc_ref[...]
    conv = cb_ref[...]
    for i in range(CONV_W):
        off = CONV_HEAD - (CONV_W - 1) + i
        conv = conv + xp_scr[off:off + c, :] * cw_ref[i:i + 1, :]
    conv = _silu(conv)

    @pl.when(ci == n_chunks - 1)
    def _():
        cnew_ref[...] = xp_scr[CONV_HEAD + t_valid - (CONV_W - 1):CONV_HEAD + t_valid, :]

    xp_scr[CONV_HEAD - (CONV_W - 1):CONV_HEAD, :] = xp_scr[CONV_HEAD + c - (CONV_W - 1):CONV_HEAD + c, :]

    xs = conv[:, :SSM_INNER]
    dt_raw = dt_ref[...] + dtb_ref[...]
    dt = jnp.maximum(dt_raw, 0.0) + jnp.log(1.0 + jnp.exp(-jnp.abs(dt_raw)))
    if t_valid < c:
        row = lax.broadcasted_iota(jnp.int32, (c, LANES), 0)
        dt = jnp.where(row < t_valid, dt, 0.0)
    a = -jnp.exp(alog_ref[...])
    da = dt * a
    da3 = jnp.concatenate(_split3(da), axis=0)
    xx = _dot(tg_ref[...], da3)
    cum = xx[0:c]
    e_cum = jnp.exp(cum)
    w_all = jnp.exp(xx[c:2 * c]) * dt
    e_last = jnp.exp(xx[2 * c:2 * c + 1])
    cum_t = _dot_tn(da3, u_ref[...])

    tril = lax.broadcasted_iota(jnp.int32, (c, c), 0) >= lax.broadcasted_iota(jnp.int32, (c, c), 1)
    heads_per_group = SSM_HEADS // SSM_GROUPS
    for g in range(SSM_GROUPS):
        bm = conv[:, SSM_INNER + g * SSM_N:SSM_INNER + (g + 1) * SSM_N]
        cm = conv[:, SSM_INNER + (SSM_GROUPS + g) * SSM_N:SSM_INNER + (SSM_GROUPS + g + 1) * SSM_N]
        bm_b = bm.astype(BF16)
        cb = _dot_nt(cm.astype(BF16), bm_b)
        for hh in range(heads_per_group):
            h = g * heads_per_group + hh
            x_h = xs[:, h * SSM_P:(h + 1) * SSM_P]
            diff = cum[:, h:h + 1] - cum_t[h:h + 1, :]
            seg = jnp.exp(jnp.where(tril, diff, NEG_BIG))
            xd = (dt[:, h:h + 1] * x_h).astype(BF16)
            cme = (cm * e_cum[:, h:h + 1]).astype(BF16)
            h_h = h_scr[h]
            y_h = _dot((cb * seg).astype(BF16), xd) + _dot_nt(cme, h_h.astype(BF16))
            xw = (w_all[:, h:h + 1] * x_h).astype(BF16)
            h_scr[h] = h_h * e_last[:, h:h + 1] + _dot_tn(xw, bm_b)
            y_scr[:, h * SSM_P:(h + 1) * SSM_P] = y_h

    y = (y_scr[...] + dsk_ref[...] * xs) * _silu(z_ref[...])
    gw = SSM_INNER // SSM_GROUPS
    for g in range(SSM_GROUPS):
        yg = y[:, g * gw:(g + 1) * gw]
        ms = jnp.mean(yg * yg, axis=-1, keepdims=True)
        y_ref[:, g * gw:(g + 1) * gw] = (yg * lax.rsqrt(ms + EPS) * nw_ref[:, g * gw:(g + 1) * gw]).astype(y_ref.dtype)

    @pl.when(ci == n_chunks - 1)
    def _():
        hfin_ref[...] = h_scr[...]


def ssd_scan(z, xbc, dt, conv_buf, h0, conv_w, conv_b, dt_bias, a_log, d_skip, norm_w, *, b, t, c, t_valid):
    n_chunks = t // c
    assert t % c == 0 and (t_valid == c or n_chunks == 1) and z.shape[0] >= b * t
    tg3, u3 = _ssd_consts(c)
    tg3 = jnp.asarray(tg3, BF16)
    u3 = jnp.asarray(u3, BF16)

    def pad_heads(v):
        return jnp.pad(v.astype(F32), (0, LANES - SSM_HEADS)).reshape(1, LANES)

    def tok(w):
        return pl.BlockSpec((c, w), lambda bi, ci: (bi * n_chunks + ci, 0))

    def const(shape):
        return pl.BlockSpec(shape, lambda bi, ci: (0,) * len(shape))

    cst = pl.BlockSpec((None, CONV_W - 1, CONV_DIM), lambda bi, ci: (bi, 0, 0))
    hst = pl.BlockSpec((None, SSM_HEADS, SSM_P, SSM_N), lambda bi, ci: (bi, 0, 0, 0))
    kern = functools.partial(_ssd_kernel, c=c, t_valid=t_valid, n_chunks=n_chunks)
    return pl.pallas_call(
        kern,
        grid=(b, n_chunks),
        in_specs=[tok(SSM_INNER), tok(CONV_DIM), tok(LANES), cst,
                  const((CONV_W, CONV_DIM)), const((1, CONV_DIM)), const((1, LANES)), const((1, LANES)),
                  const((1, SSM_INNER)), const((1, SSM_INNER)), const(tg3.shape), const(u3.shape), hst],
        out_specs=[tok(SSM_INNER), cst, hst],
        out_shape=[jax.ShapeDtypeStruct((b * t, SSM_INNER), BF16),
                   jax.ShapeDtypeStruct((b, CONV_W - 1, CONV_DIM), F32),
                   jax.ShapeDtypeStruct((b, SSM_HEADS, SSM_P, SSM_N), F32)],
        scratch_shapes=[pltpu.VMEM((CONV_HEAD + c, CONV_DIM), F32),
                        pltpu.VMEM((SSM_HEADS, SSM_P, SSM_N), F32),
                        pltpu.VMEM((c, SSM_INNER), F32)],
        compiler_params=_cparams("parallel", "arbitrary"),
        name="ssd_scan",
    )(z, xbc, dt, conv_buf, conv_w, conv_b.reshape(1, CONV_DIM), pad_heads(dt_bias), pad_heads(a_log),
      jnp.repeat(d_skip.astype(F32), SSM_P).reshape(1, SSM_INNER), norm_w.reshape(1, SSM_INNER), tg3, u3, h0)


N_GROUPS = 4
EXPERTS_PER_GROUP = 8
N_EXPERTS = N_GROUPS * EXPERTS_PER_GROUP
TOP_K = 2
D_EXPERT = 256
MOE_TM = 256


def _router_kernel(x_ref, g_ref, w_ref, b_ref, tri_ref, xn_ref, rw_ref, ri_ref, cnt_ref, carry_scr):
    @pl.when(pl.program_id(0) == 0)
    def _():
        carry_scr[...] = jnp.zeros_like(carry_scr)

    x = x_ref[...]
    ms = jnp.mean(x * x, axis=-1, keepdims=True)
    xn = x * lax.rsqrt(ms + EPS) * g_ref[...]
    for s in range(SUBLANES):
        xn_ref[pl.ds(s, x.shape[0], stride=SUBLANES), :] = xn[:, s * LANES:(s + 1) * LANES]
    logits = jnp.dot(xn, w_ref[...], precision=lax.Precision.HIGHEST, preferred_element_type=F32) + b_ref[...]
    lane = lax.broadcasted_iota(jnp.int32, logits.shape, 1)
    is_g = (lane >= N_EXPERTS) & (lane < N_EXPERTS + N_GROUPS)
    gl = jnp.where(is_g, logits, NEG_BIG)
    gmax = jnp.max(gl, axis=-1, keepdims=True)
    g_sel = jnp.min(jnp.where(gl == gmax, lane, 4 * LANES), axis=-1, keepdims=True) - N_EXPERTS
    g_w = 1.0 / jnp.sum(jnp.where(is_g, jnp.exp(gl - gmax), 0.0), axis=-1, keepdims=True)
    lo = g_sel * EXPERTS_PER_GROUP
    in_grp = (lane >= lo) & (lane < lo + EXPERTS_PER_GROUP)
    el = jnp.where(in_grp, logits, NEG_BIG)
    emax = jnp.max(el, axis=-1, keepdims=True)
    ee = jnp.where(in_grp, jnp.exp(el - emax), 0.0)
    p = ee / jnp.sum(ee, axis=-1, keepdims=True)
    p = jnp.where(in_grp, p, -1.0)
    p1 = jnp.max(p, axis=-1, keepdims=True)
    i1 = jnp.min(jnp.where(p == p1, lane, 4 * LANES), axis=-1, keepdims=True)
    p_rest = jnp.where(lane == i1, -1.0, p)
    p2 = jnp.max(p_rest, axis=-1, keepdims=True)
    i2 = jnp.min(jnp.where(p_rest == p2, lane, 4 * LANES), axis=-1, keepdims=True)
    w1 = p1 / (p1 + p2) * g_w
    w2 = p2 / (p1 + p2) * g_w
    rw_ref[...] = jnp.where(lane == 0, w1, jnp.where(lane == 1, w2, 0.0))
    hit1 = lane == i1
    hit2 = lane == i2
    onehot = (hit1 | hit2).astype(BF16)
    before = _dot(tri_ref[...], onehot) + carry_scr[0:1, :]
    r1 = jnp.sum(jnp.where(hit1, before, 0.0), axis=-1, keepdims=True).astype(jnp.int32)
    r2 = jnp.sum(jnp.where(hit2, before, 0.0), axis=-1, keepdims=True).astype(jnp.int32)
    ri_ref[...] = jnp.where(lane == 0, i1, jnp.where(lane == 1, i2, jnp.where(lane == 2, r1, jnp.where(lane == 3, r2, 0))))
    carry_scr[...] = carry_scr[...] + jnp.sum(onehot.astype(F32), axis=0, keepdims=True)
    cnt_ref[...] = carry_scr[...]


def moe_router(x, gain, w_rg, b_rg, w_re, b_re, *, tm):
    n, d = x.shape
    assert d == SUBLANES * LANES
    w = jnp.zeros((d, LANES), F32).at[:, :N_EXPERTS].set(w_re).at[:, N_EXPERTS:N_EXPERTS + N_GROUPS].set(w_rg)
    b = jnp.zeros((1, LANES), F32).at[0, :N_EXPERTS].set(b_re).at[0, N_EXPERTS:N_EXPERTS + N_GROUPS].set(b_rg)
    t = np.arange(tm)
    tri = jnp.asarray(t[None, :] < t[:, None], BF16)
    return pl.pallas_call(
        _router_kernel,
        grid=(n // tm,),
        in_specs=[pl.BlockSpec((tm, d), lambda i: (i, 0)),
                  pl.BlockSpec((1, d), lambda i: (0, 0)),
                  pl.BlockSpec((d, LANES), lambda i: (0, 0)),
                  pl.BlockSpec((1, LANES), lambda i: (0, 0)),
                  pl.BlockSpec((tm, tm), lambda i: (0, 0))],
        out_specs=[pl.BlockSpec((tm * SUBLANES, LANES), lambda i: (i, 0)),
                   pl.BlockSpec((tm, LANES), lambda i: (i, 0)),
                   pl.BlockSpec((tm, LANES), lambda i: (i, 0)),
                   pl.BlockSpec((SUBLANES, LANES), lambda i: (0, 0))],
        out_shape=[jax.ShapeDtypeStruct((n * SUBLANES, LANES), F32),
                   jax.ShapeDtypeStruct((n, LANES), F32),
                   jax.ShapeDtypeStruct((n, LANES), jnp.int32),
                   jax.ShapeDtypeStruct((SUBLANES, LANES), F32)],
        scratch_shapes=[pltpu.VMEM((SUBLANES, LANES), F32)],
        compiler_params=_cparams("arbitrary"),
        name="moe_router",
    )(x, gain.reshape(1, d), w, b, tri)


def _token_rows(idx):
    return pl.ds(pl.multiple_of(idx * SUBLANES, SUBLANES), SUBLANES)


def _moe_dispatch_kernel(dest_ref, xn_ref, zero_ref, xs_ref, sem, *, tm):
    del zero_ref
    base = pl.program_id(0) * (tm * TOP_K)

    def start(r, carry):
        for k in range(TOP_K):
            pltpu.make_async_copy(xn_ref.at[_token_rows(r)], xs_ref.at[_token_rows(dest_ref[base + r * TOP_K + k])],
                                  sem).start()
        return carry

    lax.fori_loop(0, tm, start, 0, unroll=8)

    def wait(r, carry):
        for k in range(TOP_K):
            pltpu.make_async_copy(xn_ref.at[_token_rows(0)], xs_ref.at[_token_rows(0)], sem).wait()
        return carry

    lax.fori_loop(0, tm, wait, 0, unroll=8)


def moe_dispatch(dest, xn, n_rows, *, tm):
    n = xn.shape[0] // SUBLANES
    grid_spec = pltpu.PrefetchScalarGridSpec(
        num_scalar_prefetch=1,
        grid=(n // tm,),
        in_specs=[pl.BlockSpec((tm * SUBLANES, LANES), lambda i, d: (i, 0)),
                  pl.BlockSpec(memory_space=pl.ANY)],
        out_specs=pl.BlockSpec(memory_space=pl.ANY),
        scratch_shapes=[pltpu.SemaphoreType.DMA(())],
    )
    return pl.pallas_call(
        functools.partial(_moe_dispatch_kernel, tm=tm),
        grid_spec=grid_spec,
        out_shape=jax.ShapeDtypeStruct((n_rows * SUBLANES, LANES), F32),
        input_output_aliases={2: 0},
        compiler_params=_cparams("arbitrary"),
        name="moe_dispatch",
    )(dest, xn, jnp.zeros((n_rows * SUBLANES, LANES), F32))


def _moe_ffn_kernel(te_ref, nu_ref, x_ref, wg_ref, wu_ref, wd_ref, y_ref):
    i = pl.program_id(0)

    @pl.when(i < nu_ref[0])
    def _():
        x = jnp.concatenate([x_ref[pl.ds(s, MOE_TM, stride=SUBLANES), :] for s in range(SUBLANES)], axis=1).astype(BF16)
        a = _dot(x, wg_ref[...].astype(BF16))
        b = _dot(x, wu_ref[...].astype(BF16))
        hid = (_silu(a) * b).astype(BF16)
        y = _dot(hid, wd_ref[...].astype(BF16))
        for s in range(SUBLANES):
            y_ref[pl.ds(s, MOE_TM, stride=SUBLANES), :] = y[:, s * LANES:(s + 1) * LANES]

    @pl.when(i >= nu_ref[0])
    def _():
        y_ref[...] = jnp.zeros_like(y_ref)


def moe_ffn(tile_expert, n_used, x_sorted, w_gate, w_up, w_down, layer):
    n_tiles = x_sorted.shape[0] // (MOE_TM * SUBLANES)
    d = w_gate.shape[2]
    tile = pl.BlockSpec((MOE_TM * SUBLANES, LANES), lambda i, te, nu: (i, 0))
    grid_spec = pltpu.PrefetchScalarGridSpec(
        num_scalar_prefetch=2,
        grid=(n_tiles,),
        in_specs=[tile,
                  pl.BlockSpec((None, None, d, D_EXPERT), lambda i, te, nu: (layer, te[i], 0, 0)),
                  pl.BlockSpec((None, None, d, D_EXPERT), lambda i, te, nu: (layer, te[i], 0, 0)),
                  pl.BlockSpec((None, None, D_EXPERT, d), lambda i, te, nu: (layer, te[i], 0, 0))],
        out_specs=tile,
    )
    return pl.pallas_call(
        _moe_ffn_kernel,
        grid_spec=grid_spec,
        out_shape=jax.ShapeDtypeStruct(x_sorted.shape, F32),
        compiler_params=_cparams("arbitrary"),
        name="moe_ffn",
    )(tile_expert, n_used, x_sorted, w_gate, w_up, w_down)


def _moe_combine_kernel(dest_ref, x_ref, rw_ref, ys_ref, o_ref, ybuf, sem, *, tm, n_steps):
    i = pl.program_id(0)
    slot = i % 2
    slot_rows = tm * TOP_K * SUBLANES

    def issue(step, to_slot):
        base = step * (tm * TOP_K)

        def body(r, carry):
            for k in range(TOP_K):
                dst = pl.ds(pl.multiple_of(to_slot * slot_rows + (r * TOP_K + k) * SUBLANES, SUBLANES), SUBLANES)
                pltpu.make_async_copy(ys_ref.at[_token_rows(dest_ref[base + r * TOP_K + k])], ybuf.at[dst],
                                      sem.at[to_slot]).start()
            return carry

        lax.fori_loop(0, tm, body, 0, unroll=8)

    @pl.when(i == 0)
    def _():
        issue(0, 0)

    @pl.when(i + 1 < n_steps)
    def _():
        issue(i + 1, 1 - slot)

    def wait(r, carry):
        for k in range(TOP_K):
            pltpu.make_async_copy(ys_ref.at[_token_rows(0)], ybuf.at[_token_rows(0)], sem.at[slot]).wait()
        return carry

    lax.fori_loop(0, tm, wait, 0, unroll=8)

    rw = rw_ref[...]
    g0 = rw[:, 0:1]
    g1 = rw[:, 1:2]
    first = pl.multiple_of(slot * slot_rows, SUBLANES)
    for s in range(SUBLANES):
        y0 = ybuf[pl.ds(first + s, tm, stride=TOP_K * SUBLANES), :]
        y1 = ybuf[pl.ds(first + SUBLANES + s, tm, stride=TOP_K * SUBLANES), :]
        o_ref[:, s * LANES:(s + 1) * LANES] = x_ref[:, s * LANES:(s + 1) * LANES] + (g0 * y0 + g1 * y1)


def moe_combine(dest, x, rw, y_sorted, *, tm):
    n, d = x.shape
    n_steps = n // tm
    grid_spec = pltpu.PrefetchScalarGridSpec(
        num_scalar_prefetch=1,
        grid=(n_steps,),
        in_specs=[pl.BlockSpec((tm, d), lambda i, dd: (i, 0)),
                  pl.BlockSpec((tm, LANES), lambda i, dd: (i, 0)),
                  pl.BlockSpec(memory_space=pl.ANY)],
        out_specs=pl.BlockSpec((tm, d), lambda i, dd: (i, 0)),
        scratch_shapes=[pltpu.VMEM((2 * tm * TOP_K * SUBLANES, LANES), F32),
                        pltpu.SemaphoreType.DMA((2,))],
    )
    return pl.pallas_call(
        functools.partial(_moe_combine_kernel, tm=tm, n_steps=n_steps),
        grid_spec=grid_spec,
        out_shape=jax.ShapeDtypeStruct((n, d), F32),
        compiler_params=_cparams("arbitrary"),
        name="moe_combine",
    )(dest, x, rw, y_sorted)


def _moe_plan(ri, counts, n):
    eid = ri[:, :TOP_K]
    rank = ri[:, TOP_K:2 * TOP_K]
    counts = counts[0, :N_EXPERTS].astype(jnp.int32)
    tiles = (counts + MOE_TM - 1) // MOE_TM
    tile_end = jnp.cumsum(tiles)
    tile_start = tile_end - tiles
    onehot = eid[:, :, None] == jnp.arange(N_EXPERTS, dtype=jnp.int32)[None, None, :]
    dest = jnp.sum(jnp.where(onehot, tile_start[None, None, :], 0), axis=-1) * MOE_TM + rank
    n_rows = TOP_K * n + N_EXPERTS * MOE_TM
    n_tiles = n_rows // MOE_TM
    n_used = tile_end[-1]
    t_idx = jnp.minimum(jnp.arange(n_tiles, dtype=jnp.int32), n_used - 1)
    tile_expert = jnp.sum((t_idx[:, None] >= tile_end[None, :]).astype(jnp.int32), axis=1)
    return dest.reshape(-1).astype(jnp.int32), tile_expert.astype(jnp.int32), n_used.reshape(1).astype(jnp.int32), n_rows


def hier_moe_block(x, gain, w_rg, b_rg, w_re, b_re, w_gate, w_up, w_down, layer):
    n, d = x.shape
    xn, rw, ri, counts = moe_router(x, gain, w_rg, b_rg, w_re, b_re, tm=512)
    dest, tile_expert, n_used, n_rows = _moe_plan(ri, counts, n)
    x_sorted = moe_dispatch(dest, xn, n_rows, tm=512)
    y_sorted = moe_ffn(tile_expert, n_used, x_sorted, w_gate, w_up, w_down, layer)
    return moe_combine(dest, x, rw, y_sorted, tm=256)


MLA_HEADS = 16
MLA_LORA = 256
MLA_NOPE = 64
MLA_ROPE = 32
MLA_V = 64
MLA_SCALE = (MLA_NOPE + MLA_ROPE) ** -0.5
ROPE_THETA = 10000.0
HEAD_PAD = 128
ROPE_AT = MLA_NOPE
Q_DEC = 384


def _rope_tables(pos):
    half = MLA_ROPE // 2
    inv = ROPE_THETA ** (-jnp.arange(half, dtype=F32) / half)
    ang = pos.astype(F32)[:, None] * inv[None, :]
    cos, sin = jnp.cos(ang), jnp.sin(ang)
    n = pos.shape[0]
    ones = jnp.ones((n, MLA_NOPE), F32)
    zeros_n = jnp.zeros((n, MLA_NOPE), F32)
    zeros_p = jnp.zeros((n, HEAD_PAD - MLA_NOPE - MLA_ROPE), F32)
    ctab = jnp.concatenate([ones, cos, cos, zeros_p], axis=1)
    stab = jnp.concatenate([zeros_n, sin, sin, zeros_p], axis=1)
    return ctab, stab


def _mla_weights(w_in, w_uq, w_uk, w_uv):
    d = w_in.shape[0]
    half = MLA_ROPE // 2
    w_kr = w_in[:, 2 * MLA_LORA:]
    zl = jnp.zeros((d, ROPE_AT), F32)
    zr = jnp.zeros((d, HEAD_PAD - ROPE_AT - MLA_ROPE), F32)
    kr_a = jnp.concatenate([zl, w_kr, zr], axis=1)
    kr_b = jnp.concatenate([zl, -w_kr[:, half:], w_kr[:, :half], zr], axis=1)
    w_in_p = jnp.concatenate([w_in[:, :2 * MLA_LORA], kr_a, kr_b], axis=1).astype(BF16)
    wq = w_uq.reshape(MLA_LORA, MLA_HEADS, MLA_NOPE + MLA_ROPE)
    nope, x1, x2 = wq[..., :MLA_NOPE], wq[..., MLA_NOPE:MLA_NOPE + half], wq[..., MLA_NOPE + half:]
    zp = jnp.zeros((MLA_LORA, MLA_HEADS, HEAD_PAD - MLA_NOPE - MLA_ROPE), F32)
    wq_a = jnp.concatenate([nope, x1, x2, zp], axis=-1).reshape(MLA_LORA, MLA_HEADS * HEAD_PAD).astype(BF16)
    wq_b = jnp.concatenate([jnp.zeros_like(nope), -x2, x1, zp], axis=-1).reshape(MLA_LORA, MLA_HEADS * HEAD_PAD).astype(BF16)
    zk = jnp.zeros((MLA_LORA, MLA_HEADS, HEAD_PAD - MLA_NOPE), F32)
    wuk_p = jnp.concatenate([w_uk, zk], axis=-1).reshape(MLA_LORA, MLA_HEADS * HEAD_PAD).astype(BF16)
    wuv = w_uv.reshape(MLA_LORA, MLA_HEADS * MLA_V).astype(BF16)
    absorb = jnp.transpose(w_uk, (1, 2, 0))
    sel = jnp.zeros((MLA_ROPE, Q_DEC - MLA_LORA), F32).at[jnp.arange(MLA_ROPE), jnp.arange(MLA_ROPE)].set(1.0)
    top = jnp.concatenate([absorb, jnp.zeros((MLA_HEADS, MLA_NOPE, Q_DEC - MLA_LORA), F32)], axis=-1)
    mid = jnp.broadcast_to(jnp.concatenate([jnp.zeros((MLA_ROPE, MLA_LORA), F32), sel], axis=-1)[None],
                           (MLA_HEADS, MLA_ROPE, Q_DEC))
    bot = jnp.zeros((MLA_HEADS, HEAD_PAD - MLA_NOPE - MLA_ROPE, Q_DEC), F32)
    w_dec = jnp.concatenate([top, mid, bot], axis=1).astype(BF16)
    return w_in_p, wq_a, wq_b, wuk_p, wuv, w_dec


def _mla_q_kernel(cq_ref, ckv_ref, kra_ref, krb_ref, ct_ref, st_ref, ctt_ref, stt_ref, qn_ref, kvn_ref,
                  wa_ref, wb_ref, qt_ref, ckvn_ref, krot_ref):
    cq = cq_ref[...]
    ms = jnp.mean(cq * cq, axis=-1, keepdims=True)
    cqn = (cq * lax.rsqrt(ms + EPS) * qn_ref[...]).astype(BF16)
    ckv = ckv_ref[...]
    ms2 = jnp.mean(ckv * ckv, axis=-1, keepdims=True)
    ckvn_ref[...] = ckv * lax.rsqrt(ms2 + EPS) * kvn_ref[...]
    krot_ref[...] = kra_ref[...] * ct_ref[...] + krb_ref[...] * st_ref[...]
    ctt = ctt_ref[...]
    stt = stt_ref[...]
    for h in range(MLA_HEADS):
        sl = slice(h * HEAD_PAD, (h + 1) * HEAD_PAD)
        qh = _dot_nt(wa_ref[sl, :], cqn) * ctt + _dot_nt(wb_ref[sl, :], cqn) * stt
        qt_ref[sl, :] = (qh * MLA_SCALE).astype(qt_ref.dtype)


def mla_q(cq, ckv, kr_a, kr_b, ctab, stab, q_norm, kv_norm, wq_at, wq_bt, *, tm):
    n = cq.shape[0]
    hw = MLA_HEADS * HEAD_PAD

    def tok(w):
        return pl.BlockSpec((tm, w), lambda i: (i, 0))

    def tok_t(w):
        return pl.BlockSpec((w, tm), lambda i: (0, i))

    def const(shape):
        return pl.BlockSpec(shape, lambda i: (0, 0))

    return pl.pallas_call(
        _mla_q_kernel,
        grid=(n // tm,),
        in_specs=[tok(MLA_LORA), tok(MLA_LORA), tok(HEAD_PAD), tok(HEAD_PAD), tok(HEAD_PAD), tok(HEAD_PAD),
                  tok_t(HEAD_PAD), tok_t(HEAD_PAD),
                  const((1, MLA_LORA)), const((1, MLA_LORA)), const(wq_at.shape), const(wq_bt.shape)],
        out_specs=[tok_t(hw), tok(MLA_LORA), tok(HEAD_PAD)],
        out_shape=[jax.ShapeDtypeStruct((hw, n), BF16),
                   jax.ShapeDtypeStruct((n, MLA_LORA), F32),
                   jax.ShapeDtypeStruct((n, HEAD_PAD), F32)],
        compiler_params=_cparams("parallel"),
        name="mla_q",
    )(cq, ckv, kr_a, kr_b, ctab, stab, ctab.T, stab.T, q_norm.reshape(1, MLA_LORA), kv_norm.reshape(1, MLA_LORA),
      wq_at, wq_bt)


def _mla_kv_kernel(ckvn_ref, krot_ref, wk_ref, wvt_ref, k_ref, vt_ref):
    c = ckvn_ref[...].astype(BF16)
    krot = krot_ref[...]
    for h in range(MLA_HEADS):
        sl = slice(h * HEAD_PAD, (h + 1) * HEAD_PAD)
        k_ref[:, sl] = (_dot(c, wk_ref[:, sl]) + krot).astype(k_ref.dtype)
    vt_ref[...] = _dot_nt(wvt_ref[...], c).astype(vt_ref.dtype)


def mla_kv(ckvn, krot, wuk_p, wuv_t, *, n, tm):
    hw = MLA_HEADS * HEAD_PAD
    vw = MLA_HEADS * MLA_V
    return pl.pallas_call(
        _mla_kv_kernel,
        grid=(n // tm,),
        in_specs=[pl.BlockSpec((tm, MLA_LORA), lambda i: (i, 0)),
                  pl.BlockSpec((tm, HEAD_PAD), lambda i: (i, 0)),
                  pl.BlockSpec(wuk_p.shape, lambda i: (0, 0)),
                  pl.BlockSpec(wuv_t.shape, lambda i: (0, 0))],
        out_specs=[pl.BlockSpec((tm, hw), lambda i: (i, 0)),
                   pl.BlockSpec((vw, tm), lambda i: (0, i))],
        out_shape=[jax.ShapeDtypeStruct((n, hw), BF16),
                   jax.ShapeDtypeStruct((vw, n), BF16)],
        compiler_params=_cparams("parallel"),
        name="mla_kv",
    )(ckvn, krot, wuk_p, wuv_t)


def _flash_kernel(qi_ref, ki_ref, qt_ref, k_ref, vt_ref, ot_ref, m_scr, l_scr, acc_scr, *, tq, tk):
    p_idx = pl.program_id(2)
    qi = qi_ref[p_idx]
    ki = ki_ref[p_idx]

    @pl.when(ki == 0)
    def _():
        m_scr[...] = jnp.full_like(m_scr, NEG_BIG)
        l_scr[...] = jnp.zeros_like(l_scr)
        acc_scr[...] = jnp.zeros_like(acc_scr)

    def step(masked):
        for hh in range(2):
            s = _dot(k_ref[:, hh * HEAD_PAD:(hh + 1) * HEAD_PAD], qt_ref[hh * HEAD_PAD:(hh + 1) * HEAD_PAD, :])
            if masked:
                key = lax.broadcasted_iota(jnp.int32, (tk, tq), 0)
                qry = lax.broadcasted_iota(jnp.int32, (tk, tq), 1)
                s = jnp.where(key <= qry, s, NEG_BIG)
            m_old = m_scr[hh]
            m_new = jnp.maximum(m_old, jnp.max(s, axis=0, keepdims=True))
            alpha = jnp.exp(m_old - m_new)
            p = jnp.exp(s - m_new)
            l_scr[hh] = alpha * l_scr[hh] + jnp.sum(p, axis=0, keepdims=True)
            m_scr[hh] = m_new
            pv = _dot(vt_ref[hh * MLA_V:(hh + 1) * MLA_V, :], p.astype(BF16))
            acc_scr[hh] = alpha * acc_scr[hh] + pv

    @pl.when(ki < qi)
    def _():
        step(False)

    @pl.when(ki == qi)
    def _():
        step(True)
        for hh in range(2):
            ot_ref[hh * MLA_V:(hh + 1) * MLA_V, :] = (acc_scr[hh] / l_scr[hh]).astype(ot_ref.dtype)


def flash_attention(qt, k, vt, *, b, t, tq):
    tk = tq
    nq = t // tq
    pairs = [(i, j) for i in range(nq) for j in range(i + 1)]
    qi = jnp.asarray([p[0] for p in pairs], jnp.int32)
    ki = jnp.asarray([p[1] for p in pairs], jnp.int32)
    n_hp = MLA_HEADS // 2
    grid_spec = pltpu.PrefetchScalarGridSpec(
        num_scalar_prefetch=2,
        grid=(b, n_hp, len(pairs)),
        in_specs=[pl.BlockSpec((2 * HEAD_PAD, tq), lambda bi, hp, p, qi, ki: (hp, bi * nq + qi[p])),
                  pl.BlockSpec((tk, 2 * HEAD_PAD), lambda bi, hp, p, qi, ki: (bi * nq + ki[p], hp)),
                  pl.BlockSpec((2 * MLA_V, tk), lambda bi, hp, p, qi, ki: (hp, bi * nq + ki[p]))],
        out_specs=pl.BlockSpec((2 * MLA_V, tq), lambda bi, hp, p, qi, ki: (hp, bi * nq + qi[p])),
        scratch_shapes=[pltpu.VMEM((2, 1, tq), F32), pltpu.VMEM((2, 1, tq), F32),
                        pltpu.VMEM((2, MLA_V, tq), F32)],
    )
    return pl.pallas_call(
        functools.partial(_flash_kernel, tq=tq, tk=tk),
        grid_spec=grid_spec,
        out_shape=jax.ShapeDtypeStruct((MLA_HEADS * MLA_V, b * t), BF16),
        compiler_params=_cparams("parallel", "parallel", "arbitrary"),
        name="mla_flash",
    )(qi, ki, qt, k, vt)


PAGE = 128
DEC_PB = 32
NEW_PAD = 8


def _q_dec_kernel(qt_ref, w_ref, o_ref):
    for h in range(MLA_HEADS):
        o_ref[:, h * Q_DEC:(h + 1) * Q_DEC] = _dot_tn(qt_ref[h * HEAD_PAD:(h + 1) * HEAD_PAD, :], w_ref[h]).astype(o_ref.dtype)


def mla_q_dec(qt, w_dec):
    n = qt.shape[1]
    return pl.pallas_call(
        _q_dec_kernel,
        grid=(1,),
        in_specs=[pl.BlockSpec(qt.shape, lambda i: (0, 0)), pl.BlockSpec(w_dec.shape, lambda i: (0, 0, 0))],
        out_specs=pl.BlockSpec((n, MLA_HEADS * Q_DEC), lambda i: (0, 0)),
        out_shape=jax.ShapeDtypeStruct((n, MLA_HEADS * Q_DEC), BF16),
        compiler_params=_cparams("arbitrary"),
        name="mla_q_dec",
    )(qt, w_dec)


def _decode_kernel(pt_ref, q_ref, cnew_ref, rnew_ref, cache_c, cache_rt, o_ref, cbuf, rbuf, sem,
                   *, layer, n_pages, t_new, n_seq):
    b = pl.program_id(0)
    n_blk = n_pages // DEC_PB
    rows = q_ref.shape[0]

    def page_copies(pg, slot, i):
        off = pl.ds(pl.multiple_of(i * PAGE, PAGE), PAGE)
        cc = pltpu.make_async_copy(cache_c.at[layer, pg], cbuf.at[slot, off], sem.at[0, slot])
        cr = pltpu.make_async_copy(cache_rt.at[layer, pg], rbuf.at[slot, :, off], sem.at[1, slot])
        return cc, cr

    def start_block(seq, j, slot):
        def body(i, carry):
            cc, cr = page_copies(pt_ref[seq, j * DEC_PB + i], slot, i)
            cc.start()
            cr.start()
            return carry
        lax.fori_loop(0, DEC_PB, body, 0, unroll=8)

    def wait_block(slot):
        def body(i, carry):
            cc, cr = page_copies(0, slot, 0)
            cc.wait()
            cr.wait()
            return carry
        lax.fori_loop(0, DEC_PB, body, 0, unroll=8)

    q = q_ref[...]
    ql = q[:, :MLA_LORA]
    qr = q[:, MLA_LORA:MLA_LORA + MLA_ROPE]

    @pl.when(b == 0)
    def _():
        start_block(0, 0, 0)

    m = jnp.full((rows, 1), NEG_BIG, F32)
    l = jnp.zeros((rows, 1), F32)
    acc = jnp.zeros((rows, MLA_LORA), F32)
    for j in range(n_blk):
        slot = j % 2
        if j + 1 < n_blk:
            start_block(b, j + 1, 1 - slot)
        else:
            @pl.when(b + 1 < n_seq)
            def _():
                start_block(b + 1, 0, 0)
        wait_block(slot)
        kc = cbuf[slot].astype(BF16)
        krt = rbuf[slot].astype(BF16)
        s = _dot_nt(ql, kc) + _dot(qr, krt)
        m_new = jnp.maximum(m, jnp.max(s, axis=-1, keepdims=True))
        alpha = jnp.exp(m - m_new)
        p = jnp.exp(s - m_new)
        l = alpha * l + jnp.sum(p, axis=-1, keepdims=True)
        acc = alpha * acc + _dot(p.astype(BF16), kc)
        m = m_new

    cn = cnew_ref[...].astype(BF16)
    rn = rnew_ref[...].astype(BF16)
    s = _dot_nt(ql, cn) + _dot_nt(qr, rn)
    t_row = lax.broadcasted_iota(jnp.int32, s.shape, 0) // MLA_HEADS
    col = lax.broadcasted_iota(jnp.int32, s.shape, 1)
    s = jnp.where((col <= t_row) & (col < t_new), s, NEG_BIG)
    m_new = jnp.maximum(m, jnp.max(s, axis=-1, keepdims=True))
    alpha = jnp.exp(m - m_new)
    p = jnp.exp(s - m_new)
    l = alpha * l + jnp.sum(p, axis=-1, keepdims=True)
    acc = alpha * acc + _dot(p.astype(BF16), cn)
    o_ref[...] = (acc / l).astype(o_ref.dtype)


def mla_decode(page_table, q_dec, c_new, r_new, cache_ckv, cache_krope_t, *, layer, t_new):
    b, rows, _ = q_dec.shape
    n_pages = page_table.shape[1]
    assert n_pages % (2 * DEC_PB) == 0
    grid_spec = pltpu.PrefetchScalarGridSpec(
        num_scalar_prefetch=1,
        grid=(b,),
        in_specs=[pl.BlockSpec((None, rows, Q_DEC), lambda i, pt: (i, 0, 0)),
                  pl.BlockSpec((None, NEW_PAD, MLA_LORA), lambda i, pt: (i, 0, 0)),
                  pl.BlockSpec((None, NEW_PAD, MLA_ROPE), lambda i, pt: (i, 0, 0)),
                  pl.BlockSpec(memory_space=pl.ANY),
                  pl.BlockSpec(memory_space=pl.ANY)],
        out_specs=pl.BlockSpec((None, rows, MLA_LORA), lambda i, pt: (i, 0, 0)),
        scratch_shapes=[pltpu.VMEM((2, DEC_PB * PAGE, MLA_LORA), F32),
                        pltpu.VMEM((2, MLA_ROPE, DEC_PB * PAGE), F32),
                        pltpu.SemaphoreType.DMA((2, 2))],
    )
    return pl.pallas_call(
        functools.partial(_decode_kernel, layer=layer, n_pages=n_pages, t_new=t_new, n_seq=b),
        grid_spec=grid_spec,
        out_shape=jax.ShapeDtypeStruct((b, rows, MLA_LORA), BF16),
        compiler_params=_cparams("arbitrary"),
        name="mla_decode",
    )(page_table, q_dec, c_new, r_new, cache_ckv, cache_krope_t)


def _sample_v_kernel(o_ref, wt_ref, yt_ref):
    for h in range(MLA_HEADS):
        yt_ref[h * MLA_V:(h + 1) * MLA_V, :] = _dot_nt(wt_ref[h * MLA_V:(h + 1) * MLA_V, :],
                                                       o_ref[:, h * MLA_LORA:(h + 1) * MLA_LORA]).astype(yt_ref.dtype)


def mla_sample_v(o_lat, wuv_t):
    n = o_lat.shape[0]
    return pl.pallas_call(
        _sample_v_kernel,
        grid=(1,),
        in_specs=[pl.BlockSpec(o_lat.shape, lambda i: (0, 0)), pl.BlockSpec(wuv_t.shape, lambda i: (0, 0))],
        out_specs=pl.BlockSpec((MLA_HEADS * MLA_V, n), lambda i: (0, 0)),
        out_shape=jax.ShapeDtypeStruct((MLA_HEADS * MLA_V, n), BF16),
        compiler_params=_cparams("arbitrary"),
        name="mla_sample_v",
    )(o_lat, wuv_t)


def _rmsnorm_kernel(x_ref, g_ref, o_ref):
    x = x_ref[...]
    ms = jnp.mean(x * x, axis=-1, keepdims=True)
    o_ref[...] = x * lax.rsqrt(ms + EPS) * g_ref[...]


def rmsnorm_rows(x, gain, *, tm):
    n, d = x.shape
    return pl.pallas_call(
        _rmsnorm_kernel,
        grid=(n // tm,),
        in_specs=[pl.BlockSpec((tm, d), lambda i: (i, 0)), pl.BlockSpec((1, d), lambda i: (0, 0))],
        out_specs=pl.BlockSpec((tm, d), lambda i: (i, 0)),
        out_shape=jax.ShapeDtypeStruct((n, d), F32),
        compiler_params=_cparams("parallel"),
        name="rmsnorm",
    )(x, gain.reshape(1, d))


HG_CHUNK = 64
SSD_CHUNK = 128
SAMPLE_PAD = 8
ROW_TILE = 256


def _ab_layer(x, n_p, bp, tp, bs, ts, norm_w, lb, st_hg, st_ssm, st_conv, w_in, w_out, hg_norm, conv_w, conv_b,
              dt_bias, a_log, d_skip, ssm_norm):
    sizes = [HG_W, HG_W, HG_W, HG_W, SSM_INNER, CONV_DIM, SSM_HEADS]
    offs = np.concatenate([[0], np.cumsum(sizes)])
    w_pad = jnp.pad(w_in, ((0, 0), (0, LANES - SSM_HEADS))).astype(BF16)
    splits = [(int(offs[j]), int(offs[j + 1])) for j in range(6)] + [(int(offs[6]), int(offs[6]) + LANES)]
    q, f, i_in, g, z, xbc, dt = norm_matmul(x, norm_w, w_pad, splits, [F32] * 7, tm=ROW_TILE)

    def grp(a, prompt):
        if prompt:
            return a
        a = a[n_p:].reshape(bs, ts, a.shape[1])
        return jnp.pad(a, ((0, 0), (0, SAMPLE_PAD - ts), (0, 0))).reshape(bs * SAMPLE_PAD, a.shape[2])

    outs = []
    for prompt in (True, False):
        if prompt:
            b_, t_, c_hg, c_ssd, tv_hg, tv_ssd = bp, tp, HG_CHUNK, SSD_CHUNK, HG_CHUNK, SSD_CHUNK
            s_hg = jnp.zeros((bp, HG_HEADS, HG_D, HG_D), F32)
            s_ssm = jnp.zeros((bp, SSM_HEADS, SSM_P, SSM_N), F32)
            s_conv = jnp.zeros((bp, CONV_W - 1, CONV_DIM), F32)
        else:
            b_, t_, c_hg, c_ssd, tv_hg, tv_ssd = bs, SAMPLE_PAD, SAMPLE_PAD, SAMPLE_PAD, ts, ts
            s_hg, s_ssm, s_conv = st_hg, st_ssm, st_conv
        o_hg, hg_new = hgrn_scan(grp(q, prompt), grp(f, prompt), grp(i_in, prompt), grp(g, prompt), lb, hg_norm,
                                 s_hg, b=b_, t=t_, c=c_hg, t_valid=tv_hg)
        y, conv_new, ssm_new = ssd_scan(grp(z, prompt), grp(xbc, prompt), grp(dt, prompt), s_conv, s_ssm,
                                        conv_w, conv_b, dt_bias, a_log, d_skip, ssm_norm,
                                        b=b_, t=t_, c=c_ssd, t_valid=tv_ssd)
        if not prompt:
            o_hg = o_hg.reshape(bs, SAMPLE_PAD, HG_W)[:, :ts].reshape(bs * ts, HG_W)
            y = y.reshape(bs, SAMPLE_PAD, SSM_INNER)[:, :ts].reshape(bs * ts, SSM_INNER)
        outs.append((o_hg, y, hg_new, ssm_new, conv_new))
    o_hg = jnp.concatenate([outs[0][0], outs[1][0]], axis=0)
    y = jnp.concatenate([outs[0][1], outs[1][1]], axis=0)
    w_out_b = w_out.astype(BF16)
    x = matmul_residual(x, [(o_hg, w_out_b[:HG_W], False), (y, w_out_b[HG_W:], False)], tm=ROW_TILE)
    return x, outs[0][2:], outs[1][2:]


def _mla_layer(x, n_p, bp, tp, bs, ts, past_len, norm_w, cache_ckv, cache_krope, page_table, layer_c,
               w_in, q_norm, kv_norm, w_uq, w_uk, w_uv, w_out):
    w_in_p, wq_a, wq_b, wuk_p, wuv, w_dec = _mla_weights(w_in, w_uq, w_uk, w_uv)
    wuv_t = wuv.T
    splits = [(0, MLA_LORA), (MLA_LORA, 2 * MLA_LORA), (2 * MLA_LORA, 2 * MLA_LORA + HEAD_PAD),
              (2 * MLA_LORA + HEAD_PAD, 2 * MLA_LORA + 2 * HEAD_PAD)]
    cq, ckv, kr_a, kr_b = norm_matmul(x, norm_w, w_in_p, splits, [F32] * 4, tm=ROW_TILE)
    pos = jnp.concatenate([jnp.tile(jnp.arange(tp, dtype=jnp.int32), bp),
                           jnp.tile(past_len + jnp.arange(ts, dtype=jnp.int32), bs)])
    ctab, stab = _rope_tables(pos)
    qt, ckvn, krot = mla_q(cq, ckv, kr_a, kr_b, ctab, stab, q_norm, kv_norm, wq_a.T, wq_b.T, tm=ROW_TILE)
    krope = krot[:, ROPE_AT:ROPE_AT + MLA_ROPE]
    k_p, vt_p = mla_kv(ckvn, krot, wuk_p, wuv_t, n=n_p, tm=ROW_TILE)
    ot_p = flash_attention(qt, k_p, vt_p, b=bp, t=tp, tq=512)
    q_dec = mla_q_dec(qt[:, n_p:], w_dec).reshape(bs, ts * MLA_HEADS, Q_DEC)
    c_new = jnp.pad(ckvn[n_p:].reshape(bs, ts, MLA_LORA), ((0, 0), (0, NEW_PAD - ts), (0, 0)))
    r_new = jnp.pad(krope[n_p:].reshape(bs, ts, MLA_ROPE), ((0, 0), (0, NEW_PAD - ts), (0, 0)))
    cache_krope_t = jnp.swapaxes(cache_krope, 2, 3)
    o_lat = mla_decode(page_table, q_dec, c_new, r_new, cache_ckv, cache_krope_t, layer=layer_c, t_new=ts)
    ot_s = mla_sample_v(o_lat.reshape(bs * ts, MLA_HEADS * MLA_LORA), wuv_t)
    ot = jnp.concatenate([ot_p, ot_s], axis=1)
    x = matmul_residual(x, [(ot, w_out.astype(BF16), True)], tm=ROW_TILE)
    return x, (ckvn[:n_p].reshape(bp, tp, MLA_LORA), krope[:n_p].reshape(bp, tp, MLA_ROPE)), \
        (ckvn[n_p:].reshape(bs, ts, MLA_LORA), krope[n_p:].reshape(bs, ts, MLA_ROPE))


def kernel(x_prompt, x_sample, state_hgrn, state_ssm, state_conv, cache_ckv, cache_krope, page_table,
           norm_mix, norm_ffn, norm_final, w_in_ab, w_out_ab, hgrn_lb, hgrn_norm, conv_w, conv_b,
           dt_bias, a_log, d_skip, ssm_norm, w_in_c, q_norm, kv_norm, w_uq, w_uk, w_uv, w_out_c,
           w_route_group, b_route_group, w_route_expert, b_route_expert, w_gate, w_up, w_down):
    bp, tp, d = x_prompt.shape
    bs, ts, _ = x_sample.shape
    n_p = bp * tp
    depth = norm_mix.shape[0]
    n_a = w_in_ab.shape[0]
    past_len = page_table.shape[1] * cache_ckv.shape[2]
    lb_all = jnp.cumsum(jax.nn.softmax(hgrn_lb.astype(F32), axis=0), axis=0)[:n_a]
    x = jnp.concatenate([x_prompt.reshape(n_p, d), x_sample.reshape(bs * ts, d)], axis=0)
    a_p, a_s, c_p, c_s = [], [], [], []
    for layer in range(depth):
        j = layer // 2
        if layer % 2 == 0:
            x, sp, ss = _ab_layer(x, n_p, bp, tp, bs, ts, norm_mix[layer], lb_all[j], state_hgrn[j], state_ssm[j],
                                  state_conv[j], w_in_ab[j], w_out_ab[j], hgrn_norm[j], conv_w[j], conv_b[j],
                                  dt_bias[j], a_log[j], d_skip[j], ssm_norm[j])
            a_p.append(sp)
            a_s.append(ss)
        else:
            x, cp, cs = _mla_layer(x, n_p, bp, tp, bs, ts, past_len, norm_mix[layer], cache_ckv, cache_krope,
                                   page_table, j, w_in_c[j], q_norm[j], kv_norm[j], w_uq[j], w_uk[j], w_uv[j],
                                   w_out_c[j])
            c_p.append(cp)
            c_s.append(cs)
        x = hier_moe_block(x, norm_ffn[layer], w_route_group[layer], b_route_group[layer], w_route_expert[layer],
                           b_route_expert[layer], w_gate, w_up, w_down, layer)
    y = rmsnorm_rows(x, norm_final, tm=ROW_TILE)

    def stack(items, k):
        return jnp.stack([it[k] for it in items])

    return (y[:n_p].reshape(bp, tp, d), y[n_p:].reshape(bs, ts, d),
            stack(a_p, 0), stack(a_s, 0), stack(a_p, 1), stack(a_s, 1), stack(a_p, 2), stack(a_s, 2),
            stack(c_p, 0), stack(c_s, 0), stack(c_p, 1), stack(c_s, 1))
```

```python
import functools
import math

import jax
import jax.numpy as jnp
import numpy as np
from jax import lax
from jax.experimental import pallas as pl
from jax.experimental.pallas import tpu as pltpu

F32 = jnp.float32
BF16 = jnp.bfloat16

EPS = 1e-6
D_MODEL = 1024
HG_HEADS = 4
HG_D = 128
HG_W = HG_HEADS * HG_D
SSM_HEADS = 16
SSM_P = 64
SSM_N = 128
SSM_GROUPS = 2
SSM_INNER = SSM_HEADS * SSM_P
CONV_W = 4
CONV_DIM = SSM_INNER + 2 * SSM_GROUPS * SSM_N
LANES = 128
SUBLANES = 8
VMEM_LIMIT = 48 * 1024 * 1024
NEG_BIG = -1e30


def _cparams(*sem):
    return pltpu.CompilerParams(dimension_semantics=sem, vmem_limit_bytes=VMEM_LIMIT)


def _dot(a, b):
    return jnp.dot(a, b, preferred_element_type=F32)


def _dot_nt(a, b):
    return lax.dot_general(a, b, (((1,), (1,)), ((), ())), preferred_element_type=F32)


def _dot_tn(a, b):
    return lax.dot_general(a, b, (((0,), (0,)), ((), ())), preferred_element_type=F32)


def _split3(x):
    hi = x.astype(BF16)
    r1 = x - hi.astype(F32)
    mid = r1.astype(BF16)
    lo = (r1 - mid.astype(F32)).astype(BF16)
    return hi, mid, lo


def _silu(x):
    return x * (1.0 / (1.0 + jnp.exp(-x)))


def _sigmoid(x):
    return 1.0 / (1.0 + jnp.exp(-x))


def _norm_matmul_kernel(x_ref, g_ref, w_ref, *out_refs, splits, normalize):
    x = x_ref[...].astype(F32)
    if normalize:
        ms = jnp.mean(x * x, axis=-1, keepdims=True)
        x = x * lax.rsqrt(ms + EPS) * g_ref[...]
    h = x.astype(BF16)
    for (a, b), o_ref in zip(splits, out_refs):
        o_ref[...] = _dot(h, w_ref[:, a:b]).astype(o_ref.dtype)


def norm_matmul(x, gain, w, splits, out_dtypes, *, tm, normalize=True):
    n, k = x.shape
    assert n % tm == 0
    kern = functools.partial(_norm_matmul_kernel, splits=tuple(splits), normalize=normalize)
    out_shape = [jax.ShapeDtypeStruct((n, b - a), dt) for (a, b), dt in zip(splits, out_dtypes)]
    out_specs = [pl.BlockSpec((tm, b - a), lambda i: (i, 0)) for (a, b) in splits]
    return pl.pallas_call(
        kern,
        grid=(n // tm,),
        in_specs=[pl.BlockSpec((tm, k), lambda i: (i, 0)),
                  pl.BlockSpec((1, k), lambda i: (0, 0)),
                  pl.BlockSpec(w.shape, lambda i: (0, 0))],
        out_specs=out_specs,
        out_shape=out_shape,
        compiler_params=_cparams("parallel"),
        name="norm_matmul",
    )(x, gain.reshape(1, k), w)


def _matmul_residual_kernel(*refs, transposed):
    n_pairs = len(transposed)
    res_ref = refs[0]
    o_ref = refs[1 + 2 * n_pairs]
    acc = res_ref[...]
    for j in range(n_pairs):
        a_ref, w_ref = refs[1 + 2 * j], refs[2 + 2 * j]
        a = a_ref[...].astype(BF16)
        acc = acc + (_dot_tn(a, w_ref[...]) if transposed[j] else _dot(a, w_ref[...]))
    o_ref[...] = acc


def matmul_residual(res, pairs, *, tm):
    n, d = res.shape
    assert n % tm == 0
    in_specs = [pl.BlockSpec((tm, d), lambda i: (i, 0))]
    args = [res]
    for a, w, tr in pairs:
        if tr:
            in_specs.append(pl.BlockSpec((a.shape[0], tm), lambda i: (0, i)))
        else:
            in_specs.append(pl.BlockSpec((tm, a.shape[1]), lambda i: (i, 0)))
        in_specs.append(pl.BlockSpec(w.shape, lambda i: (0, 0)))
        args += [a, w]
    return pl.pallas_call(
        functools.partial(_matmul_residual_kernel, transposed=tuple(bool(p[2]) for p in pairs)),
        grid=(n // tm,),
        in_specs=in_specs,
        out_specs=pl.BlockSpec((tm, d), lambda i: (i, 0)),
        out_shape=jax.ShapeDtypeStruct((n, d), F32),
        compiler_params=_cparams("parallel"),
        name="matmul_residual",
    )(*args)


def _hgrn_level_halves(c):
    halves = []
    b = c // 2
    while b >= 1:
        halves.append(b)
        b //= 2
    return halves


@functools.lru_cache(maxsize=None)
def _hgrn_consts(c):
    t = np.arange(c)
    rows = [t[None, :] <= t[:, None], t[None, :] > t[:, None]]
    masks = []
    for b in _hgrn_level_halves(c):
        blk = t // b
        st = blk * b
        en = st + b - 1
        odd = blk % 2 == 1
        even = ~odd
        rows.append(odd[:, None] & (t[None, :] >= st[:, None]) & (t[None, :] <= t[:, None]))
        rows.append(even[:, None] & (t[None, :] > t[:, None]) & (t[None, :] <= en[:, None]))
        masks.append((t[:, None] // (2 * b) == t[None, :] // (2 * b)) & odd[:, None] & even[None, :])
    masks.append(np.eye(c, dtype=bool))
    rows.append(np.ones((SUBLANES, c), dtype=bool))
    dg = np.concatenate(rows, axis=0).astype(np.float32)
    dg3 = np.concatenate([dg, dg, dg], axis=1)
    mk = np.stack(masks).astype(np.float32)
    return dg3, mk


def _hgrn_kernel(q_ref, f_ref, i_ref, g_ref, lb_ref, nw_ref, dg_ref, mk_ref, s0_ref,
                 o_ref, sfin_ref, s_scr, *, c, t_valid, n_chunks, n_seq):
    ci = pl.program_id(1)

    @pl.when(ci == 0)
    def _():
        s_scr[...] = s0_ref[...]

    for si in range(n_seq):
        rows = slice(si * c, (si + 1) * c)
        _hgrn_chunk(q_ref.at[rows], f_ref.at[rows], i_ref.at[rows], g_ref.at[rows], lb_ref, nw_ref, dg_ref, mk_ref,
                    o_ref.at[rows], s_scr.at[si], c=c, t_valid=t_valid)

    @pl.when(ci == n_chunks - 1)
    def _():
        sfin_ref[...] = s_scr[...]


def _hgrn_chunk(q_ref, f_ref, i_ref, g_ref, lb_ref, nw_ref, dg_ref, mk_ref, o_ref, s_scr, *, c, t_valid):
    lb = lb_ref[...]
    fl = f_ref[...]
    sig = _sigmoid(fl)
    logf = jnp.log(lb + (1.0 - lb) * sig)
    kk = (1.0 - lb) * (1.0 - sig)
    if t_valid < c:
        row = lax.broadcasted_iota(jnp.int32, (c, HG_W), 0)
        live = row < t_valid
        logf = jnp.where(live, logf, 0.0)
        kk = jnp.where(live, kk, 0.0)
    qa = _silu(q_ref[...])
    vv = i_ref[...].astype(BF16)

    lf3 = jnp.concatenate(_split3(logf), axis=0)
    ex = jnp.exp(_dot(dg_ref[...], lf3))
    halves = _hgrn_level_halves(c)
    n_lv = len(halves)
    e_cum = ex[0:c]
    e_rev = ex[c:2 * c]
    q_state = (qa * e_cum).astype(BF16)
    k_state = (kk * e_rev).astype(BF16)
    q_lv = [(qa * ex[(2 + 2 * l) * c:(3 + 2 * l) * c]).astype(BF16) for l in range(n_lv)]
    k_lv = [(kk * ex[(3 + 2 * l) * c:(4 + 2 * l) * c]).astype(BF16) for l in range(n_lv)]
    q_lv.append(qa.astype(BF16))
    k_lv.append(kk.astype(BF16))
    ones_t = jnp.ones((3 * c, HG_D), BF16)
    gate = _silu(g_ref[...])
    nw = nw_ref[...]

    for h in range(HG_HEADS):
        sl = slice(h * HG_D, (h + 1) * HG_D)
        sc = jnp.zeros((c, c), F32)
        for l in range(n_lv + 1):
            sc = sc + mk_ref[l] * _dot_nt(q_lv[l][:, sl], k_lv[l][:, sl])
        s_h = s_scr[h]
        o_h = _dot(sc.astype(BF16), vv[:, sl]) + _dot(q_state[:, sl], s_h.astype(BF16))
        dec = jnp.exp(_dot_tn(lf3[:, sl], ones_t))
        s_scr[h] = s_h * dec + _dot_tn(k_state[:, sl], vv[:, sl])
        ms = jnp.mean(o_h * o_h, axis=-1, keepdims=True)
        o_h = o_h * lax.rsqrt(ms + EPS) * nw[:, sl] * gate[:, sl]
        o_ref[:, sl] = o_h.astype(o_ref.dtype)


def hgrn_scan(q, f, i_in, g, lb, norm_w, s0, *, b, t, c, t_valid, n_seq=1):
    n_chunks = t // c
    assert t % c == 0 and (t_valid == c or n_chunks == 1) and q.shape[0] >= b * t
    assert b % n_seq == 0 and (n_seq == 1 or n_chunks == 1)
    dg3, mk = _hgrn_consts(c)
    dg3 = jnp.asarray(dg3, BF16)
    mk = jnp.asarray(mk, F32)
    tok = pl.BlockSpec((n_seq * c, HG_W), lambda bi, ci: (bi * n_chunks + ci, 0))
    st = pl.BlockSpec((n_seq, HG_HEADS, HG_D, HG_D), lambda bi, ci: (bi, 0, 0, 0))
    row = pl.BlockSpec((1, HG_W), lambda bi, ci: (0, 0))
    kern = functools.partial(_hgrn_kernel, c=c, t_valid=t_valid, n_chunks=n_chunks, n_seq=n_seq)
    return pl.pallas_call(
        kern,
        grid=(b // n_seq, n_chunks),
        in_specs=[tok, tok, tok, tok, row, row,
                  pl.BlockSpec(dg3.shape, lambda bi, ci: (0, 0)),
                  pl.BlockSpec(mk.shape, lambda bi, ci: (0, 0, 0)),
                  st],
        out_specs=[tok, st],
        out_shape=[jax.ShapeDtypeStruct((b * t, HG_W), BF16),
                   jax.ShapeDtypeStruct((b, HG_HEADS, HG_D, HG_D), F32)],
        scratch_shapes=[pltpu.VMEM((n_seq, HG_HEADS, HG_D, HG_D), F32)],
        compiler_params=_cparams("parallel", "arbitrary"),
        name="hgrn_scan",
    )(q, f, i_in, g, lb.reshape(1, HG_W), norm_w.reshape(1, HG_W), dg3, mk, s0)


CONV_HEAD = 8


@functools.lru_cache(maxsize=None)
def _ssd_consts(c):
    t = np.arange(c)
    tri = (t[None, :] <= t[:, None]).astype(np.float32)
    rev = (t[None, :] > t[:, None]).astype(np.float32)
    ones = np.ones((SUBLANES, c), np.float32)
    tg = np.concatenate([tri, rev, ones], axis=0)
    tg3 = np.concatenate([tg, tg, tg], axis=1)
    u3 = np.concatenate([tri.T, tri.T, tri.T], axis=0)
    return tg3, u3


def _ssd_kernel(z_ref, xbc_ref, dt_ref, cbuf_ref, cw_ref, cb_ref, dtb_ref, alog_ref, dsk_ref, nw_ref,
                tg_ref, u_ref, h0_ref,
                y_ref, cnew_ref, hfin_ref, xp_scr, h_scr, y_scr, *, c, t_valid, n_chunks, n_seq):
    ci = pl.program_id(1)

    @pl.when(ci == 0)
    def _():
        h_scr[...] = h0_ref[...]
        xp_scr[:, CONV_HEAD - (CONV_W - 1):CONV_HEAD, :] = cbuf_ref[...]

    for si in range(n_seq):
        rows = slice(si * c, (si + 1) * c)
        _ssd_chunk(ci, z_ref.at[rows], xbc_ref.at[rows], dt_ref.at[rows], cw_ref, cb_ref, dtb_ref, alog_ref, dsk_ref,
                   nw_ref, tg_ref, u_ref, y_ref.at[rows], cnew_ref.at[si], xp_scr.at[si], h_scr.at[si], y_scr.at[si],
                   c=c, t_valid=t_valid, n_chunks=n_chunks)

    @pl.when(ci == n_chunks - 1)
    def _():
        hfin_ref[...] = h_scr[...]


def _ssd_chunk(ci, z_ref, xbc_ref, dt_ref, cw_ref, cb_ref, dtb_ref, alog_ref, dsk_ref, nw_ref, tg_ref, u_ref,
               y_ref, cnew_ref, xp_scr, h_scr, y_scr, *, c, t_valid, n_chunks):
    xp_scr[CONV_HEAD:CONV_HEAD + c, :] = xbc_ref[...]
    conv = cb_ref[...]
    for i in range(CONV_W):
        off = CONV_HEAD - (CONV_W - 1) + i
        conv = conv + xp_scr[off:off + c, :] * cw_ref[i:i + 1, :]
    conv = _silu(conv)

    @pl.when(ci == n_chunks - 1)
    def _():
        cnew_ref[...] = xp_scr[CONV_HEAD + t_valid - (CONV_W - 1):CONV_HEAD + t_valid, :]

    xp_scr[CONV_HEAD - (CONV_W - 1):CONV_HEAD, :] = xp_scr[CONV_HEAD + c - (CONV_W - 1):CONV_HEAD + c, :]

    xs = conv[:, :SSM_INNER]
    dt_raw = dt_ref[...] + dtb_ref[...]
    dt = jnp.maximum(dt_raw, 0.0) + jnp.log(1.0 + jnp.exp(-jnp.abs(dt_raw)))
    if t_valid < c:
        row = lax.broadcasted_iota(jnp.int32, (c, LANES), 0)
        dt = jnp.where(row < t_valid, dt, 0.0)
    a = -jnp.exp(alog_ref[...])
    da = dt * a
    da3 = jnp.concatenate(_split3(da), axis=0)
    xx = _dot(tg_ref[...], da3)
    cum = xx[0:c]
    e_cum = jnp.exp(cum)
    w_all = jnp.exp(xx[c:2 * c]) * dt
    e_last = jnp.exp(xx[2 * c:2 * c + 1])
    cum_t = _dot_tn(da3, u_ref[...])

    tril = lax.broadcasted_iota(jnp.int32, (c, c), 0) >= lax.broadcasted_iota(jnp.int32, (c, c), 1)
    heads_per_group = SSM_HEADS // SSM_GROUPS
    for g in range(SSM_GROUPS):
        bm = conv[:, SSM_INNER + g * SSM_N:SSM_INNER + (g + 1) * SSM_N]
        cm = conv[:, SSM_INNER + (SSM_GROUPS + g) * SSM_N:SSM_INNER + (SSM_GROUPS + g + 1) * SSM_N]
        bm_b = bm.astype(BF16)
        cb = _dot_nt(cm.astype(BF16), bm_b)
        for hh in range(heads_per_group):
            h = g * heads_per_group + hh
            x_h = xs[:, h * SSM_P:(h + 1) * SSM_P]
            diff = cum[:, h:h + 1] - cum_t[h:h + 1, :]
            seg = jnp.exp(jnp.where(tril, diff, NEG_BIG))
            xd = (dt[:, h:h + 1] * x_h).astype(BF16)
            cme = (cm * e_cum[:, h:h + 1]).astype(BF16)
            h_h = h_scr[h]
            y_h = _dot((cb * seg).astype(BF16), xd) + _dot_nt(cme, h_h.astype(BF16))
            xw = (w_all[:, h:h + 1] * x_h).astype(BF16)
            h_scr[h] = h_h * e_last[:, h:h + 1] + _dot_tn(xw, bm_b)
            y_scr[:, h * SSM_P:(h + 1) * SSM_P] = y_h

    y = (y_scr[...] + dsk_ref[...] * xs) * _silu(z_ref[...])
    gw = SSM_INNER // SSM_GROUPS
    for g in range(SSM_GROUPS):
        yg = y[:, g * gw:(g + 1) * gw]
        ms = jnp.mean(yg * yg, axis=-1, keepdims=True)
        y_ref[:, g * gw:(g + 1) * gw] = (yg * lax.rsqrt(ms + EPS) * nw_ref[:, g * gw:(g + 1) * gw]).astype(y_ref.dtype)


def ssd_scan(z, xbc, dt, conv_buf, h0, conv_w, conv_b, dt_bias, a_log, d_skip, norm_w, *, b, t, c, t_valid,
             n_seq=1):
    n_chunks = t // c
    assert t % c == 0 and (t_valid == c or n_chunks == 1) and z.shape[0] >= b * t
    assert b % n_seq == 0 and (n_seq == 1 or n_chunks == 1)
    tg3, u3 = _ssd_consts(c)
    tg3 = jnp.asarray(tg3, BF16)
    u3 = jnp.asarray(u3, BF16)

    def pad_heads(v):
        return jnp.pad(v.astype(F32), (0, LANES - SSM_HEADS)).reshape(1, LANES)

    def tok(w):
        return pl.BlockSpec((n_seq * c, w), lambda bi, ci: (bi * n_chunks + ci, 0))

    def const(shape):
        return pl.BlockSpec(shape, lambda bi, ci: (0,) * len(shape))

    cst = pl.BlockSpec((n_seq, CONV_W - 1, CONV_DIM), lambda bi, ci: (bi, 0, 0))
    hst = pl.BlockSpec((n_seq, SSM_HEADS, SSM_P, SSM_N), lambda bi, ci: (bi, 0, 0, 0))
    kern = functools.partial(_ssd_kernel, c=c, t_valid=t_valid, n_chunks=n_chunks, n_seq=n_seq)
    return pl.pallas_call(
        kern,
        grid=(b // n_seq, n_chunks),
        in_specs=[tok(SSM_INNER), tok(CONV_DIM), tok(LANES), cst,
                  const((CONV_W, CONV_DIM)), const((1, CONV_DIM)), const((1, LANES)), const((1, LANES)),
                  const((1, SSM_INNER)), const((1, SSM_INNER)), const(tg3.shape), const(u3.shape), hst],
        out_specs=[tok(SSM_INNER), cst, hst],
        out_shape=[jax.ShapeDtypeStruct((b * t, SSM_INNER), BF16),
                   jax.ShapeDtypeStruct((b, CONV_W - 1, CONV_DIM), F32),
                   jax.ShapeDtypeStruct((b, SSM_HEADS, SSM_P, SSM_N), F32)],
        scratch_shapes=[pltpu.VMEM((n_seq, CONV_HEAD + c, CONV_DIM), F32),
                        pltpu.VMEM((n_seq, SSM_HEADS, SSM_P, SSM_N), F32),
                        pltpu.VMEM((n_seq, c, SSM_INNER), F32)],
        compiler_params=_cparams("parallel", "arbitrary"),
        name="ssd_scan",
    )(z, xbc, dt, conv_buf, conv_w, conv_b.reshape(1, CONV_DIM), pad_heads(dt_bias), pad_heads(a_log),
      jnp.repeat(d_skip.astype(F32), SSM_P).reshape(1, SSM_INNER), norm_w.reshape(1, SSM_INNER), tg3, u3, h0)


N_GROUPS = 4
EXPERTS_PER_GROUP = 8
N_EXPERTS = N_GROUPS * EXPERTS_PER_GROUP
TOP_K = 2
D_EXPERT = 256
MOE_TM = 256


def _router_kernel(x_ref, g_ref, w_ref, b_ref, tri_ref, xn_ref, rw_ref, ri_ref, cnt_ref, carry_scr):
    @pl.when(pl.program_id(0) == 0)
    def _():
        carry_scr[...] = jnp.zeros_like(carry_scr)

    x = x_ref[...]
    ms = jnp.mean(x * x, axis=-1, keepdims=True)
    xn = x * lax.rsqrt(ms + EPS) * g_ref[...]
    for s in range(SUBLANES):
        xn_ref[pl.ds(s, x.shape[0], stride=SUBLANES), :] = xn[:, s * LANES:(s + 1) * LANES]
    logits = jnp.dot(xn, w_ref[...], precision=lax.Precision.HIGHEST, preferred_element_type=F32) + b_ref[...]
    lane = lax.broadcasted_iota(jnp.int32, logits.shape, 1)
    is_g = (lane >= N_EXPERTS) & (lane < N_EXPERTS + N_GROUPS)
    gl = jnp.where(is_g, logits, NEG_BIG)
    gmax = jnp.max(gl, axis=-1, keepdims=True)
    g_sel = jnp.min(jnp.where(gl == gmax, lane, 4 * LANES), axis=-1, keepdims=True) - N_EXPERTS
    g_w = 1.0 / jnp.sum(jnp.where(is_g, jnp.exp(gl - gmax), 0.0), axis=-1, keepdims=True)
    lo = g_sel * EXPERTS_PER_GROUP
    in_grp = (lane >= lo) & (lane < lo + EXPERTS_PER_GROUP)
    el = jnp.where(in_grp, logits, NEG_BIG)
    emax = jnp.max(el, axis=-1, keepdims=True)
    ee = jnp.where(in_grp, jnp.exp(el - emax), 0.0)
    p = ee / jnp.sum(ee, axis=-1, keepdims=True)
    p = jnp.where(in_grp, p, -1.0)
    p1 = jnp.max(p, axis=-1, keepdims=True)
    i1 = jnp.min(jnp.where(p == p1, lane, 4 * LANES), axis=-1, keepdims=True)
    p_rest = jnp.where(lane == i1, -1.0, p)
    p2 = jnp.max(p_rest, axis=-1, keepdims=True)
    i2 = jnp.min(jnp.where(p_rest == p2, lane, 4 * LANES), axis=-1, keepdims=True)
    w1 = p1 / (p1 + p2) * g_w
    w2 = p2 / (p1 + p2) * g_w
    rw_ref[...] = jnp.where(lane == 0, w1, jnp.where(lane == 1, w2, 0.0))
    hit1 = lane == i1
    hit2 = lane == i2
    onehot = (hit1 | hit2).astype(BF16)
    before = _dot(tri_ref[...], onehot) + carry_scr[0:1, :]
    r1 = jnp.sum(jnp.where(hit1, before, 0.0), axis=-1, keepdims=True).astype(jnp.int32)
    r2 = jnp.sum(jnp.where(hit2, before, 0.0), axis=-1, keepdims=True).astype(jnp.int32)
    ri_ref[...] = jnp.where(lane == 0, i1, jnp.where(lane == 1, i2, jnp.where(lane == 2, r1, jnp.where(lane == 3, r2, 0))))
    carry_scr[...] = carry_scr[...] + jnp.sum(onehot.astype(F32), axis=0, keepdims=True)
    cnt_ref[...] = carry_scr[...]


def moe_router(x, gain, w_rg, b_rg, w_re, b_re, *, tm):
    n, d = x.shape
    assert d == SUBLANES * LANES
    w = jnp.zeros((d, LANES), F32).at[:, :N_EXPERTS].set(w_re).at[:, N_EXPERTS:N_EXPERTS + N_GROUPS].set(w_rg)
    b = jnp.zeros((1, LANES), F32).at[0, :N_EXPERTS].set(b_re).at[0, N_EXPERTS:N_EXPERTS + N_GROUPS].set(b_rg)
    t = np.arange(tm)
    tri = jnp.asarray(t[None, :] < t[:, None], BF16)
    return pl.pallas_call(
        _router_kernel,
        grid=(n // tm,),
        in_specs=[pl.BlockSpec((tm, d), lambda i: (i, 0)),
                  pl.BlockSpec((1, d), lambda i: (0, 0)),
                  pl.BlockSpec((d, LANES), lambda i: (0, 0)),
                  pl.BlockSpec((1, LANES), lambda i: (0, 0)),
                  pl.BlockSpec((tm, tm), lambda i: (0, 0))],
        out_specs=[pl.BlockSpec((tm * SUBLANES, LANES), lambda i: (i, 0)),
                   pl.BlockSpec((tm, LANES), lambda i: (i, 0)),
                   pl.BlockSpec((tm, LANES), lambda i: (i, 0)),
                   pl.BlockSpec((SUBLANES, LANES), lambda i: (0, 0))],
        out_shape=[jax.ShapeDtypeStruct((n * SUBLANES, LANES), F32),
                   jax.ShapeDtypeStruct((n, LANES), F32),
                   jax.ShapeDtypeStruct((n, LANES), jnp.int32),
                   jax.ShapeDtypeStruct((SUBLANES, LANES), F32)],
        scratch_shapes=[pltpu.VMEM((SUBLANES, LANES), F32)],
        compiler_params=_cparams("arbitrary"),
        name="moe_router",
    )(x, gain.reshape(1, d), w, b, tri)


def _token_rows(idx):
    return pl.ds(pl.multiple_of(idx * SUBLANES, SUBLANES), SUBLANES)


def _moe_dispatch_kernel(dest_ref, xn_ref, zero_ref, xs_ref, sem, *, tm):
    del zero_ref
    base = pl.program_id(0) * (tm * TOP_K)

    def start(r, carry):
        for k in range(TOP_K):
            pltpu.make_async_copy(xn_ref.at[_token_rows(r)], xs_ref.at[_token_rows(dest_ref[base + r * TOP_K + k])],
                                  sem).start()
        return carry

    lax.fori_loop(0, tm, start, 0, unroll=8)

    def wait(r, carry):
        for k in range(TOP_K):
            pltpu.make_async_copy(xn_ref.at[_token_rows(0)], xs_ref.at[_token_rows(0)], sem).wait()
        return carry

    lax.fori_loop(0, tm, wait, 0, unroll=8)


def moe_dispatch(dest, xn, n_rows, *, tm):
    n = xn.shape[0] // SUBLANES
    grid_spec = pltpu.PrefetchScalarGridSpec(
        num_scalar_prefetch=1,
        grid=(n // tm,),
        in_specs=[pl.BlockSpec((tm * SUBLANES, LANES), lambda i, d: (i, 0)),
                  pl.BlockSpec(memory_space=pl.ANY)],
        out_specs=pl.BlockSpec(memory_space=pl.ANY),
        scratch_shapes=[pltpu.SemaphoreType.DMA(())],
    )
    return pl.pallas_call(
        functools.partial(_moe_dispatch_kernel, tm=tm),
        grid_spec=grid_spec,
        out_shape=jax.ShapeDtypeStruct((n_rows * SUBLANES, LANES), F32),
        input_output_aliases={2: 0},
        compiler_params=_cparams("arbitrary"),
        name="moe_dispatch",
    )(dest, xn, jnp.zeros((n_rows * SUBLANES, LANES), F32))


def _moe_ffn_kernel(te_ref, nu_ref, x_ref, wg_ref, wu_ref, wd_ref, y_ref):
    i = pl.program_id(0)

    @pl.when(i < nu_ref[0])
    def _():
        x = jnp.concatenate([x_ref[pl.ds(s, MOE_TM, stride=SUBLANES), :] for s in range(SUBLANES)], axis=1).astype(BF16)
        a = _dot(x, wg_ref[...].astype(BF16))
        b = _dot(x, wu_ref[...].astype(BF16))
        hid = (_silu(a) * b).astype(BF16)
        y = _dot(hid, wd_ref[...].astype(BF16))
        for s in range(SUBLANES):
            y_ref[pl.ds(s, MOE_TM, stride=SUBLANES), :] = y[:, s * LANES:(s + 1) * LANES]

    @pl.when(i >= nu_ref[0])
    def _():
        y_ref[...] = jnp.zeros_like(y_ref)


def moe_ffn(tile_expert, n_used, x_sorted, w_gate, w_up, w_down, layer):
    n_tiles = x_sorted.shape[0] // (MOE_TM * SUBLANES)
    d = w_gate.shape[2]
    tile = pl.BlockSpec((MOE_TM * SUBLANES, LANES), lambda i, te, nu: (i, 0))
    grid_spec = pltpu.PrefetchScalarGridSpec(
        num_scalar_prefetch=2,
        grid=(n_tiles,),
        in_specs=[tile,
                  pl.BlockSpec((None, None, d, D_EXPERT), lambda i, te, nu: (layer, te[i], 0, 0)),
                  pl.BlockSpec((None, None, d, D_EXPERT), lambda i, te, nu: (layer, te[i], 0, 0)),
                  pl.BlockSpec((None, None, D_EXPERT, d), lambda i, te, nu: (layer, te[i], 0, 0))],
        out_specs=tile,
    )
    return pl.pallas_call(
        _moe_ffn_kernel,
        grid_spec=grid_spec,
        out_shape=jax.ShapeDtypeStruct(x_sorted.shape, F32),
        compiler_params=_cparams("arbitrary"),
        name="moe_ffn",
    )(tile_expert, n_used, x_sorted, w_gate, w_up, w_down)


def _moe_combine_kernel(dest_ref, x_ref, rw_ref, fg_ref, ys_ref, o_ref, ybuf, sem, *, tm, n_steps, final_norm):
    i = pl.program_id(0)
    slot = i % 2
    slot_rows = tm * TOP_K * SUBLANES

    def issue(step, to_slot):
        base = step * (tm * TOP_K)

        def body(r, carry):
            for k in range(TOP_K):
                dst = pl.ds(pl.multiple_of(to_slot * slot_rows + (r * TOP_K + k) * SUBLANES, SUBLANES), SUBLANES)
                pltpu.make_async_copy(ys_ref.at[_token_rows(dest_ref[base + r * TOP_K + k])], ybuf.at[dst],
                                      sem.at[to_slot]).start()
            return carry

        lax.fori_loop(0, tm, body, 0, unroll=8)

    @pl.when(i == 0)
    def _():
        issue(0, 0)

    @pl.when(i + 1 < n_steps)
    def _():
        issue(i + 1, 1 - slot)

    def wait(r, carry):
        for k in range(TOP_K):
            pltpu.make_async_copy(ys_ref.at[_token_rows(0)], ybuf.at[_token_rows(0)], sem.at[slot]).wait()
        return carry

    lax.fori_loop(0, tm, wait, 0, unroll=8)

    rw = rw_ref[...]
    g0 = rw[:, 0:1]
    g1 = rw[:, 1:2]
    first = pl.multiple_of(slot * slot_rows, SUBLANES)
    pieces = []
    for s in range(SUBLANES):
        y0 = ybuf[pl.ds(first + s, tm, stride=TOP_K * SUBLANES), :]
        y1 = ybuf[pl.ds(first + SUBLANES + s, tm, stride=TOP_K * SUBLANES), :]
        pieces.append(x_ref[:, s * LANES:(s + 1) * LANES] + (g0 * y0 + g1 * y1))
    if final_norm:
        sq = sum(jnp.sum(p * p, axis=-1, keepdims=True) for p in pieces)
        scale = lax.rsqrt(sq * (1.0 / (SUBLANES * LANES)) + EPS)
        pieces = [p * scale * fg_ref[:, s * LANES:(s + 1) * LANES] for s, p in enumerate(pieces)]
    for s, p in enumerate(pieces):
        o_ref[:, s * LANES:(s + 1) * LANES] = p


def moe_combine(dest, x, rw, y_sorted, final_gain, *, tm, final_norm):
    n, d = x.shape
    n_steps = n // tm
    grid_spec = pltpu.PrefetchScalarGridSpec(
        num_scalar_prefetch=1,
        grid=(n_steps,),
        in_specs=[pl.BlockSpec((tm, d), lambda i, dd: (i, 0)),
                  pl.BlockSpec((tm, LANES), lambda i, dd: (i, 0)),
                  pl.BlockSpec((1, d), lambda i, dd: (0, 0)),
                  pl.BlockSpec(memory_space=pl.ANY)],
        out_specs=pl.BlockSpec((tm, d), lambda i, dd: (i, 0)),
        scratch_shapes=[pltpu.VMEM((2 * tm * TOP_K * SUBLANES, LANES), F32),
                        pltpu.SemaphoreType.DMA((2,))],
    )
    return pl.pallas_call(
        functools.partial(_moe_combine_kernel, tm=tm, n_steps=n_steps, final_norm=final_norm),
        grid_spec=grid_spec,
        out_shape=jax.ShapeDtypeStruct((n, d), F32),
        compiler_params=_cparams("arbitrary"),
        name="moe_combine",
    )(dest, x, rw, final_gain.reshape(1, d), y_sorted)


def _moe_plan(ri, counts, n):
    eid = ri[:, :TOP_K]
    rank = ri[:, TOP_K:2 * TOP_K]
    counts = counts[0, :N_EXPERTS].astype(jnp.int32)
    tiles = (counts + MOE_TM - 1) // MOE_TM
    tile_end = jnp.cumsum(tiles)
    tile_start = tile_end - tiles
    onehot = eid[:, :, None] == jnp.arange(N_EXPERTS, dtype=jnp.int32)[None, None, :]
    dest = jnp.sum(jnp.where(onehot, tile_start[None, None, :], 0), axis=-1) * MOE_TM + rank
    n_rows = TOP_K * n + N_EXPERTS * MOE_TM
    n_tiles = n_rows // MOE_TM
    n_used = tile_end[-1]
    t_idx = jnp.minimum(jnp.arange(n_tiles, dtype=jnp.int32), n_used - 1)
    tile_expert = jnp.sum((t_idx[:, None] >= tile_end[None, :]).astype(jnp.int32), axis=1)
    return dest.reshape(-1).astype(jnp.int32), tile_expert.astype(jnp.int32), n_used.reshape(1).astype(jnp.int32), n_rows


def hier_moe_block(x, gain, w_rg, b_rg, w_re, b_re, w_gate, w_up, w_down, layer, final_gain, final_norm):
    n, d = x.shape
    xn, rw, ri, counts = moe_router(x, gain, w_rg, b_rg, w_re, b_re, tm=512)
    dest, tile_expert, n_used, n_rows = _moe_plan(ri, counts, n)
    x_sorted = moe_dispatch(dest, xn, n_rows, tm=512)
    y_sorted = moe_ffn(tile_expert, n_used, x_sorted, w_gate, w_up, w_down, layer)
    return moe_combine(dest, x, rw, y_sorted, final_gain, tm=256, final_norm=final_norm)


MLA_HEADS = 16
MLA_LORA = 256
MLA_NOPE = 64
MLA_ROPE = 32
MLA_V = 64
MLA_SCALE = (MLA_NOPE + MLA_ROPE) ** -0.5
Q_SCALE = MLA_SCALE * math.log2(math.e)
ROPE_THETA = 10000.0
HEAD_PAD = 128
ROPE_AT = MLA_NOPE
Q_DEC = 384


def _rope_tables(pos):
    half = MLA_ROPE // 2
    inv = ROPE_THETA ** (-jnp.arange(half, dtype=F32) / half)
    ang = pos.astype(F32)[:, None] * inv[None, :]
    cos, sin = jnp.cos(ang), jnp.sin(ang)
    n = pos.shape[0]
    ones = jnp.ones((n, MLA_NOPE), F32)
    zeros_n = jnp.zeros((n, MLA_NOPE), F32)
    zeros_p = jnp.zeros((n, HEAD_PAD - MLA_NOPE - MLA_ROPE), F32)
    ctab = jnp.concatenate([ones, cos, cos, zeros_p], axis=1)
    stab = jnp.concatenate([zeros_n, sin, sin, zeros_p], axis=1)
    return ctab, stab


def _mla_weights(w_in, w_uq, w_uk, w_uv):
    d = w_in.shape[0]
    half = MLA_ROPE // 2
    w_kr = w_in[:, 2 * MLA_LORA:]
    zl = jnp.zeros((d, ROPE_AT), F32)
    zr = jnp.zeros((d, HEAD_PAD - ROPE_AT - MLA_ROPE), F32)
    kr_a = jnp.concatenate([zl, w_kr, zr], axis=1)
    kr_b = jnp.concatenate([zl, -w_kr[:, half:], w_kr[:, :half], zr], axis=1)
    w_in_p = jnp.concatenate([w_in[:, :2 * MLA_LORA], kr_a, kr_b], axis=1).astype(BF16)
    wq = w_uq.reshape(MLA_LORA, MLA_HEADS, MLA_NOPE + MLA_ROPE)
    nope, x1, x2 = wq[..., :MLA_NOPE], wq[..., MLA_NOPE:MLA_NOPE + half], wq[..., MLA_NOPE + half:]
    zp = jnp.zeros((MLA_LORA, MLA_HEADS, HEAD_PAD - MLA_NOPE - MLA_ROPE), F32)
    wq_a = jnp.concatenate([nope, x1, x2, zp], axis=-1).reshape(MLA_LORA, MLA_HEADS * HEAD_PAD).astype(BF16)
    wq_b = jnp.concatenate([jnp.zeros_like(nope), -x2, x1, zp], axis=-1).reshape(MLA_LORA, MLA_HEADS * HEAD_PAD).astype(BF16)
    zk = jnp.zeros((MLA_LORA, MLA_HEADS, HEAD_PAD - MLA_NOPE), F32)
    wuk_p = jnp.concatenate([w_uk, zk], axis=-1).reshape(MLA_LORA, MLA_HEADS * HEAD_PAD).astype(BF16)
    wuv = w_uv.reshape(MLA_LORA, MLA_HEADS * MLA_V).astype(BF16)
    absorb = jnp.transpose(w_uk, (1, 2, 0))
    sel = jnp.zeros((MLA_ROPE, Q_DEC - MLA_LORA), F32).at[jnp.arange(MLA_ROPE), jnp.arange(MLA_ROPE)].set(1.0)
    top = jnp.concatenate([absorb, jnp.zeros((MLA_HEADS, MLA_NOPE, Q_DEC - MLA_LORA), F32)], axis=-1)
    mid = jnp.broadcast_to(jnp.concatenate([jnp.zeros((MLA_ROPE, MLA_LORA), F32), sel], axis=-1)[None],
                           (MLA_HEADS, MLA_ROPE, Q_DEC))
    bot = jnp.zeros((MLA_HEADS, HEAD_PAD - MLA_NOPE - MLA_ROPE, Q_DEC), F32)
    w_dec = jnp.concatenate([top, mid, bot], axis=1).astype(BF16)
    return w_in_p, wq_a, wq_b, wuk_p, wuv, w_dec


def _mla_q_kernel(cq_ref, ckv_ref, kra_ref, krb_ref, ct_ref, st_ref, ctt_ref, stt_ref, qn_ref, kvn_ref,
                  wa_ref, wb_ref, qt_ref, ckvn_ref, krot_ref):
    cq = cq_ref[...]
    ms = jnp.mean(cq * cq, axis=-1, keepdims=True)
    cqn = (cq * lax.rsqrt(ms + EPS) * qn_ref[...]).astype(BF16)
    ckv = ckv_ref[...]
    ms2 = jnp.mean(ckv * ckv, axis=-1, keepdims=True)
    ckvn_ref[...] = ckv * lax.rsqrt(ms2 + EPS) * kvn_ref[...]
    krot_ref[...] = kra_ref[...] * ct_ref[...] + krb_ref[...] * st_ref[...]
    ctt = ctt_ref[...]
    stt = stt_ref[...]
    for h in range(MLA_HEADS):
        sl = slice(h * HEAD_PAD, (h + 1) * HEAD_PAD)
        qh = _dot_nt(wa_ref[sl, :], cqn) * ctt + _dot_nt(wb_ref[sl, :], cqn) * stt
        qt_ref[sl, :] = (qh * Q_SCALE).astype(qt_ref.dtype)


def mla_q(cq, ckv, kr_a, kr_b, ctab, stab, q_norm, kv_norm, wq_at, wq_bt, *, tm):
    n = cq.shape[0]
    hw = MLA_HEADS * HEAD_PAD

    def tok(w):
        return pl.BlockSpec((tm, w), lambda i: (i, 0))

    def tok_t(w):
        return pl.BlockSpec((w, tm), lambda i: (0, i))

    def const(shape):
        return pl.BlockSpec(shape, lambda i: (0, 0))

    return pl.pallas_call(
        _mla_q_kernel,
        grid=(n // tm,),
        in_specs=[tok(MLA_LORA), tok(MLA_LORA), tok(HEAD_PAD), tok(HEAD_PAD), tok(HEAD_PAD), tok(HEAD_PAD),
                  tok_t(HEAD_PAD), tok_t(HEAD_PAD),
                  const((1, MLA_LORA)), const((1, MLA_LORA)), const(wq_at.shape), const(wq_bt.shape)],
        out_specs=[tok_t(hw), tok(MLA_LORA), tok(HEAD_PAD)],
        out_shape=[jax.ShapeDtypeStruct((hw, n), BF16),
                   jax.ShapeDtypeStruct((n, MLA_LORA), F32),
                   jax.ShapeDtypeStruct((n, HEAD_PAD), F32)],
        compiler_params=_cparams("parallel"),
        name="mla_q",
    )(cq, ckv, kr_a, kr_b, ctab, stab, ctab.T, stab.T, q_norm.reshape(1, MLA_LORA), kv_norm.reshape(1, MLA_LORA),
      wq_at, wq_bt)


def _mla_kv_kernel(ckvn_ref, krot_ref, wk_ref, wvt_ref, k_ref, vt_ref):
    c = ckvn_ref[...].astype(BF16)
    krot = krot_ref[...]
    for h in range(MLA_HEADS):
        sl = slice(h * HEAD_PAD, (h + 1) * HEAD_PAD)
        k_ref[:, sl] = (_dot(c, wk_ref[:, sl]) + krot).astype(k_ref.dtype)
    vt_ref[...] = _dot_nt(wvt_ref[...], c).astype(vt_ref.dtype)


def mla_kv(ckvn, krot, wuk_p, wuv_t, *, n, tm):
    hw = MLA_HEADS * HEAD_PAD
    vw = MLA_HEADS * MLA_V
    return pl.pallas_call(
        _mla_kv_kernel,
        grid=(n // tm,),
        in_specs=[pl.BlockSpec((tm, MLA_LORA), lambda i: (i, 0)),
                  pl.BlockSpec((tm, HEAD_PAD), lambda i: (i, 0)),
                  pl.BlockSpec(wuk_p.shape, lambda i: (0, 0)),
                  pl.BlockSpec(wuv_t.shape, lambda i: (0, 0))],
        out_specs=[pl.BlockSpec((tm, hw), lambda i: (i, 0)),
                   pl.BlockSpec((None, vw, tm), lambda i: (i, 0, 0))],
        out_shape=[jax.ShapeDtypeStruct((n, hw), BF16),
                   jax.ShapeDtypeStruct((n // tm, vw, tm), BF16)],
        compiler_params=_cparams("parallel"),
        name="mla_kv",
    )(ckvn, krot, wuk_p, wuv_t)


def _flash_kernel(qt_ref, k_ref, vt_ref, ot_ref, m_scr, l_scr, acc_scr, p_scr, alpha_scr, *, tq, tkb):
    qi = pl.program_id(2)
    sub = tq // tkb
    m_scr[...] = jnp.full_like(m_scr, NEG_BIG)
    l_scr[...] = jnp.zeros_like(l_scr)
    acc_scr[...] = jnp.zeros_like(acc_scr)
    p_scr[...] = jnp.zeros_like(p_scr)
    alpha_scr[...] = jnp.ones_like(alpha_scr)

    def retire(kb_prev):
        for hh in range(2):
            pv = _dot(vt_ref[kb_prev, hh * MLA_V:(hh + 1) * MLA_V, :], p_scr[hh])
            acc_scr[hh] = alpha_scr[hh] * acc_scr[hh] + pv

    def key_block(kb, diag_off):
        rows = pl.ds(pl.multiple_of(kb * tkb, tkb), tkb)
        strips = [(hh, st) for hh in range(2) for st in range(tq // FLASH_STRIP)]
        scores = [_dot(k_ref[rows, hh * HEAD_PAD:(hh + 1) * HEAD_PAD],
                       qt_ref[hh * HEAD_PAD:(hh + 1) * HEAD_PAD, st * FLASH_STRIP:(st + 1) * FLASH_STRIP])
                  for hh, st in strips]
        retire(jnp.maximum(kb - 1, 0))
        for (hh, st), s in zip(strips, scores):
            cols = slice(st * FLASH_STRIP, (st + 1) * FLASH_STRIP)
            if diag_off is not None:
                key = lax.broadcasted_iota(jnp.int32, s.shape, 0) + diag_off
                qry = lax.broadcasted_iota(jnp.int32, s.shape, 1) + st * FLASH_STRIP
                s = jnp.where(key <= qry, s, NEG_BIG)
            m_old = m_scr[hh, :, cols]
            m_new = jnp.maximum(m_old, jnp.max(s, axis=0, keepdims=True))
            alpha = jnp.exp2(m_old - m_new)
            p = jnp.exp2(s - m_new)
            l_scr[hh, :, cols] = alpha * l_scr[hh, :, cols] + jnp.sum(p, axis=0, keepdims=True)
            m_scr[hh, :, cols] = m_new
            p_scr[hh, :, cols] = p.astype(BF16)
            alpha_scr[hh, :, cols] = alpha

    def full_tile(j, carry):
        for d in range(sub):
            key_block(j * sub + d, None)
        return carry

    lax.fori_loop(0, qi, full_tile, 0)
    for d in range(sub):
        key_block(qi * sub + d, d * tkb)
    retire(qi * sub + sub - 1)
    for hh in range(2):
        ot_ref[hh * MLA_V:(hh + 1) * MLA_V, :] = (acc_scr[hh] / l_scr[hh]).astype(ot_ref.dtype)


def flash_attention(qt, k, vt, *, b, t, tq):
    tkb = vt.shape[2]
    nq = t // tq
    nkb = t // tkb
    assert tq % tkb == 0 and t % tq == 0
    n_hp = MLA_HEADS // 2
    return pl.pallas_call(
        functools.partial(_flash_kernel, tq=tq, tkb=tkb),
        grid=(b, n_hp, nq),
        in_specs=[pl.BlockSpec((2 * HEAD_PAD, tq), lambda bi, hp, qi: (hp, bi * nq + qi)),
                  pl.BlockSpec((t, 2 * HEAD_PAD), lambda bi, hp, qi: (bi, hp)),
                  pl.BlockSpec((nkb, 2 * MLA_V, tkb), lambda bi, hp, qi: (bi, hp, 0))],
        out_specs=pl.BlockSpec((2 * MLA_V, tq), lambda bi, hp, qi: (hp, bi * nq + qi)),
        out_shape=jax.ShapeDtypeStruct((MLA_HEADS * MLA_V, b * t), BF16),
        scratch_shapes=[pltpu.VMEM((2, 1, tq), F32), pltpu.VMEM((2, 1, tq), F32),
                        pltpu.VMEM((2, MLA_V, tq), F32), pltpu.VMEM((2, tkb, tq), BF16),
                        pltpu.VMEM((2, 1, tq), F32)],
        compiler_params=_cparams("parallel", "parallel", "arbitrary"),
        name="mla_flash",
    )(qt, k, vt)


PAGE = 128
DEC_PB = 32
NEW_PAD = 8


def _q_dec_kernel(qt_ref, w_ref, o_ref):
    for h in range(MLA_HEADS):
        o_ref[:, h * Q_DEC:(h + 1) * Q_DEC] = _dot_tn(qt_ref[h * HEAD_PAD:(h + 1) * HEAD_PAD, :], w_ref[h]).astype(o_ref.dtype)


def mla_q_dec(qt, w_dec):
    n = qt.shape[1]
    return pl.pallas_call(
        _q_dec_kernel,
        grid=(1,),
        in_specs=[pl.BlockSpec(qt.shape, lambda i: (0, 0)), pl.BlockSpec(w_dec.shape, lambda i: (0, 0, 0))],
        out_specs=pl.BlockSpec((n, MLA_HEADS * Q_DEC), lambda i: (0, 0)),
        out_shape=jax.ShapeDtypeStruct((n, MLA_HEADS * Q_DEC), BF16),
        compiler_params=_cparams("arbitrary"),
        name="mla_q_dec",
    )(qt, w_dec)


def _decode_kernel(pt_ref, q_ref, cnew_ref, rnew_ref, cache_c, cache_rt, o_ref, cbuf, rbuf, sem,
                   *, layer, n_pages, t_new, n_seq):
    b = pl.program_id(0)
    n_blk = n_pages // DEC_PB
    rows = q_ref.shape[0]

    def page_copies(pg, slot, i):
        off = pl.ds(pl.multiple_of(i * PAGE, PAGE), PAGE)
        cc = pltpu.make_async_copy(cache_c.at[layer, pg], cbuf.at[slot, off], sem.at[0, slot])
        cr = pltpu.make_async_copy(cache_rt.at[layer, pg], rbuf.at[slot, :, off], sem.at[1, slot])
        return cc, cr

    def start_block(seq, j, slot):
        def body(i, carry):
            cc, cr = page_copies(pt_ref[seq, j * DEC_PB + i], slot, i)
            cc.start()
            cr.start()
            return carry
        lax.fori_loop(0, DEC_PB, body, 0, unroll=8)

    def wait_block(slot):
        def body(i, carry):
            cc, cr = page_copies(0, slot, 0)
            cc.wait()
            cr.wait()
            return carry
        lax.fori_loop(0, DEC_PB, body, 0, unroll=8)

    q = q_ref[...]
    ql = q[:, :MLA_LORA]
    qr = q[:, MLA_LORA:MLA_LORA + MLA_ROPE]

    @pl.when(b == 0)
    def _():
        start_block(0, 0, 0)

    m = jnp.full((rows, 1), NEG_BIG, F32)
    l = jnp.zeros((rows, 1), F32)
    acc = jnp.zeros((rows, MLA_LORA), F32)
    for j in range(n_blk):
        slot = j % 2
        if j + 1 < n_blk:
            start_block(b, j + 1, 1 - slot)
        else:
            @pl.when(b + 1 < n_seq)
            def _():
                start_block(b + 1, 0, 0)
        wait_block(slot)
        kc = cbuf[slot].astype(BF16)
        krt = rbuf[slot].astype(BF16)
        s = _dot_nt(ql, kc) + _dot(qr, krt)
        m_new = jnp.maximum(m, jnp.max(s, axis=-1, keepdims=True))
        alpha = jnp.exp2(m - m_new)
        p = jnp.exp2(s - m_new)
        l = alpha * l + jnp.sum(p, axis=-1, keepdims=True)
        acc = alpha * acc + _dot(p.astype(BF16), kc)
        m = m_new

    cn = cnew_ref[...].astype(BF16)
    rn = rnew_ref[...].astype(BF16)
    s = _dot_nt(ql, cn) + _dot_nt(qr, rn)
    t_row = lax.broadcasted_iota(jnp.int32, s.shape, 0) // MLA_HEADS
    col = lax.broadcasted_iota(jnp.int32, s.shape, 1)
    s = jnp.where((col <= t_row) & (col < t_new), s, NEG_BIG)
    m_new = jnp.maximum(m, jnp.max(s, axis=-1, keepdims=True))
    alpha = jnp.exp2(m - m_new)
    p = jnp.exp2(s - m_new)
    l = alpha * l + jnp.sum(p, axis=-1, keepdims=True)
    acc = alpha * acc + _dot(p.astype(BF16), cn)
    o_ref[...] = (acc / l).astype(o_ref.dtype)


def mla_decode(page_table, q_dec, c_new, r_new, cache_ckv, cache_krope_t, *, layer, t_new):
    b, rows, _ = q_dec.shape
    n_pages = page_table.shape[1]
    assert n_pages % (2 * DEC_PB) == 0
    grid_spec = pltpu.PrefetchScalarGridSpec(
        num_scalar_prefetch=1,
        grid=(b,),
        in_specs=[pl.BlockSpec((None, rows, Q_DEC), lambda i, pt: (i, 0, 0)),
                  pl.BlockSpec((None, NEW_PAD, MLA_LORA), lambda i, pt: (i, 0, 0)),
                  pl.BlockSpec((None, NEW_PAD, MLA_ROPE), lambda i, pt: (i, 0, 0)),
                  pl.BlockSpec(memory_space=pl.ANY),
                  pl.BlockSpec(memory_space=pl.ANY)],
        out_specs=pl.BlockSpec((None, rows, MLA_LORA), lambda i, pt: (i, 0, 0)),
        scratch_shapes=[pltpu.VMEM((2, DEC_PB * PAGE, MLA_LORA), F32),
                        pltpu.VMEM((2, MLA_ROPE, DEC_PB * PAGE), F32),
                        pltpu.SemaphoreType.DMA((2, 2))],
    )
    return pl.pallas_call(
        functools.partial(_decode_kernel, layer=layer, n_pages=n_pages, t_new=t_new, n_seq=b),
        grid_spec=grid_spec,
        out_shape=jax.ShapeDtypeStruct((b, rows, MLA_LORA), BF16),
        compiler_params=_cparams("arbitrary"),
        name="mla_decode",
    )(page_table, q_dec, c_new, r_new, cache_ckv, cache_krope_t)


def _sample_v_kernel(o_ref, wt_ref, yt_ref):
    for h in range(MLA_HEADS):
        yt_ref[h * MLA_V:(h + 1) * MLA_V, :] = _dot_nt(wt_ref[h * MLA_V:(h + 1) * MLA_V, :],
                                                       o_ref[:, h * MLA_LORA:(h + 1) * MLA_LORA]).astype(yt_ref.dtype)


def mla_sample_v(o_lat, wuv_t):
    n = o_lat.shape[0]
    return pl.pallas_call(
        _sample_v_kernel,
        grid=(1,),
        in_specs=[pl.BlockSpec(o_lat.shape, lambda i: (0, 0)), pl.BlockSpec(wuv_t.shape, lambda i: (0, 0))],
        out_specs=pl.BlockSpec((MLA_HEADS * MLA_V, n), lambda i: (0, 0)),
        out_shape=jax.ShapeDtypeStruct((MLA_HEADS * MLA_V, n), BF16),
        compiler_params=_cparams("arbitrary"),
        name="mla_sample_v",
    )(o_lat, wuv_t)


HG_CHUNK = 64
SSD_CHUNK = 128
SAMPLE_PAD = 8
SAMPLE_SEQS = 4
FLASH_QUERIES = 512
FLASH_KEYS = 512
FLASH_STRIP = 256
ROW_TILE = 256


def _ab_layer(x, n_p, bp, tp, bs, ts, norm_w, lb, st_hg, st_ssm, st_conv, w_in, w_out, hg_norm, conv_w, conv_b,
              dt_bias, a_log, d_skip, ssm_norm):
    sizes = [HG_W, HG_W, HG_W, HG_W, SSM_INNER, CONV_DIM, SSM_HEADS]
    offs = np.concatenate([[0], np.cumsum(sizes)])
    w_pad = jnp.pad(w_in, ((0, 0), (0, LANES - SSM_HEADS))).astype(BF16)
    splits = [(int(offs[j]), int(offs[j + 1])) for j in range(6)] + [(int(offs[6]), int(offs[6]) + LANES)]
    q, f, i_in, g, z, xbc, dt = norm_matmul(x, norm_w, w_pad, splits, [F32] * 7, tm=ROW_TILE)

    def grp(a, prompt):
        if prompt:
            return a
        a = a[n_p:].reshape(bs, ts, a.shape[1])
        return jnp.pad(a, ((0, 0), (0, SAMPLE_PAD - ts), (0, 0))).reshape(bs * SAMPLE_PAD, a.shape[2])

    outs = []
    for prompt in (True, False):
        if prompt:
            b_, t_, c_hg, c_ssd, tv_hg, tv_ssd = bp, tp, HG_CHUNK, SSD_CHUNK, HG_CHUNK, SSD_CHUNK
            s_hg = jnp.zeros((bp, HG_HEADS, HG_D, HG_D), F32)
            s_ssm = jnp.zeros((bp, SSM_HEADS, SSM_P, SSM_N), F32)
            s_conv = jnp.zeros((bp, CONV_W - 1, CONV_DIM), F32)
        else:
            b_, t_, c_hg, c_ssd, tv_hg, tv_ssd = bs, SAMPLE_PAD, SAMPLE_PAD, SAMPLE_PAD, ts, ts
            s_hg, s_ssm, s_conv = st_hg, st_ssm, st_conv
        n_seq = 1 if prompt else math.gcd(bs, SAMPLE_SEQS)
        o_hg, hg_new = hgrn_scan(grp(q, prompt), grp(f, prompt), grp(i_in, prompt), grp(g, prompt), lb, hg_norm,
                                 s_hg, b=b_, t=t_, c=c_hg, t_valid=tv_hg, n_seq=n_seq)
        y, conv_new, ssm_new = ssd_scan(grp(z, prompt), grp(xbc, prompt), grp(dt, prompt), s_conv, s_ssm,
                                        conv_w, conv_b, dt_bias, a_log, d_skip, ssm_norm,
                                        b=b_, t=t_, c=c_ssd, t_valid=tv_ssd, n_seq=n_seq)
        if not prompt:
            o_hg = o_hg.reshape(bs, SAMPLE_PAD, HG_W)[:, :ts].reshape(bs * ts, HG_W)
            y = y.reshape(bs, SAMPLE_PAD, SSM_INNER)[:, :ts].reshape(bs * ts, SSM_INNER)
        outs.append((o_hg, y, hg_new, ssm_new, conv_new))
    o_hg = jnp.concatenate([outs[0][0], outs[1][0]], axis=0)
    y = jnp.concatenate([outs[0][1], outs[1][1]], axis=0)
    w_out_b = w_out.astype(BF16)
    x = matmul_residual(x, [(o_hg, w_out_b[:HG_W], False), (y, w_out_b[HG_W:], False)], tm=ROW_TILE)
    return x, outs[0][2:], outs[1][2:]


def _mla_layer(x, n_p, bp, tp, bs, ts, past_len, norm_w, cache_ckv, cache_krope, page_table, layer_c,
               w_in, q_norm, kv_norm, w_uq, w_uk, w_uv, w_out):
    w_in_p, wq_a, wq_b, wuk_p, wuv, w_dec = _mla_weights(w_in, w_uq, w_uk, w_uv)
    wuv_t = wuv.T
    splits = [(0, MLA_LORA), (MLA_LORA, 2 * MLA_LORA), (2 * MLA_LORA, 2 * MLA_LORA + HEAD_PAD),
              (2 * MLA_LORA + HEAD_PAD, 2 * MLA_LORA + 2 * HEAD_PAD)]
    cq, ckv, kr_a, kr_b = norm_matmul(x, norm_w, w_in_p, splits, [F32] * 4, tm=ROW_TILE)
    pos = jnp.concatenate([jnp.tile(jnp.arange(tp, dtype=jnp.int32), bp),
                           jnp.tile(past_len + jnp.arange(ts, dtype=jnp.int32), bs)])
    ctab, stab = _rope_tables(pos)
    qt, ckvn, krot = mla_q(cq, ckv, kr_a, kr_b, ctab, stab, q_norm, kv_norm, wq_a.T, wq_b.T, tm=ROW_TILE)
    krope = krot[:, ROPE_AT:ROPE_AT + MLA_ROPE]
    k_p, vt_p = mla_kv(ckvn, krot, wuk_p, wuv_t, n=n_p, tm=math.gcd(tp, FLASH_KEYS))
    ot_p = flash_attention(qt, k_p, vt_p, b=bp, t=tp, tq=math.gcd(tp, FLASH_QUERIES))
    q_dec = mla_q_dec(qt[:, n_p:], w_dec).reshape(bs, ts * MLA_HEADS, Q_DEC)
    c_new = jnp.pad(ckvn[n_p:].reshape(bs, ts, MLA_LORA), ((0, 0), (0, NEW_PAD - ts), (0, 0)))
    r_new = jnp.pad(krope[n_p:].reshape(bs, ts, MLA_ROPE), ((0, 0), (0, NEW_PAD - ts), (0, 0)))
    cache_krope_t = jnp.swapaxes(cache_krope, 2, 3)
    o_lat = mla_decode(page_table, q_dec, c_new, r_new, cache_ckv, cache_krope_t, layer=layer_c, t_new=ts)
    ot_s = mla_sample_v(o_lat.reshape(bs * ts, MLA_HEADS * MLA_LORA), wuv_t)
    ot = jnp.concatenate([ot_p, ot_s], axis=1)
    x = matmul_residual(x, [(ot, w_out.astype(BF16), True)], tm=ROW_TILE)
    return x, (ckvn[:n_p].reshape(bp, tp, MLA_LORA), krope[:n_p].reshape(bp, tp, MLA_ROPE)), \
        (ckvn[n_p:].reshape(bs, ts, MLA_LORA), krope[n_p:].reshape(bs, ts, MLA_ROPE))


def kernel(x_prompt, x_sample, state_hgrn, state_ssm, state_conv, cache_ckv, cache_krope, page_table,
           norm_mix, norm_ffn, norm_final, w_in_ab, w_out_ab, hgrn_lb, hgrn_norm, conv_w, conv_b,
           dt_bias, a_log, d_skip, ssm_norm, w_in_c, q_norm, kv_norm, w_uq, w_uk, w_uv, w_out_c,
           w_route_group, b_route_group, w_route_expert, b_route_expert, w_gate, w_up, w_down):
    bp, tp, d = x_prompt.shape
    bs, ts, _ = x_sample.shape
    n_p = bp * tp
    depth = norm_mix.shape[0]
    n_a = w_in_ab.shape[0]
    past_len = page_table.shape[1] * cache_ckv.shape[2]
    lb_all = jnp.cumsum(jax.nn.softmax(hgrn_lb.astype(F32), axis=0), axis=0)[:n_a]
    x = jnp.concatenate([x_prompt.reshape(n_p, d), x_sample.reshape(bs * ts, d)], axis=0)
    a_p, a_s, c_p, c_s = [], [], [], []
    for layer in range(depth):
        j = layer // 2
        if layer % 2 == 0:
            x, sp, ss = _ab_layer(x, n_p, bp, tp, bs, ts, norm_mix[layer], lb_all[j], state_hgrn[j], state_ssm[j],
                                  state_conv[j], w_in_ab[j], w_out_ab[j], hgrn_norm[j], conv_w[j], conv_b[j],
                                  dt_bias[j], a_log[j], d_skip[j], ssm_norm[j])
            a_p.append(sp)
            a_s.append(ss)
        else:
            x, cp, cs = _mla_layer(x, n_p, bp, tp, bs, ts, past_len, norm_mix[layer], cache_ckv, cache_krope,
                                   page_table, j, w_in_c[j], q_norm[j], kv_norm[j], w_uq[j], w_uk[j], w_uv[j],
                                   w_out_c[j])
            c_p.append(cp)
            c_s.append(cs)
        x = hier_moe_block(x, norm_ffn[layer], w_route_group[layer], b_route_group[layer], w_route_expert[layer],
                           b_route_expert[layer], w_gate, w_up, w_down, layer, norm_final,
                           final_norm=(layer == depth - 1))
    y = x

    def stack(items, k):
        return jnp.stack([it[k] for it in items])

    return (y[:n_p].reshape(bp, tp, d), y[n_p:].reshape(bs, ts, d),
            stack(a_p, 0), stack(a_s, 0), stack(a_p, 1), stack(a_s, 1), stack(a_p, 2), stack(a_s, 2),
            stack(c_p, 0), stack(c_s, 0), stack(c_p, 1), stack(c_s, 1))
```

```python
import functools
import math

import jax
import jax.numpy as jnp
import numpy as np
from jax import lax
from jax.experimental import pallas as pl
from jax.experimental.pallas import tpu as pltpu

F32 = jnp.float32
BF16 = jnp.bfloat16

EPS = 1e-6
D_MODEL = 1024
HG_HEADS = 4
HG_D = 128
HG_W = HG_HEADS * HG_D
SSM_HEADS = 16
SSM_P = 64
SSM_N = 128
SSM_GROUPS = 2
SSM_INNER = SSM_HEADS * SSM_P
CONV_W = 4
CONV_DIM = SSM_INNER + 2 * SSM_GROUPS * SSM_N
LANES = 128
SUBLANES = 8
VMEM_LIMIT = 48 * 1024 * 1024
NEG_BIG = -1e30


def _cparams(*sem):
    return pltpu.CompilerParams(dimension_semantics=sem, vmem_limit_bytes=VMEM_LIMIT)


def _dot(a, b):
    return jnp.dot(a, b, preferred_element_type=F32)


def _dot_nt(a, b):
    return lax.dot_general(a, b, (((1,), (1,)), ((), ())), preferred_element_type=F32)


def _dot_tn(a, b):
    return lax.dot_general(a, b, (((0,), (0,)), ((), ())), preferred_element_type=F32)


def _split3(x):
    hi = x.astype(BF16)
    r1 = x - hi.astype(F32)
    mid = r1.astype(BF16)
    lo = (r1 - mid.astype(F32)).astype(BF16)
    return hi, mid, lo


def _silu(x):
    return x * (1.0 / (1.0 + jnp.exp(-x)))


def _sigmoid(x):
    return 1.0 / (1.0 + jnp.exp(-x))


def _norm_matmul_kernel(x_ref, g_ref, w_ref, *out_refs, splits, normalize):
    x = x_ref[...].astype(F32)
    if normalize:
        ms = jnp.mean(x * x, axis=-1, keepdims=True)
        x = x * lax.rsqrt(ms + EPS) * g_ref[...]
    h = x.astype(BF16)
    for (a, b), o_ref in zip(splits, out_refs):
        o_ref[...] = _dot(h, w_ref[:, a:b]).astype(o_ref.dtype)


def norm_matmul(x, gain, w, splits, out_dtypes, *, tm, normalize=True):
    n, k = x.shape
    assert n % tm == 0
    kern = functools.partial(_norm_matmul_kernel, splits=tuple(splits), normalize=normalize)
    out_shape = [jax.ShapeDtypeStruct((n, b - a), dt) for (a, b), dt in zip(splits, out_dtypes)]
    out_specs = [pl.BlockSpec((tm, b - a), lambda i: (i, 0)) for (a, b) in splits]
    return pl.pallas_call(
        kern,
        grid=(n // tm,),
        in_specs=[pl.BlockSpec((tm, k), lambda i: (i, 0)),
                  pl.BlockSpec((1, k), lambda i: (0, 0)),
                  pl.BlockSpec(w.shape, lambda i: (0, 0))],
        out_specs=out_specs,
        out_shape=out_shape,
        compiler_params=_cparams("parallel"),
        name="norm_matmul",
    )(x, gain.reshape(1, k), w)


def _matmul_residual_kernel(*refs, transposed):
    n_pairs = len(transposed)
    res_ref = refs[0]
    o_ref = refs[1 + 2 * n_pairs]
    acc = res_ref[...]
    for j in range(n_pairs):
        a_ref, w_ref = refs[1 + 2 * j], refs[2 + 2 * j]
        a = a_ref[...].astype(BF16)
        acc = acc + (_dot_tn(a, w_ref[...]) if transposed[j] else _dot(a, w_ref[...]))
    o_ref[...] = acc


def matmul_residual(res, pairs, *, tm):
    n, d = res.shape
    assert n % tm == 0
    in_specs = [pl.BlockSpec((tm, d), lambda i: (i, 0))]
    args = [res]
    for a, w, tr in pairs:
        if tr:
            in_specs.append(pl.BlockSpec((a.shape[0], tm), lambda i: (0, i)))
        else:
            in_specs.append(pl.BlockSpec((tm, a.shape[1]), lambda i: (i, 0)))
        in_specs.append(pl.BlockSpec(w.shape, lambda i: (0, 0)))
        args += [a, w]
    return pl.pallas_call(
        functools.partial(_matmul_residual_kernel, transposed=tuple(bool(p[2]) for p in pairs)),
        grid=(n // tm,),
        in_specs=in_specs,
        out_specs=pl.BlockSpec((tm, d), lambda i: (i, 0)),
        out_shape=jax.ShapeDtypeStruct((n, d), F32),
        compiler_params=_cparams("parallel"),
        name="matmul_residual",
    )(*args)


def _hgrn_level_halves(c):
    halves = []
    b = c // 2
    while b >= 1:
        halves.append(b)
        b //= 2
    return halves


@functools.lru_cache(maxsize=None)
def _hgrn_consts(c):
    t = np.arange(c)
    rows = [t[None, :] <= t[:, None], t[None, :] > t[:, None]]
    masks = []
    for b in _hgrn_level_halves(c):
        blk = t // b
        st = blk * b
        en = st + b - 1
        odd = blk % 2 == 1
        even = ~odd
        rows.append(odd[:, None] & (t[None, :] >= st[:, None]) & (t[None, :] <= t[:, None]))
        rows.append(even[:, None] & (t[None, :] > t[:, None]) & (t[None, :] <= en[:, None]))
        masks.append((t[:, None] // (2 * b) == t[None, :] // (2 * b)) & odd[:, None] & even[None, :])
    masks.append(np.eye(c, dtype=bool))
    rows.append(np.ones((SUBLANES, c), dtype=bool))
    dg = np.concatenate(rows, axis=0).astype(np.float32)
    dg3 = np.concatenate([dg, dg, dg], axis=1)
    mk = np.stack(masks).astype(np.float32)
    return dg3, mk


def _hgrn_kernel(q_ref, f_ref, i_ref, g_ref, lb_ref, nw_ref, dg_ref, mk_ref, s0_ref,
                 o_ref, sfin_ref, s_scr, *, c, t_valid, n_chunks, n_seq):
    ci = pl.program_id(1)

    @pl.when(ci == 0)
    def _():
        s_scr[...] = s0_ref[...]

    for si in range(n_seq):
        rows = slice(si * c, (si + 1) * c)
        _hgrn_chunk(q_ref.at[rows], f_ref.at[rows], i_ref.at[rows], g_ref.at[rows], lb_ref, nw_ref, dg_ref, mk_ref,
                    o_ref.at[rows], s_scr.at[si], c=c, t_valid=t_valid)

    @pl.when(ci == n_chunks - 1)
    def _():
        sfin_ref[...] = s_scr[...]


def _hgrn_chunk(q_ref, f_ref, i_ref, g_ref, lb_ref, nw_ref, dg_ref, mk_ref, o_ref, s_scr, *, c, t_valid):
    lb = lb_ref[...]
    fl = f_ref[...]
    sig = _sigmoid(fl)
    logf = jnp.log(lb + (1.0 - lb) * sig)
    kk = (1.0 - lb) * (1.0 - sig)
    if t_valid < c:
        row = lax.broadcasted_iota(jnp.int32, (c, HG_W), 0)
        live = row < t_valid
        logf = jnp.where(live, logf, 0.0)
        kk = jnp.where(live, kk, 0.0)
    qa = _silu(q_ref[...])
    vv = i_ref[...].astype(BF16)

    lf3 = jnp.concatenate(_split3(logf), axis=0)
    ex = jnp.exp(_dot(dg_ref[...], lf3))
    halves = _hgrn_level_halves(c)
    n_lv = len(halves)
    e_cum = ex[0:c]
    e_rev = ex[c:2 * c]
    q_state = (qa * e_cum).astype(BF16)
    k_state = (kk * e_rev).astype(BF16)
    q_lv = [(qa * ex[(2 + 2 * l) * c:(3 + 2 * l) * c]).astype(BF16) for l in range(n_lv)]
    k_lv = [(kk * ex[(3 + 2 * l) * c:(4 + 2 * l) * c]).astype(BF16) for l in range(n_lv)]
    q_lv.append(qa.astype(BF16))
    k_lv.append(kk.astype(BF16))
    ones_t = jnp.ones((3 * c, HG_D), BF16)
    gate = _silu(g_ref[...])
    nw = nw_ref[...]

    for h in range(HG_HEADS):
        sl = slice(h * HG_D, (h + 1) * HG_D)
        sc = jnp.zeros((c, c), F32)
        for l in range(n_lv + 1):
            sc = sc + mk_ref[l] * _dot_nt(q_lv[l][:, sl], k_lv[l][:, sl])
        s_h = s_scr[h]
        o_h = _dot(sc.astype(BF16), vv[:, sl]) + _dot(q_state[:, sl], s_h.astype(BF16))
        dec = jnp.exp(_dot_tn(lf3[:, sl], ones_t))
        s_scr[h] = s_h * dec + _dot_tn(k_state[:, sl], vv[:, sl])
        ms = jnp.mean(o_h * o_h, axis=-1, keepdims=True)
        o_h = o_h * lax.rsqrt(ms + EPS) * nw[:, sl] * gate[:, sl]
        o_ref[:, sl] = o_h.astype(o_ref.dtype)


def hgrn_scan(q, f, i_in, g, lb, norm_w, s0, *, b, t, c, t_valid, n_seq=1):
    n_chunks = t // c
    assert t % c == 0 and (t_valid == c or n_chunks == 1) and q.shape[0] >= b * t
    assert b % n_seq == 0 and (n_seq == 1 or n_chunks == 1)
    dg3, mk = _hgrn_consts(c)
    dg3 = jnp.asarray(dg3, BF16)
    mk = jnp.asarray(mk, F32)
    tok = pl.BlockSpec((n_seq * c, HG_W), lambda bi, ci: (bi * n_chunks + ci, 0))
    st = pl.BlockSpec((n_seq, HG_HEADS, HG_D, HG_D), lambda bi, ci: (bi, 0, 0, 0))
    row = pl.BlockSpec((1, HG_W), lambda bi, ci: (0, 0))
    kern = functools.partial(_hgrn_kernel, c=c, t_valid=t_valid, n_chunks=n_chunks, n_seq=n_seq)
    return pl.pallas_call(
        kern,
        grid=(b // n_seq, n_chunks),
        in_specs=[tok, tok, tok, tok, row, row,
                  pl.BlockSpec(dg3.shape, lambda bi, ci: (0, 0)),
                  pl.BlockSpec(mk.shape, lambda bi, ci: (0, 0, 0)),
                  st],
        out_specs=[tok, st],
        out_shape=[jax.ShapeDtypeStruct((b * t, HG_W), BF16),
                   jax.ShapeDtypeStruct((b, HG_HEADS, HG_D, HG_D), F32)],
        scratch_shapes=[pltpu.VMEM((n_seq, HG_HEADS, HG_D, HG_D), F32)],
        compiler_params=_cparams("parallel", "arbitrary"),
        name="hgrn_scan",
    )(q, f, i_in, g, lb.reshape(1, HG_W), norm_w.reshape(1, HG_W), dg3, mk, s0)


CONV_HEAD = 8


@functools.lru_cache(maxsize=None)
def _ssd_consts(c):
    t = np.arange(c)
    tri = (t[None, :] <= t[:, None]).astype(np.float32)
    rev = (t[None, :] > t[:, None]).astype(np.float32)
    ones = np.ones((SUBLANES, c), np.float32)
    tg = np.concatenate([tri, rev, ones], axis=0)
    tg3 = np.concatenate([tg, tg, tg], axis=1)
    u3 = np.concatenate([tri.T, tri.T, tri.T], axis=0)
    return tg3, u3


def _ssd_kernel(z_ref, xbc_ref, dt_ref, cbuf_ref, cw_ref, cb_ref, dtb_ref, alog_ref, dsk_ref, nw_ref,
                tg_ref, u_ref, h0_ref,
                y_ref, cnew_ref, hfin_ref, xp_scr, h_scr, y_scr, *, c, t_valid, n_chunks, n_seq):
    ci = pl.program_id(1)

    @pl.when(ci == 0)
    def _():
        h_scr[...] = h0_ref[...]
        xp_scr[:, CONV_HEAD - (CONV_W - 1):CONV_HEAD, :] = cbuf_ref[...]

    for si in range(n_seq):
        rows = slice(si * c, (si + 1) * c)
        _ssd_chunk(ci, z_ref.at[rows], xbc_ref.at[rows], dt_ref.at[rows], cw_ref, cb_ref, dtb_ref, alog_ref, dsk_ref,
                   nw_ref, tg_ref, u_ref, y_ref.at[rows], cnew_ref.at[si], xp_scr.at[si], h_scr.at[si], y_scr.at[si],
                   c=c, t_valid=t_valid, n_chunks=n_chunks)

    @pl.when(ci == n_chunks - 1)
    def _():
        hfin_ref[...] = h_scr[...]


def _ssd_chunk(ci, z_ref, xbc_ref, dt_ref, cw_ref, cb_ref, dtb_ref, alog_ref, dsk_ref, nw_ref, tg_ref, u_ref,
               y_ref, cnew_ref, xp_scr, h_scr, y_scr, *, c, t_valid, n_chunks):
    xp_scr[CONV_HEAD:CONV_HEAD + c, :] = xbc_ref[...]
    conv = cb_ref[...]
    for i in range(CONV_W):
        off = CONV_HEAD - (CONV_W - 1) + i
        conv = conv + xp_scr[off:off + c, :] * cw_ref[i:i + 1, :]
    conv = _silu(conv)

    @pl.when(ci == n_chunks - 1)
    def _():
        cnew_ref[...] = xp_scr[CONV_HEAD + t_valid - (CONV_W - 1):CONV_HEAD + t_valid, :]

    xp_scr[CONV_HEAD - (CONV_W - 1):CONV_HEAD, :] = xp_scr[CONV_HEAD + c - (CONV_W - 1):CONV_HEAD + c, :]

    xs = conv[:, :SSM_INNER]
    dt_raw = dt_ref[...] + dtb_ref[...]
    dt = jnp.maximum(dt_raw, 0.0) + jnp.log(1.0 + jnp.exp(-jnp.abs(dt_raw)))
    if t_valid < c:
        row = lax.broadcasted_iota(jnp.int32, (c, LANES), 0)
        dt = jnp.where(row < t_valid, dt, 0.0)
    a = -jnp.exp(alog_ref[...])
    da = dt * a
    da3 = jnp.concatenate(_split3(da), axis=0)
    xx = _dot(tg_ref[...], da3)
    cum = xx[0:c]
    e_cum = jnp.exp(cum)
    w_all = jnp.exp(xx[c:2 * c]) * dt
    e_last = jnp.exp(xx[2 * c:2 * c + 1])
    cum_t = _dot_tn(da3, u_ref[...])

    tril = lax.broadcasted_iota(jnp.int32, (c, c), 0) >= lax.broadcasted_iota(jnp.int32, (c, c), 1)
    heads_per_group = SSM_HEADS // SSM_GROUPS
    for g in range(SSM_GROUPS):
        bm = conv[:, SSM_INNER + g * SSM_N:SSM_INNER + (g + 1) * SSM_N]
        cm = conv[:, SSM_INNER + (SSM_GROUPS + g) * SSM_N:SSM_INNER + (SSM_GROUPS + g + 1) * SSM_N]
        bm_b = bm.astype(BF16)
        cb = _dot_nt(cm.astype(BF16), bm_b)
        for hh in range(heads_per_group):
            h = g * heads_per_group + hh
            x_h = xs[:, h * SSM_P:(h + 1) * SSM_P]
            diff = cum[:, h:h + 1] - cum_t[h:h + 1, :]
            seg = jnp.exp(jnp.where(tril, diff, NEG_BIG))
            xd = (dt[:, h:h + 1] * x_h).astype(BF16)
            cme = (cm * e_cum[:, h:h + 1]).astype(BF16)
            h_h = h_scr[h]
            y_h = _dot((cb * seg).astype(BF16), xd) + _dot_nt(cme, h_h.astype(BF16))
            xw = (w_all[:, h:h + 1] * x_h).astype(BF16)
            h_scr[h] = h_h * e_last[:, h:h + 1] + _dot_tn(xw, bm_b)
            y_scr[:, h * SSM_P:(h + 1) * SSM_P] = y_h

    y = (y_scr[...] + dsk_ref[...] * xs) * _silu(z_ref[...])
    gw = SSM_INNER // SSM_GROUPS
    for g in range(SSM_GROUPS):
        yg = y[:, g * gw:(g + 1) * gw]
        ms = jnp.mean(yg * yg, axis=-1, keepdims=True)
        y_ref[:, g * gw:(g + 1) * gw] = (yg * lax.rsqrt(ms + EPS) * nw_ref[:, g * gw:(g + 1) * gw]).astype(y_ref.dtype)


def ssd_scan(z, xbc, dt, conv_buf, h0, conv_w, conv_b, dt_bias, a_log, d_skip, norm_w, *, b, t, c, t_valid,
             n_seq=1):
    n_chunks = t // c
    assert t % c == 0 and (t_valid == c or n_chunks == 1) and z.shape[0] >= b * t
    assert b % n_seq == 0 and (n_seq == 1 or n_chunks == 1)
    tg3, u3 = _ssd_consts(c)
    tg3 = jnp.asarray(tg3, BF16)
    u3 = jnp.asarray(u3, BF16)

    def pad_heads(v):
        return jnp.pad(v.astype(F32), (0, LANES - SSM_HEADS)).reshape(1, LANES)

    def tok(w):
        return pl.BlockSpec((n_seq * c, w), lambda bi, ci: (bi * n_chunks + ci, 0))

    def const(shape):
        return pl.BlockSpec(shape, lambda bi, ci: (0,) * len(shape))

    cst = pl.BlockSpec((n_seq, CONV_W - 1, CONV_DIM), lambda bi, ci: (bi, 0, 0))
    hst = pl.BlockSpec((n_seq, SSM_HEADS, SSM_P, SSM_N), lambda bi, ci: (bi, 0, 0, 0))
    kern = functools.partial(_ssd_kernel, c=c, t_valid=t_valid, n_chunks=n_chunks, n_seq=n_seq)
    return pl.pallas_call(
        kern,
        grid=(b // n_seq, n_chunks),
        in_specs=[tok(SSM_INNER), tok(CONV_DIM), tok(LANES), cst,
                  const((CONV_W, CONV_DIM)), const((1, CONV_DIM)), const((1, LANES)), const((1, LANES)),
                  const((1, SSM_INNER)), const((1, SSM_INNER)), const(tg3.shape), const(u3.shape), hst],
        out_specs=[tok(SSM_INNER), cst, hst],
        out_shape=[jax.ShapeDtypeStruct((b * t, SSM_INNER), BF16),
                   jax.ShapeDtypeStruct((b, CONV_W - 1, CONV_DIM), F32),
                   jax.ShapeDtypeStruct((b, SSM_HEADS, SSM_P, SSM_N), F32)],
        scratch_shapes=[pltpu.VMEM((n_seq, CONV_HEAD + c, CONV_DIM), F32),
                        pltpu.VMEM((n_seq, SSM_HEADS, SSM_P, SSM_N), F32),
                        pltpu.VMEM((n_seq, c, SSM_INNER), F32)],
        compiler_params=_cparams("parallel", "arbitrary"),
        name="ssd_scan",
    )(z, xbc, dt, conv_buf, conv_w, conv_b.reshape(1, CONV_DIM), pad_heads(dt_bias), pad_heads(a_log),
      jnp.repeat(d_skip.astype(F32), SSM_P).reshape(1, SSM_INNER), norm_w.reshape(1, SSM_INNER), tg3, u3, h0)


N_GROUPS = 4
EXPERTS_PER_GROUP = 8
N_EXPERTS = N_GROUPS * EXPERTS_PER_GROUP
TOP_K = 2
D_EXPERT = 256
MOE_TM = 256


def _router_kernel(x_ref, g_ref, w_ref, b_ref, tri_ref, xn_ref, rw_ref, ri_ref, cnt_ref, carry_scr):
    @pl.when(pl.program_id(0) == 0)
    def _():
        carry_scr[...] = jnp.zeros_like(carry_scr)

    x = x_ref[...]
    ms = jnp.mean(x * x, axis=-1, keepdims=True)
    xn = x * lax.rsqrt(ms + EPS) * g_ref[...]
    for s in range(SUBLANES):
        xn_ref[pl.ds(s, x.shape[0], stride=SUBLANES), :] = xn[:, s * LANES:(s + 1) * LANES]
    logits = jnp.dot(xn, w_ref[...], precision=lax.Precision.HIGHEST, preferred_element_type=F32) + b_ref[...]
    lane = lax.broadcasted_iota(jnp.int32, logits.shape, 1)
    is_g = (lane >= N_EXPERTS) & (lane < N_EXPERTS + N_GROUPS)
    gl = jnp.where(is_g, logits, NEG_BIG)
    gmax = jnp.max(gl, axis=-1, keepdims=True)
    g_sel = jnp.min(jnp.where(gl == gmax, lane, 4 * LANES), axis=-1, keepdims=True) - N_EXPERTS
    g_w = 1.0 / jnp.sum(jnp.where(is_g, jnp.exp(gl - gmax), 0.0), axis=-1, keepdims=True)
    lo = g_sel * EXPERTS_PER_GROUP
    in_grp = (lane >= lo) & (lane < lo + EXPERTS_PER_GROUP)
    el = jnp.where(in_grp, logits, NEG_BIG)
    emax = jnp.max(el, axis=-1, keepdims=True)
    ee = jnp.where(in_grp, jnp.exp(el - emax), 0.0)
    p = ee / jnp.sum(ee, axis=-1, keepdims=True)
    p = jnp.where(in_grp, p, -1.0)
    p1 = jnp.max(p, axis=-1, keepdims=True)
    i1 = jnp.min(jnp.where(p == p1, lane, 4 * LANES), axis=-1, keepdims=True)
    p_rest = jnp.where(lane == i1, -1.0, p)
    p2 = jnp.max(p_rest, axis=-1, keepdims=True)
    i2 = jnp.min(jnp.where(p_rest == p2, lane, 4 * LANES), axis=-1, keepdims=True)
    w1 = p1 / (p1 + p2) * g_w
    w2 = p2 / (p1 + p2) * g_w
    rw_ref[...] = jnp.where(lane == 0, w1, jnp.where(lane == 1, w2, 0.0))
    hit1 = lane == i1
    hit2 = lane == i2
    onehot = (hit1 | hit2).astype(BF16)
    before = _dot(tri_ref[...], onehot) + carry_scr[0:1, :]
    r1 = jnp.sum(jnp.where(hit1, before, 0.0), axis=-1, keepdims=True).astype(jnp.int32)
    r2 = jnp.sum(jnp.where(hit2, before, 0.0), axis=-1, keepdims=True).astype(jnp.int32)
    ri_ref[...] = jnp.where(lane == 0, i1, jnp.where(lane == 1, i2, jnp.where(lane == 2, r1, jnp.where(lane == 3, r2, 0))))
    carry_scr[...] = carry_scr[...] + jnp.sum(onehot.astype(F32), axis=0, keepdims=True)
    cnt_ref[...] = carry_scr[...]


def moe_router(x, gain, w_rg, b_rg, w_re, b_re, *, tm):
    n, d = x.shape
    assert d == SUBLANES * LANES
    w = jnp.zeros((d, LANES), F32).at[:, :N_EXPERTS].set(w_re).at[:, N_EXPERTS:N_EXPERTS + N_GROUPS].set(w_rg)
    b = jnp.zeros((1, LANES), F32).at[0, :N_EXPERTS].set(b_re).at[0, N_EXPERTS:N_EXPERTS + N_GROUPS].set(b_rg)
    t = np.arange(tm)
    tri = jnp.asarray(t[None, :] < t[:, None], BF16)
    return pl.pallas_call(
        _router_kernel,
        grid=(n // tm,),
        in_specs=[pl.BlockSpec((tm, d), lambda i: (i, 0)),
                  pl.BlockSpec((1, d), lambda i: (0, 0)),
                  pl.BlockSpec((d, LANES), lambda i: (0, 0)),
                  pl.BlockSpec((1, LANES), lambda i: (0, 0)),
                  pl.BlockSpec((tm, tm), lambda i: (0, 0))],
        out_specs=[pl.BlockSpec((tm * SUBLANES, LANES), lambda i: (i, 0)),
                   pl.BlockSpec((tm, LANES), lambda i: (i, 0)),
                   pl.BlockSpec((tm, LANES), lambda i: (i, 0)),
                   pl.BlockSpec((SUBLANES, LANES), lambda i: (0, 0))],
        out_shape=[jax.ShapeDtypeStruct((n * SUBLANES, LANES), F32),
                   jax.ShapeDtypeStruct((n, LANES), F32),
                   jax.ShapeDtypeStruct((n, LANES), jnp.int32),
                   jax.ShapeDtypeStruct((SUBLANES, LANES), F32)],
        scratch_shapes=[pltpu.VMEM((SUBLANES, LANES), F32)],
        compiler_params=_cparams("arbitrary"),
        name="moe_router",
    )(x, gain.reshape(1, d), w, b, tri)


def _token_rows(idx):
    return pl.ds(pl.multiple_of(idx * SUBLANES, SUBLANES), SUBLANES)


def _moe_dispatch_kernel(dest_ref, xn_ref, zero_ref, xs_ref, sem, *, tm):
    del zero_ref
    base = pl.program_id(0) * (tm * TOP_K)

    def start(r, carry):
        for k in range(TOP_K):
            pltpu.make_async_copy(xn_ref.at[_token_rows(r)], xs_ref.at[_token_rows(dest_ref[base + r * TOP_K + k])],
                                  sem).start()
        return carry

    lax.fori_loop(0, tm, start, 0, unroll=8)

    def wait(r, carry):
        for k in range(TOP_K):
            pltpu.make_async_copy(xn_ref.at[_token_rows(0)], xs_ref.at[_token_rows(0)], sem).wait()
        return carry

    lax.fori_loop(0, tm, wait, 0, unroll=8)


def moe_dispatch(dest, xn, n_rows, *, tm):
    n = xn.shape[0] // SUBLANES
    grid_spec = pltpu.PrefetchScalarGridSpec(
        num_scalar_prefetch=1,
        grid=(n // tm,),
        in_specs=[pl.BlockSpec((tm * SUBLANES, LANES), lambda i, d: (i, 0)),
                  pl.BlockSpec(memory_space=pl.ANY)],
        out_specs=pl.BlockSpec(memory_space=pl.ANY),
        scratch_shapes=[pltpu.SemaphoreType.DMA(())],
    )
    return pl.pallas_call(
        functools.partial(_moe_dispatch_kernel, tm=tm),
        grid_spec=grid_spec,
        out_shape=jax.ShapeDtypeStruct((n_rows * SUBLANES, LANES), F32),
        input_output_aliases={2: 0},
        compiler_params=_cparams("arbitrary"),
        name="moe_dispatch",
    )(dest, xn, jnp.zeros((n_rows * SUBLANES, LANES), F32))


def _moe_ffn_kernel(te_ref, nu_ref, x_ref, wg_ref, wu_ref, wd_ref, y_ref):
    i = pl.program_id(0)

    @pl.when(i < nu_ref[0])
    def _():
        x = jnp.concatenate([x_ref[pl.ds(s, MOE_TM, stride=SUBLANES), :] for s in range(SUBLANES)], axis=1).astype(BF16)
        a = _dot(x, wg_ref[...].astype(BF16))
        b = _dot(x, wu_ref[...].astype(BF16))
        hid = (_silu(a) * b).astype(BF16)
        y = _dot(hid, wd_ref[...].astype(BF16))
        for s in range(SUBLANES):
            y_ref[pl.ds(s, MOE_TM, stride=SUBLANES), :] = y[:, s * LANES:(s + 1) * LANES]

    @pl.when(i >= nu_ref[0])
    def _():
        y_ref[...] = jnp.zeros_like(y_ref)


def moe_ffn(tile_expert, n_used, x_sorted, w_gate, w_up, w_down, layer):
    n_tiles = x_sorted.shape[0] // (MOE_TM * SUBLANES)
    d = w_gate.shape[2]
    tile = pl.BlockSpec((MOE_TM * SUBLANES, LANES), lambda i, te, nu: (i, 0))
    grid_spec = pltpu.PrefetchScalarGridSpec(
        num_scalar_prefetch=2,
        grid=(n_tiles,),
        in_specs=[tile,
                  pl.BlockSpec((None, None, d, D_EXPERT), lambda i, te, nu: (layer, te[i], 0, 0)),
                  pl.BlockSpec((None, None, d, D_EXPERT), lambda i, te, nu: (layer, te[i], 0, 0)),
                  pl.BlockSpec((None, None, D_EXPERT, d), lambda i, te, nu: (layer, te[i], 0, 0))],
        out_specs=tile,
    )
    return pl.pallas_call(
        _moe_ffn_kernel,
        grid_spec=grid_spec,
        out_shape=jax.ShapeDtypeStruct(x_sorted.shape, F32),
        compiler_params=_cparams("arbitrary"),
        name="moe_ffn",
    )(tile_expert, n_used, x_sorted, w_gate, w_up, w_down)


def _moe_combine_kernel(dest_ref, x_ref, rw_ref, fg_ref, ys_ref, o_ref, ybuf, sem, *, tm, n_steps, final_norm):
    i = pl.program_id(0)
    slot = i % 2
    slot_rows = tm * TOP_K * SUBLANES

    def issue(step, to_slot):
        base = step * (tm * TOP_K)

        def body(r, carry):
            for k in range(TOP_K):
                dst = pl.ds(pl.multiple_of(to_slot * slot_rows + (r * TOP_K + k) * SUBLANES, SUBLANES), SUBLANES)
                pltpu.make_async_copy(ys_ref.at[_token_rows(dest_ref[base + r * TOP_K + k])], ybuf.at[dst],
                                      sem.at[to_slot]).start()
            return carry

        lax.fori_loop(0, tm, body, 0, unroll=8)

    @pl.when(i == 0)
    def _():
        issue(0, 0)

    @pl.when(i + 1 < n_steps)
    def _():
        issue(i + 1, 1 - slot)

    def wait(r, carry):
        for k in range(TOP_K):
            pltpu.make_async_copy(ys_ref.at[_token_rows(0)], ybuf.at[_token_rows(0)], sem.at[slot]).wait()
        return carry

    lax.fori_loop(0, tm, wait, 0, unroll=8)

    rw = rw_ref[...]
    g0 = rw[:, 0:1]
    g1 = rw[:, 1:2]
    first = pl.multiple_of(slot * slot_rows, SUBLANES)
    pieces = []
    for s in range(SUBLANES):
        y0 = ybuf[pl.ds(first + s, tm, stride=TOP_K * SUBLANES), :]
        y1 = ybuf[pl.ds(first + SUBLANES + s, tm, stride=TOP_K * SUBLANES), :]
        pieces.append(x_ref[:, s * LANES:(s + 1) * LANES] + (g0 * y0 + g1 * y1))
    if final_norm:
        sq = sum(jnp.sum(p * p, axis=-1, keepdims=True) for p in pieces)
        scale = lax.rsqrt(sq * (1.0 / (SUBLANES * LANES)) + EPS)
        pieces = [p * scale * fg_ref[:, s * LANES:(s + 1) * LANES] for s, p in enumerate(pieces)]
    for s, p in enumerate(pieces):
        o_ref[:, s * LANES:(s + 1) * LANES] = p


def moe_combine(dest, x, rw, y_sorted, final_gain, *, tm, final_norm):
    n, d = x.shape
    n_steps = n // tm
    grid_spec = pltpu.PrefetchScalarGridSpec(
        num_scalar_prefetch=1,
        grid=(n_steps,),
        in_specs=[pl.BlockSpec((tm, d), lambda i, dd: (i, 0)),
                  pl.BlockSpec((tm, LANES), lambda i, dd: (i, 0)),
                  pl.BlockSpec((1, d), lambda i, dd: (0, 0)),
                  pl.BlockSpec(memory_space=pl.ANY)],
        out_specs=pl.BlockSpec((tm, d), lambda i, dd: (i, 0)),
        scratch_shapes=[pltpu.VMEM((2 * tm * TOP_K * SUBLANES, LANES), F32),
                        pltpu.SemaphoreType.DMA((2,))],
    )
    return pl.pallas_call(
        functools.partial(_moe_combine_kernel, tm=tm, n_steps=n_steps, final_norm=final_norm),
        grid_spec=grid_spec,
        out_shape=jax.ShapeDtypeStruct((n, d), F32),
        compiler_params=_cparams("arbitrary"),
        name="moe_combine",
    )(dest, x, rw, final_gain.reshape(1, d), y_sorted)


def _moe_plan(ri, counts, n):
    eid = ri[:, :TOP_K]
    rank = ri[:, TOP_K:2 * TOP_K]
    counts = counts[0, :N_EXPERTS].astype(jnp.int32)
    tiles = (counts + MOE_TM - 1) // MOE_TM
    tile_end = jnp.cumsum(tiles)
    tile_start = tile_end - tiles
    onehot = eid[:, :, None] == jnp.arange(N_EXPERTS, dtype=jnp.int32)[None, None, :]
    dest = jnp.sum(jnp.where(onehot, tile_start[None, None, :], 0), axis=-1) * MOE_TM + rank
    n_rows = TOP_K * n + N_EXPERTS * MOE_TM
    n_tiles = n_rows // MOE_TM
    n_used = tile_end[-1]
    t_idx = jnp.minimum(jnp.arange(n_tiles, dtype=jnp.int32), n_used - 1)
    tile_expert = jnp.sum((t_idx[:, None] >= tile_end[None, :]).astype(jnp.int32), axis=1)
    return dest.reshape(-1).astype(jnp.int32), tile_expert.astype(jnp.int32), n_used.reshape(1).astype(jnp.int32), n_rows


def hier_moe_block(x, gain, w_rg, b_rg, w_re, b_re, w_gate, w_up, w_down, layer, final_gain, final_norm):
    n, d = x.shape
    xn, rw, ri, counts = moe_router(x, gain, w_rg, b_rg, w_re, b_re, tm=512)
    dest, tile_expert, n_used, n_rows = _moe_plan(ri, counts, n)
    x_sorted = moe_dispatch(dest, xn, n_rows, tm=512)
    y_sorted = moe_ffn(tile_expert, n_used, x_sorted, w_gate, w_up, w_down, layer)
    return moe_combine(dest, x, rw, y_sorted, final_gain, tm=256, final_norm=final_norm)


MLA_HEADS = 16
MLA_LORA = 256
MLA_NOPE = 64
MLA_ROPE = 32
MLA_V = 64
MLA_SCALE = (MLA_NOPE + MLA_ROPE) ** -0.5
Q_SCALE = MLA_SCALE * math.log2(math.e)
ROPE_THETA = 10000.0
HEAD_PAD = 128
ROPE_AT = MLA_NOPE
Q_DEC = 384
V_ROWS = 80


def _rope_tables(pos):
    half = MLA_ROPE // 2
    inv = ROPE_THETA ** (-jnp.arange(half, dtype=F32) / half)
    ang = pos.astype(F32)[:, None] * inv[None, :]
    cos, sin = jnp.cos(ang), jnp.sin(ang)
    n = pos.shape[0]
    ones = jnp.ones((n, MLA_NOPE), F32)
    zeros_n = jnp.zeros((n, MLA_NOPE), F32)
    zeros_p = jnp.zeros((n, HEAD_PAD - MLA_NOPE - MLA_ROPE), F32)
    ctab = jnp.concatenate([ones, cos, cos, zeros_p], axis=1)
    stab = jnp.concatenate([zeros_n, sin, sin, zeros_p], axis=1)
    return ctab, stab


def _mla_weights(w_in, w_uq, w_uk, w_uv):
    d = w_in.shape[0]
    half = MLA_ROPE // 2
    w_kr = w_in[:, 2 * MLA_LORA:]
    zl = jnp.zeros((d, ROPE_AT), F32)
    zr = jnp.zeros((d, HEAD_PAD - ROPE_AT - MLA_ROPE), F32)
    kr_a = jnp.concatenate([zl, w_kr, zr], axis=1)
    kr_b = jnp.concatenate([zl, -w_kr[:, half:], w_kr[:, :half], zr], axis=1)
    w_in_p = jnp.concatenate([w_in[:, :2 * MLA_LORA], kr_a, kr_b], axis=1).astype(BF16)
    wq = w_uq.reshape(MLA_LORA, MLA_HEADS, MLA_NOPE + MLA_ROPE)
    nope, x1, x2 = wq[..., :MLA_NOPE], wq[..., MLA_NOPE:MLA_NOPE + half], wq[..., MLA_NOPE + half:]
    zp = jnp.zeros((MLA_LORA, MLA_HEADS, HEAD_PAD - MLA_NOPE - MLA_ROPE), F32)
    wq_a = jnp.concatenate([nope, x1, x2, zp], axis=-1).reshape(MLA_LORA, MLA_HEADS * HEAD_PAD).astype(BF16)
    wq_b = jnp.concatenate([jnp.zeros_like(nope), -x2, x1, zp], axis=-1).reshape(MLA_LORA, MLA_HEADS * HEAD_PAD).astype(BF16)
    zk = jnp.zeros((MLA_LORA, MLA_HEADS, HEAD_PAD - MLA_NOPE), F32)
    wuk_p = jnp.concatenate([w_uk, zk], axis=-1).reshape(MLA_LORA, MLA_HEADS * HEAD_PAD).astype(BF16)
    wuv = w_uv.reshape(MLA_LORA, MLA_HEADS * MLA_V).astype(BF16)
    absorb = jnp.transpose(w_uk, (1, 2, 0))
    sel = jnp.zeros((MLA_ROPE, Q_DEC - MLA_LORA), F32).at[jnp.arange(MLA_ROPE), jnp.arange(MLA_ROPE)].set(1.0)
    top = jnp.concatenate([absorb, jnp.zeros((MLA_HEADS, MLA_NOPE, Q_DEC - MLA_LORA), F32)], axis=-1)
    mid = jnp.broadcast_to(jnp.concatenate([jnp.zeros((MLA_ROPE, MLA_LORA), F32), sel], axis=-1)[None],
                           (MLA_HEADS, MLA_ROPE, Q_DEC))
    bot = jnp.zeros((MLA_HEADS, HEAD_PAD - MLA_NOPE - MLA_ROPE, Q_DEC), F32)
    w_dec = jnp.concatenate([top, mid, bot], axis=1).astype(BF16)
    return w_in_p, wq_a, wq_b, wuk_p, wuv, w_dec


def _mla_q_kernel(cq_ref, ckv_ref, kra_ref, krb_ref, ct_ref, st_ref, ctt_ref, stt_ref, qn_ref, kvn_ref,
                  wa_ref, wb_ref, qt_ref, ckvn_ref, krot_ref):
    cq = cq_ref[...]
    ms = jnp.mean(cq * cq, axis=-1, keepdims=True)
    cqn = (cq * lax.rsqrt(ms + EPS) * qn_ref[...]).astype(BF16)
    ckv = ckv_ref[...]
    ms2 = jnp.mean(ckv * ckv, axis=-1, keepdims=True)
    ckvn_ref[...] = ckv * lax.rsqrt(ms2 + EPS) * kvn_ref[...]
    krot_ref[...] = kra_ref[...] * ct_ref[...] + krb_ref[...] * st_ref[...]
    ctt = ctt_ref[...]
    stt = stt_ref[...]
    for h in range(MLA_HEADS):
        sl = slice(h * HEAD_PAD, (h + 1) * HEAD_PAD)
        qh = _dot_nt(wa_ref[sl, :], cqn) * ctt + _dot_nt(wb_ref[sl, :], cqn) * stt
        qt_ref[sl, :] = (qh * Q_SCALE).astype(qt_ref.dtype)


def mla_q(cq, ckv, kr_a, kr_b, ctab, stab, q_norm, kv_norm, wq_at, wq_bt, *, tm):
    n = cq.shape[0]
    hw = MLA_HEADS * HEAD_PAD

    def tok(w):
        return pl.BlockSpec((tm, w), lambda i: (i, 0))

    def tok_t(w):
        return pl.BlockSpec((w, tm), lambda i: (0, i))

    def const(shape):
        return pl.BlockSpec(shape, lambda i: (0, 0))

    return pl.pallas_call(
        _mla_q_kernel,
        grid=(n // tm,),
        in_specs=[tok(MLA_LORA), tok(MLA_LORA), tok(HEAD_PAD), tok(HEAD_PAD), tok(HEAD_PAD), tok(HEAD_PAD),
                  tok_t(HEAD_PAD), tok_t(HEAD_PAD),
                  const((1, MLA_LORA)), const((1, MLA_LORA)), const(wq_at.shape), const(wq_bt.shape)],
        out_specs=[tok_t(hw), tok(MLA_LORA), tok(HEAD_PAD)],
        out_shape=[jax.ShapeDtypeStruct((hw, n), BF16),
                   jax.ShapeDtypeStruct((n, MLA_LORA), F32),
                   jax.ShapeDtypeStruct((n, HEAD_PAD), F32)],
        compiler_params=_cparams("parallel"),
        name="mla_q",
    )(cq, ckv, kr_a, kr_b, ctab, stab, ctab.T, stab.T, q_norm.reshape(1, MLA_LORA), kv_norm.reshape(1, MLA_LORA),
      wq_at, wq_bt)


def _mla_kv_kernel(ckvn_ref, krot_ref, wk_ref, wvt_ref, k_ref, vt_ref):
    c = ckvn_ref[...].astype(BF16)
    krot = krot_ref[...]
    for h in range(MLA_HEADS):
        sl = slice(h * HEAD_PAD, (h + 1) * HEAD_PAD)
        k_ref[:, sl] = (_dot(c, wk_ref[:, sl]) + krot).astype(k_ref.dtype)
    vt = _dot_nt(wvt_ref[...], c)
    row = lax.broadcasted_iota(jnp.int32, vt.shape, 0)
    vt_ref[...] = jnp.where(row % V_ROWS == MLA_V, 1.0, vt).astype(vt_ref.dtype)


def mla_kv(ckvn, krot, wuk_p, wuv_t, *, n, tm):
    hw = MLA_HEADS * HEAD_PAD
    vw = MLA_HEADS * V_ROWS
    wuv_t = jnp.pad(wuv_t.reshape(MLA_HEADS, MLA_V, MLA_LORA), ((0, 0), (0, V_ROWS - MLA_V), (0, 0))).reshape(vw, MLA_LORA)
    return pl.pallas_call(
        _mla_kv_kernel,
        grid=(n // tm,),
        in_specs=[pl.BlockSpec((tm, MLA_LORA), lambda i: (i, 0)),
                  pl.BlockSpec((tm, HEAD_PAD), lambda i: (i, 0)),
                  pl.BlockSpec(wuk_p.shape, lambda i: (0, 0)),
                  pl.BlockSpec(wuv_t.shape, lambda i: (0, 0))],
        out_specs=[pl.BlockSpec((tm, hw), lambda i: (i, 0)),
                   pl.BlockSpec((None, vw, tm), lambda i: (i, 0, 0))],
        out_shape=[jax.ShapeDtypeStruct((n, hw), BF16),
                   jax.ShapeDtypeStruct((n // tm, vw, tm), BF16)],
        compiler_params=_cparams("parallel"),
        name="mla_kv",
    )(ckvn, krot, wuk_p, wuv_t)


def _flash_kernel(qt_ref, k_ref, vt_ref, ot_ref, m_scr, acc_scr, p_scr, alpha_scr, *, tq, tkb):
    qi = pl.program_id(2)
    sub = tq // tkb
    m_scr[...] = jnp.full_like(m_scr, NEG_BIG)
    acc_scr[...] = jnp.zeros_like(acc_scr)
    p_scr[...] = jnp.zeros_like(p_scr)
    alpha_scr[...] = jnp.ones_like(alpha_scr)

    def retire(kb_prev):
        for hh in range(2):
            pv = _dot(vt_ref[kb_prev, hh * V_ROWS:(hh + 1) * V_ROWS, :], p_scr[hh])
            acc_scr[hh] = alpha_scr[hh] * acc_scr[hh] + pv

    strips = [(hh, st) for hh in range(2) for st in range(tq // FLASH_STRIP)]

    def score(kb, hh, st):
        rows = pl.ds(pl.multiple_of(kb * tkb, tkb), tkb)
        return _dot(k_ref[rows, hh * HEAD_PAD:(hh + 1) * HEAD_PAD],
                    qt_ref[hh * HEAD_PAD:(hh + 1) * HEAD_PAD, st * FLASH_STRIP:(st + 1) * FLASH_STRIP])

    def key_block(kb, diag_off):
        scores = [score(kb, hh, st) for hh, st in strips]
        retire(jnp.maximum(kb - 1, 0))
        for (hh, st), s in zip(strips, scores):
            cols = slice(st * FLASH_STRIP, (st + 1) * FLASH_STRIP)
            if diag_off is not None:
                key = lax.broadcasted_iota(jnp.int32, s.shape, 0) + diag_off
                qry = lax.broadcasted_iota(jnp.int32, s.shape, 1) + st * FLASH_STRIP
                s = jnp.where(key <= qry, s, NEG_BIG)
            m_old = m_scr[hh, :, cols]
            m_new = jnp.maximum(m_old, jnp.max(s, axis=0, keepdims=True))
            alpha = jnp.exp2(m_old - m_new)
            m_scr[hh, :, cols] = m_new
            p_scr[hh, :, cols] = jnp.exp2(s - m_new).astype(BF16)
            alpha_scr[hh, :, cols] = alpha

    def full_tile(j, carry):
        for d in range(sub):
            key_block(j * sub + d, None)
        return carry

    lax.fori_loop(0, qi, full_tile, 0)
    for d in range(sub):
        key_block(qi * sub + d, d * tkb)
    retire(qi * sub + sub - 1)
    for hh in range(2):
        acc = acc_scr[hh]
        ot_ref[hh * MLA_V:(hh + 1) * MLA_V, :] = (acc[:MLA_V] / acc[MLA_V:MLA_V + 1]).astype(ot_ref.dtype)


def flash_attention(qt, k, vt, *, b, t, tq):
    tkb = vt.shape[2]
    nq = t // tq
    nkb = t // tkb
    assert tq % tkb == 0 and t % tq == 0
    n_hp = MLA_HEADS // 2
    return pl.pallas_call(
        functools.partial(_flash_kernel, tq=tq, tkb=tkb),
        grid=(b, n_hp, nq),
        in_specs=[pl.BlockSpec((2 * HEAD_PAD, tq), lambda bi, hp, qi: (hp, bi * nq + qi)),
                  pl.BlockSpec((t, 2 * HEAD_PAD), lambda bi, hp, qi: (bi, hp)),
                  pl.BlockSpec((nkb, 2 * V_ROWS, tkb), lambda bi, hp, qi: (bi, hp, 0))],
        out_specs=pl.BlockSpec((2 * MLA_V, tq), lambda bi, hp, qi: (hp, bi * nq + qi)),
        out_shape=jax.ShapeDtypeStruct((MLA_HEADS * MLA_V, b * t), BF16),
        scratch_shapes=[pltpu.VMEM((2, 1, tq), F32),
                        pltpu.VMEM((2, V_ROWS, tq), F32), pltpu.VMEM((2, tkb, tq), BF16),
                        pltpu.VMEM((2, 1, tq), F32)],
        compiler_params=_cparams("parallel", "parallel", "arbitrary"),
        name="mla_flash",
    )(qt, k, vt)


PAGE = 128
DEC_PB = 32
DEC_SUB = 2048
NEW_PAD = 8


def _q_dec_kernel(qt_ref, w_ref, o_ref):
    for h in range(MLA_HEADS):
        o_ref[:, h * Q_DEC:(h + 1) * Q_DEC] = _dot_tn(qt_ref[h * HEAD_PAD:(h + 1) * HEAD_PAD, :], w_ref[h]).astype(o_ref.dtype)


def mla_q_dec(qt, w_dec):
    n = qt.shape[1]
    return pl.pallas_call(
        _q_dec_kernel,
        grid=(1,),
        in_specs=[pl.BlockSpec(qt.shape, lambda i: (0, 0)), pl.BlockSpec(w_dec.shape, lambda i: (0, 0, 0))],
        out_specs=pl.BlockSpec((n, MLA_HEADS * Q_DEC), lambda i: (0, 0)),
        out_shape=jax.ShapeDtypeStruct((n, MLA_HEADS * Q_DEC), BF16),
        compiler_params=_cparams("arbitrary"),
        name="mla_q_dec",
    )(qt, w_dec)


def _decode_kernel(pt_ref, q_ref, cnew_ref, rnew_ref, cache_c, cache_rt, o_ref, cbuf, rbuf, sem,
                   *, layer, n_pages, t_new, n_seq):
    b = pl.program_id(0)
    n_blk = n_pages // DEC_PB
    rows = q_ref.shape[0]

    def page_copies(pg, slot, i):
        off = pl.ds(pl.multiple_of(i * PAGE, PAGE), PAGE)
        cc = pltpu.make_async_copy(cache_c.at[layer, pg], cbuf.at[slot, off], sem.at[0, slot])
        cr = pltpu.make_async_copy(cache_rt.at[layer, pg], rbuf.at[slot, :, off], sem.at[1, slot])
        return cc, cr

    def start_block(seq, j, slot):
        def body(i, carry):
            cc, cr = page_copies(pt_ref[seq, j * DEC_PB + i], slot, i)
            cc.start()
            cr.start()
            return carry
        lax.fori_loop(0, DEC_PB, body, 0, unroll=8)

    def wait_block(slot):
        def body(i, carry):
            cc, cr = page_copies(0, slot, 0)
            cc.wait()
            cr.wait()
            return carry
        lax.fori_loop(0, DEC_PB, body, 0, unroll=8)

    q = q_ref[...]
    ql = q[:, :MLA_LORA]
    qr = q[:, MLA_LORA:MLA_LORA + MLA_ROPE]

    @pl.when(b == 0)
    def _():
        start_block(0, 0, 0)

    m = jnp.full((rows, 1), NEG_BIG, F32)
    l = jnp.zeros((rows, 1), F32)
    acc = jnp.zeros((rows, MLA_LORA), F32)
    pending = None
    for j in range(n_blk):
        slot = j % 2
        if j + 1 < n_blk:
            start_block(b, j + 1, 1 - slot)
        else:
            @pl.when(b + 1 < n_seq)
            def _():
                start_block(b + 1, 0, 0)
        wait_block(slot)
        for u in range(DEC_PB * PAGE // DEC_SUB):
            keys = slice(u * DEC_SUB, (u + 1) * DEC_SUB)
            kc = cbuf[slot, keys, :].astype(BF16)
            krt = rbuf[slot, :, keys].astype(BF16)
            s = _dot_nt(ql, kc) + _dot(qr, krt)
            if pending is not None:
                p_prev, alpha_prev, kc_prev = pending
                acc = alpha_prev * acc + _dot(p_prev, kc_prev)
            m_new = jnp.maximum(m, jnp.max(s, axis=-1, keepdims=True))
            alpha = jnp.exp2(m - m_new)
            p = jnp.exp2(s - m_new)
            l = alpha * l + jnp.sum(p, axis=-1, keepdims=True)
            m = m_new
            pending = (p.astype(BF16), alpha, kc)

    cn = cnew_ref[...].astype(BF16)
    rn = rnew_ref[...].astype(BF16)
    s = _dot_nt(ql, cn) + _dot_nt(qr, rn)
    p_prev, alpha_prev, kc_prev = pending
    acc = alpha_prev * acc + _dot(p_prev, kc_prev)
    t_row = lax.broadcasted_iota(jnp.int32, s.shape, 0) // MLA_HEADS
    col = lax.broadcasted_iota(jnp.int32, s.shape, 1)
    s = jnp.where((col <= t_row) & (col < t_new), s, NEG_BIG)
    m_new = jnp.maximum(m, jnp.max(s, axis=-1, keepdims=True))
    alpha = jnp.exp2(m - m_new)
    p = jnp.exp2(s - m_new)
    l = alpha * l + jnp.sum(p, axis=-1, keepdims=True)
    acc = alpha * acc + _dot(p.astype(BF16), cn)
    o_ref[...] = (acc / l).astype(o_ref.dtype)


def mla_decode(page_table, q_dec, c_new, r_new, cache_ckv, cache_krope_t, *, layer, t_new):
    b, rows, _ = q_dec.shape
    n_pages = page_table.shape[1]
    assert n_pages % (2 * DEC_PB) == 0
    grid_spec = pltpu.PrefetchScalarGridSpec(
        num_scalar_prefetch=1,
        grid=(b,),
        in_specs=[pl.BlockSpec((None, rows, Q_DEC), lambda i, pt: (i, 0, 0)),
                  pl.BlockSpec((None, NEW_PAD, MLA_LORA), lambda i, pt: (i, 0, 0)),
                  pl.BlockSpec((None, NEW_PAD, MLA_ROPE), lambda i, pt: (i, 0, 0)),
                  pl.BlockSpec(memory_space=pl.ANY),
                  pl.BlockSpec(memory_space=pl.ANY)],
        out_specs=pl.BlockSpec((None, rows, MLA_LORA), lambda i, pt: (i, 0, 0)),
        scratch_shapes=[pltpu.VMEM((2, DEC_PB * PAGE, MLA_LORA), F32),
                        pltpu.VMEM((2, MLA_ROPE, DEC_PB * PAGE), F32),
                        pltpu.SemaphoreType.DMA((2, 2))],
    )
    return pl.pallas_call(
        functools.partial(_decode_kernel, layer=layer, n_pages=n_pages, t_new=t_new, n_seq=b),
        grid_spec=grid_spec,
        out_shape=jax.ShapeDtypeStruct((b, rows, MLA_LORA), BF16),
        compiler_params=_cparams("arbitrary"),
        name="mla_decode",
    )(page_table, q_dec, c_new, r_new, cache_ckv, cache_krope_t)


def _sample_v_kernel(o_ref, wt_ref, yt_ref):
    for h in range(MLA_HEADS):
        yt_ref[h * MLA_V:(h + 1) * MLA_V, :] = _dot_nt(wt_ref[h * MLA_V:(h + 1) * MLA_V, :],
                                                       o_ref[:, h * MLA_LORA:(h + 1) * MLA_LORA]).astype(yt_ref.dtype)


def mla_sample_v(o_lat, wuv_t):
    n = o_lat.shape[0]
    return pl.pallas_call(
        _sample_v_kernel,
        grid=(1,),
        in_specs=[pl.BlockSpec(o_lat.shape, lambda i: (0, 0)), pl.BlockSpec(wuv_t.shape, lambda i: (0, 0))],
        out_specs=pl.BlockSpec((MLA_HEADS * MLA_V, n), lambda i: (0, 0)),
        out_shape=jax.ShapeDtypeStruct((MLA_HEADS * MLA_V, n), BF16),
        compiler_params=_cparams("arbitrary"),
        name="mla_sample_v",
    )(o_lat, wuv_t)


HG_CHUNK = 64
SSD_CHUNK = 128
SAMPLE_PAD = 8
SAMPLE_SEQS = 4
FLASH_QUERIES = 512
FLASH_KEYS = 512
FLASH_STRIP = 256
ROW_TILE = 256


def _ab_layer(x, n_p, bp, tp, bs, ts, norm_w, lb, st_hg, st_ssm, st_conv, w_in, w_out, hg_norm, conv_w, conv_b,
              dt_bias, a_log, d_skip, ssm_norm):
    sizes = [HG_W, HG_W, HG_W, HG_W, SSM_INNER, CONV_DIM, SSM_HEADS]
    offs = np.concatenate([[0], np.cumsum(sizes)])
    w_pad = jnp.pad(w_in, ((0, 0), (0, LANES - SSM_HEADS))).astype(BF16)
    splits = [(int(offs[j]), int(offs[j + 1])) for j in range(6)] + [(int(offs[6]), int(offs[6]) + LANES)]
    q, f, i_in, g, z, xbc, dt = norm_matmul(x, norm_w, w_pad, splits, [F32] * 7, tm=ROW_TILE)

    def grp(a, prompt):
        if prompt:
            return a
        a = a[n_p:].reshape(bs, ts, a.shape[1])
        return jnp.pad(a, ((0, 0), (0, SAMPLE_PAD - ts), (0, 0))).reshape(bs * SAMPLE_PAD, a.shape[2])

    outs = []
    for prompt in (True, False):
        if prompt:
            b_, t_, c_hg, c_ssd, tv_hg, tv_ssd = bp, tp, HG_CHUNK, SSD_CHUNK, HG_CHUNK, SSD_CHUNK
            s_hg = jnp.zeros((bp, HG_HEADS, HG_D, HG_D), F32)
            s_ssm = jnp.zeros((bp, SSM_HEADS, SSM_P, SSM_N), F32)
            s_conv = jnp.zeros((bp, CONV_W - 1, CONV_DIM), F32)
        else:
            b_, t_, c_hg, c_ssd, tv_hg, tv_ssd = bs, SAMPLE_PAD, SAMPLE_PAD, SAMPLE_PAD, ts, ts
            s_hg, s_ssm, s_conv = st_hg, st_ssm, st_conv
        n_seq = 1 if prompt else math.gcd(bs, SAMPLE_SEQS)
        o_hg, hg_new = hgrn_scan(grp(q, prompt), grp(f, prompt), grp(i_in, prompt), grp(g, prompt), lb, hg_norm,
                                 s_hg, b=b_, t=t_, c=c_hg, t_valid=tv_hg, n_seq=n_seq)
        y, conv_new, ssm_new = ssd_scan(grp(z, prompt), grp(xbc, prompt), grp(dt, prompt), s_conv, s_ssm,
                                        conv_w, conv_b, dt_bias, a_log, d_skip, ssm_norm,
                                        b=b_, t=t_, c=c_ssd, t_valid=tv_ssd, n_seq=n_seq)
        if not prompt:
            o_hg = o_hg.reshape(bs, SAMPLE_PAD, HG_W)[:, :ts].reshape(bs * ts, HG_W)
            y = y.reshape(bs, SAMPLE_PAD, SSM_INNER)[:, :ts].reshape(bs * ts, SSM_INNER)
        outs.append((o_hg, y, hg_new, ssm_new, conv_new))
    o_hg = jnp.concatenate([outs[0][0], outs[1][0]], axis=0)
    y = jnp.concatenate([outs[0][1], outs[1][1]], axis=0)
    w_out_b = w_out.astype(BF16)
    x = matmul_residual(x, [(o_hg, w_out_b[:HG_W], False), (y, w_out_b[HG_W:], False)], tm=ROW_TILE)
    return x, outs[0][2:], outs[1][2:]


def _mla_layer(x, n_p, bp, tp, bs, ts, past_len, norm_w, cache_ckv, cache_krope, page_table, layer_c,
               w_in, q_norm, kv_norm, w_uq, w_uk, w_uv, w_out):
    w_in_p, wq_a, wq_b, wuk_p, wuv, w_dec = _mla_weights(w_in, w_uq, w_uk, w_uv)
    wuv_t = wuv.T
    splits = [(0, MLA_LORA), (MLA_LORA, 2 * MLA_LORA), (2 * MLA_LORA, 2 * MLA_LORA + HEAD_PAD),
              (2 * MLA_LORA + HEAD_PAD, 2 * MLA_LORA + 2 * HEAD_PAD)]
    cq, ckv, kr_a, kr_b = norm_matmul(x, norm_w, w_in_p, splits, [F32] * 4, tm=ROW_TILE)
    pos = jnp.concatenate([jnp.tile(jnp.arange(tp, dtype=jnp.int32), bp),
                           jnp.tile(past_len + jnp.arange(ts, dtype=jnp.int32), bs)])
    ctab, stab = _rope_tables(pos)
    qt, ckvn, krot = mla_q(cq, ckv, kr_a, kr_b, ctab, stab, q_norm, kv_norm, wq_a.T, wq_b.T, tm=ROW_TILE)
    krope = krot[:, ROPE_AT:ROPE_AT + MLA_ROPE]
    k_p, vt_p = mla_kv(ckvn, krot, wuk_p, wuv_t, n=n_p, tm=math.gcd(tp, FLASH_KEYS))
    ot_p = flash_attention(qt, k_p, vt_p, b=bp, t=tp, tq=math.gcd(tp, FLASH_QUERIES))
    q_dec = mla_q_dec(qt[:, n_p:], w_dec).reshape(bs, ts * MLA_HEADS, Q_DEC)
    c_new = jnp.pad(ckvn[n_p:].reshape(bs, ts, MLA_LORA), ((0, 0), (0, NEW_PAD - ts), (0, 0)))
    r_new = jnp.pad(krope[n_p:].reshape(bs, ts, MLA_ROPE), ((0, 0), (0, NEW_PAD - ts), (0, 0)))
    cache_krope_t = jnp.swapaxes(cache_krope, 2, 3)
    o_lat = mla_decode(page_table, q_dec, c_new, r_new, cache_ckv, cache_krope_t, layer=layer_c, t_new=ts)
    ot_s = mla_sample_v(o_lat.reshape(bs * ts, MLA_HEADS * MLA_LORA), wuv_t)
    ot = jnp.concatenate([ot_p, ot_s], axis=1)
    x = matmul_residual(x, [(ot, w_out.astype(BF16), True)], tm=ROW_TILE)
    return x, (ckvn[:n_p].reshape(bp, tp, MLA_LORA), krope[:n_p].reshape(bp, tp, MLA_ROPE)), \
        (ckvn[n_p:].reshape(bs, ts, MLA_LORA), krope[n_p:].reshape(bs, ts, MLA_ROPE))


def kernel(x_prompt, x_sample, state_hgrn, state_ssm, state_conv, cache_ckv, cache_krope, page_table,
           norm_mix, norm_ffn, norm_final, w_in_ab, w_out_ab, hgrn_lb, hgrn_norm, conv_w, conv_b,
           dt_bias, a_log, d_skip, ssm_norm, w_in_c, q_norm, kv_norm, w_uq, w_uk, w_uv, w_out_c,
           w_route_group, b_route_group, w_route_expert, b_route_expert, w_gate, w_up, w_down):
    bp, tp, d = x_prompt.shape
    bs, ts, _ = x_sample.shape
    n_p = bp * tp
    depth = norm_mix.shape[0]
    n_a = w_in_ab.shape[0]
    past_len = page_table.shape[1] * cache_ckv.shape[2]
    lb_all = jnp.cumsum(jax.nn.softmax(hgrn_lb.astype(F32), axis=0), axis=0)[:n_a]
    x = jnp.concatenate([x_prompt.reshape(n_p, d), x_sample.reshape(bs * ts, d)], axis=0)
    a_p, a_s, c_p, c_s = [], [], [], []
    for layer in range(depth):
        j = layer // 2
        if layer % 2 == 0:
            x, sp, ss = _ab_layer(x, n_p, bp, tp, bs, ts, norm_mix[layer], lb_all[j], state_hgrn[j], state_ssm[j],
                                  state_conv[j], w_in_ab[j], w_out_ab[j], hgrn_norm[j], conv_w[j], conv_b[j],
                                  dt_bias[j], a_log[j], d_skip[j], ssm_norm[j])
            a_p.append(sp)
            a_s.append(ss)
        else:
            x, cp, cs = _mla_layer(x, n_p, bp, tp, bs, ts, past_len, norm_mix[layer], cache_ckv, cache_krope,
                                   page_table, j, w_in_c[j], q_norm[j], kv_norm[j], w_uq[j], w_uk[j], w_uv[j],
                                   w_out_c[j])
            c_p.append(cp)
            c_s.append(cs)
        x = hier_moe_block(x, norm_ffn[layer], w_route_group[layer], b_route_group[layer], w_route_expert[layer],
                           b_route_expert[layer], w_gate, w_up, w_down, layer, norm_final,
                           final_norm=(layer == depth - 1))
    y = x

    def stack(items, k):
        return jnp.stack([it[k] for it in items])

    return (y[:n_p].reshape(bp, tp, d), y[n_p:].reshape(bs, ts, d),
            stack(a_p, 0), stack(a_s, 0), stack(a_p, 1), stack(a_s, 1), stack(a_p, 2), stack(a_s, 2),
            stack(c_p, 0), stack(c_s, 0), stack(c_p, 1), stack(c_s, 1))
```

```python
import functools
import math

import jax
import jax.numpy as jnp
import numpy as np
from jax import lax
from jax.experimental import pallas as pl
from jax.experimental.pallas import tpu as pltpu

F32 = jnp.float32
BF16 = jnp.bfloat16

EPS = 1e-6
D_MODEL = 1024
HG_HEADS = 4
HG_D = 128
HG_W = HG_HEADS * HG_D
SSM_HEADS = 16
SSM_P = 64
SSM_N = 128
SSM_GROUPS = 2
SSM_INNER = SSM_HEADS * SSM_P
CONV_W = 4
CONV_DIM = SSM_INNER + 2 * SSM_GROUPS * SSM_N
LANES = 128
SUBLANES = 8
VMEM_LIMIT = 48 * 1024 * 1024
NEG_BIG = -1e30


def _cparams(*sem):
    return pltpu.CompilerParams(dimension_semantics=sem, vmem_limit_bytes=VMEM_LIMIT)


def _dot(a, b):
    return jnp.dot(a, b, preferred_element_type=F32)


def _dot_nt(a, b):
    return lax.dot_general(a, b, (((1,), (1,)), ((), ())), preferred_element_type=F32)


def _dot_tn(a, b):
    return lax.dot_general(a, b, (((0,), (0,)), ((), ())), preferred_element_type=F32)


def _split3(x):
    hi = x.astype(BF16)
    r1 = x - hi.astype(F32)
    mid = r1.astype(BF16)
    lo = (r1 - mid.astype(F32)).astype(BF16)
    return hi, mid, lo


def _silu(x):
    return x * (1.0 / (1.0 + jnp.exp(-x)))


def _sigmoid(x):
    return 1.0 / (1.0 + jnp.exp(-x))


def _as_parts(a):
    return tuple(a) if isinstance(a, (tuple, list)) else (a,)


def _part_specs(parts, tm, transposed=False):
    specs, bounds, start = [], [], 0
    for p in parts:
        n_t = (p.shape[1] if transposed else p.shape[0]) // tm

        def imap(i, lo=start, n_t=n_t):
            j = jnp.clip(i - lo, 0, n_t - 1)
            return (0, j) if transposed else (j, 0)

        specs.append(pl.BlockSpec((p.shape[0], tm) if transposed else (tm, p.shape[1]), imap))
        start += n_t
        bounds.append(start)
    return specs, tuple(bounds)


def _pick_part(i, bounds, refs):
    val = refs[-1][...]
    for k in reversed(range(len(refs) - 1)):
        val = jnp.where(i < bounds[k], refs[k][...], val)
    return val


def _norm_matmul_kernel(*refs, x_bounds, splits, normalize):
    n_x = len(x_bounds)
    g_ref, w_ref = refs[n_x], refs[n_x + 1]
    out_refs = refs[n_x + 2:]
    x = _pick_part(pl.program_id(0), x_bounds, refs[:n_x]).astype(F32)
    if normalize:
        ms = jnp.mean(x * x, axis=-1, keepdims=True)
        x = x * lax.rsqrt(ms + EPS) * g_ref[...]
    h = x.astype(BF16)
    for (a, b), o_ref in zip(splits, out_refs):
        o_ref[...] = _dot(h, w_ref[:, a:b]).astype(o_ref.dtype)


def norm_matmul(x, gain, w, splits, out_dtypes, *, tm, normalize=True):
    x_parts = _as_parts(x)
    k = x_parts[0].shape[1]
    x_specs, x_bounds = _part_specs(x_parts, tm)
    n = x_bounds[-1] * tm
    assert all(p.shape[0] % tm == 0 for p in x_parts)
    kern = functools.partial(_norm_matmul_kernel, x_bounds=x_bounds, splits=tuple(splits), normalize=normalize)
    out_shape = [jax.ShapeDtypeStruct((n, b - a), dt) for (a, b), dt in zip(splits, out_dtypes)]
    out_specs = [pl.BlockSpec((tm, b - a), lambda i: (i, 0)) for (a, b) in splits]
    return pl.pallas_call(
        kern,
        grid=(n // tm,),
        in_specs=x_specs + [pl.BlockSpec((1, k), lambda i: (0, 0)),
                            pl.BlockSpec(w.shape, lambda i: (0, 0))],
        out_specs=out_specs,
        out_shape=out_shape,
        compiler_params=_cparams("parallel"),
        name="norm_matmul",
    )(*x_parts, gain.reshape(1, k), w)


def _matmul_residual_kernel(*refs, res_bounds, a_bounds, transposed):
    i = pl.program_id(0)
    pos = len(res_bounds)
    acc = _pick_part(i, res_bounds, refs[:pos])
    for bounds, tr in zip(a_bounds, transposed):
        a = _pick_part(i, bounds, refs[pos:pos + len(bounds)]).astype(BF16)
        w_ref = refs[pos + len(bounds)]
        pos += len(bounds) + 1
        acc = acc + (_dot_tn(a, w_ref[...]) if tr else _dot(a, w_ref[...]))
    refs[pos][...] = acc


def matmul_residual(res, pairs, *, tm):
    res_parts = _as_parts(res)
    d = res_parts[0].shape[1]
    in_specs, res_bounds = _part_specs(res_parts, tm)
    n = res_bounds[-1] * tm
    args = list(res_parts)
    a_bounds = []
    for a, w, tr in pairs:
        a_parts = _as_parts(a)
        specs, bounds = _part_specs(a_parts, tm, transposed=tr)
        assert bounds[-1] == res_bounds[-1]
        in_specs += specs + [pl.BlockSpec(w.shape, lambda i: (0, 0))]
        args += list(a_parts) + [w]
        a_bounds.append(bounds)
    kern = functools.partial(_matmul_residual_kernel, res_bounds=res_bounds, a_bounds=tuple(a_bounds),
                             transposed=tuple(bool(p[2]) for p in pairs))
    return pl.pallas_call(
        kern,
        grid=(n // tm,),
        in_specs=in_specs,
        out_specs=pl.BlockSpec((tm, d), lambda i: (i, 0)),
        out_shape=jax.ShapeDtypeStruct((n, d), F32),
        compiler_params=_cparams("parallel"),
        name="matmul_residual",
    )(*args)


def _hgrn_level_halves(c):
    halves = []
    b = c // 2
    while b >= 1:
        halves.append(b)
        b //= 2
    return halves


@functools.lru_cache(maxsize=None)
def _hgrn_consts(c):
    t = np.arange(c)
    rows = [t[None, :] <= t[:, None], t[None, :] > t[:, None]]
    masks = []
    for b in _hgrn_level_halves(c):
        blk = t // b
        st = blk * b
        en = st + b - 1
        odd = blk % 2 == 1
        even = ~odd
        rows.append(odd[:, None] & (t[None, :] >= st[:, None]) & (t[None, :] <= t[:, None]))
        rows.append(even[:, None] & (t[None, :] > t[:, None]) & (t[None, :] <= en[:, None]))
        masks.append((t[:, None] // (2 * b) == t[None, :] // (2 * b)) & odd[:, None] & even[None, :])
    masks.append(np.eye(c, dtype=bool))
    rows.append(np.ones((SUBLANES, c), dtype=bool))
    dg = np.concatenate(rows, axis=0).astype(np.float32)
    dg3 = np.concatenate([dg, dg, dg], axis=1)
    mk = np.stack(masks).astype(np.float32)
    return dg3, mk


def _hgrn_kernel(q_ref, f_ref, i_ref, g_ref, lb_ref, nw_ref, dg_ref, mk_ref, s0_ref,
                 o_ref, sfin_ref, s_scr, *, c, t_valid, n_chunks, n_seq):
    ci = pl.program_id(1)

    @pl.when(ci == 0)
    def _():
        s_scr[...] = s0_ref[...]

    for si in range(n_seq):
        rows = slice(si * c, (si + 1) * c)
        _hgrn_chunk(q_ref.at[rows], f_ref.at[rows], i_ref.at[rows], g_ref.at[rows], lb_ref, nw_ref, dg_ref, mk_ref,
                    o_ref.at[rows], s_scr.at[si], c=c, t_valid=t_valid)

    @pl.when(ci == n_chunks - 1)
    def _():
        sfin_ref[...] = s_scr[...]


def _hgrn_chunk(q_ref, f_ref, i_ref, g_ref, lb_ref, nw_ref, dg_ref, mk_ref, o_ref, s_scr, *, c, t_valid):
    lb = lb_ref[...]
    fl = f_ref[...]
    sig = _sigmoid(fl)
    logf = jnp.log(lb + (1.0 - lb) * sig)
    kk = (1.0 - lb) * (1.0 - sig)
    if t_valid < c:
        row = lax.broadcasted_iota(jnp.int32, (c, HG_W), 0)
        live = row < t_valid
        logf = jnp.where(live, logf, 0.0)
        kk = jnp.where(live, kk, 0.0)
    qa = _silu(q_ref[...])
    vv = i_ref[...].astype(BF16)

    lf3 = jnp.concatenate(_split3(logf), axis=0)
    ex = jnp.exp(_dot(dg_ref[...], lf3))
    halves = _hgrn_level_halves(c)
    n_lv = len(halves)
    e_cum = ex[0:c]
    e_rev = ex[c:2 * c]
    q_state = (qa * e_cum).astype(BF16)
    k_state = (kk * e_rev).astype(BF16)
    q_lv = [(qa * ex[(2 + 2 * l) * c:(3 + 2 * l) * c]).astype(BF16) for l in range(n_lv)]
    k_lv = [(kk * ex[(3 + 2 * l) * c:(4 + 2 * l) * c]).astype(BF16) for l in range(n_lv)]
    q_lv.append(qa.astype(BF16))
    k_lv.append(kk.astype(BF16))
    ones_t = jnp.ones((3 * c, HG_D), BF16)
    gate = _silu(g_ref[...])
    nw = nw_ref[...]

    for h in range(HG_HEADS):
        sl = slice(h * HG_D, (h + 1) * HG_D)
        sc = jnp.zeros((c, c), F32)
        for l in range(n_lv + 1):
            sc = sc + mk_ref[l] * _dot_nt(q_lv[l][:, sl], k_lv[l][:, sl])
        s_h = s_scr[h]
        o_h = _dot(sc.astype(BF16), vv[:, sl]) + _dot(q_state[:, sl], s_h.astype(BF16))
        dec = jnp.exp(_dot_tn(lf3[:, sl], ones_t))
        s_scr[h] = s_h * dec + _dot_tn(k_state[:, sl], vv[:, sl])
        ms = jnp.mean(o_h * o_h, axis=-1, keepdims=True)
        o_h = o_h * lax.rsqrt(ms + EPS) * nw[:, sl] * gate[:, sl]
        o_ref[:, sl] = o_h.astype(o_ref.dtype)


def hgrn_scan(q, f, i_in, g, lb, norm_w, s0, *, b, t, c, t_valid, n_seq=1):
    n_chunks = t // c
    assert t % c == 0 and (t_valid == c or n_chunks == 1) and q.shape[0] >= b * t
    assert b % n_seq == 0 and (n_seq == 1 or n_chunks == 1)
    dg3, mk = _hgrn_consts(c)
    dg3 = jnp.asarray(dg3, BF16)
    mk = jnp.asarray(mk, F32)
    tok = pl.BlockSpec((n_seq * c, HG_W), lambda bi, ci: (bi * n_chunks + ci, 0))
    st = pl.BlockSpec((n_seq, HG_HEADS, HG_D, HG_D), lambda bi, ci: (bi, 0, 0, 0))
    row = pl.BlockSpec((1, HG_W), lambda bi, ci: (0, 0))
    kern = functools.partial(_hgrn_kernel, c=c, t_valid=t_valid, n_chunks=n_chunks, n_seq=n_seq)
    return pl.pallas_call(
        kern,
        grid=(b // n_seq, n_chunks),
        in_specs=[tok, tok, tok, tok, row, row,
                  pl.BlockSpec(dg3.shape, lambda bi, ci: (0, 0)),
                  pl.BlockSpec(mk.shape, lambda bi, ci: (0, 0, 0)),
                  st],
        out_specs=[tok, st],
        out_shape=[jax.ShapeDtypeStruct((b * t, HG_W), BF16),
                   jax.ShapeDtypeStruct((b, HG_HEADS, HG_D, HG_D), F32)],
        scratch_shapes=[pltpu.VMEM((n_seq, HG_HEADS, HG_D, HG_D), F32)],
        compiler_params=_cparams("parallel", "arbitrary"),
        name="hgrn_scan",
    )(q, f, i_in, g, lb.reshape(1, HG_W), norm_w.reshape(1, HG_W), dg3, mk, s0)


CONV_HEAD = 8


@functools.lru_cache(maxsize=None)
def _ssd_consts(c):
    t = np.arange(c)
    tri = (t[None, :] <= t[:, None]).astype(np.float32)
    rev = (t[None, :] > t[:, None]).astype(np.float32)
    ones = np.ones((SUBLANES, c), np.float32)
    tg = np.concatenate([tri, rev, ones], axis=0)
    tg3 = np.concatenate([tg, tg, tg], axis=1)
    u3 = np.concatenate([tri.T, tri.T, tri.T], axis=0)
    return tg3, u3


def _ssd_kernel(z_ref, xbc_ref, dt_ref, cbuf_ref, cw_ref, cb_ref, dtb_ref, alog_ref, dsk_ref, nw_ref,
                tg_ref, u_ref, h0_ref,
                y_ref, cnew_ref, hfin_ref, xp_scr, h_scr, y_scr, *, c, t_valid, n_chunks, n_seq):
    ci = pl.program_id(1)

    @pl.when(ci == 0)
    def _():
        h_scr[...] = h0_ref[...]
        xp_scr[:, CONV_HEAD - (CONV_W - 1):CONV_HEAD, :] = cbuf_ref[...]

    for si in range(n_seq):
        rows = slice(si * c, (si + 1) * c)
        _ssd_chunk(ci, z_ref.at[rows], xbc_ref.at[rows], dt_ref.at[rows], cw_ref, cb_ref, dtb_ref, alog_ref, dsk_ref,
                   nw_ref, tg_ref, u_ref, y_ref.at[rows], cnew_ref.at[si], xp_scr.at[si], h_scr.at[si], y_scr.at[si],
                   c=c, t_valid=t_valid, n_chunks=n_chunks)

    @pl.when(ci == n_chunks - 1)
    def _():
        hfin_ref[...] = h_scr[...]


def _ssd_chunk(ci, z_ref, xbc_ref, dt_ref, cw_ref, cb_ref, dtb_ref, alog_ref, dsk_ref, nw_ref, tg_ref, u_ref,
               y_ref, cnew_ref, xp_scr, h_scr, y_scr, *, c, t_valid, n_chunks):
    xp_scr[CONV_HEAD:CONV_HEAD + c, :] = xbc_ref[...]
    conv = cb_ref[...]
    for i in range(CONV_W):
        off = CONV_HEAD - (CONV_W - 1) + i
        conv = conv + xp_scr[off:off + c, :] * cw_ref[i:i + 1, :]
    conv = _silu(conv)

    @pl.when(ci == n_chunks - 1)
    def _():
        cnew_ref[...] = xp_scr[CONV_HEAD + t_valid - (CONV_W - 1):CONV_HEAD + t_valid, :]

    xp_scr[CONV_HEAD - (CONV_W - 1):CONV_HEAD, :] = xp_scr[CONV_HEAD + c - (CONV_W - 1):CONV_HEAD + c, :]

    xs = conv[:, :SSM_INNER]
    dt_raw = dt_ref[...] + dtb_ref[...]
    dt = jnp.maximum(dt_raw, 0.0) + jnp.log(1.0 + jnp.exp(-jnp.abs(dt_raw)))
    if t_valid < c:
        row = lax.broadcasted_iota(jnp.int32, (c, LANES), 0)
        dt = jnp.where(row < t_valid, dt, 0.0)
    a = -jnp.exp(alog_ref[...])
    da = dt * a
    da3 = jnp.concatenate(_split3(da), axis=0)
    xx = _dot(tg_ref[...], da3)
    cum = xx[0:c]
    e_cum = jnp.exp(cum)
    w_all = jnp.exp(xx[c:2 * c]) * dt
    e_last = jnp.exp(xx[2 * c:2 * c + 1])
    cum_t = _dot_tn(da3, u_ref[...])

    tril = lax.broadcasted_iota(jnp.int32, (c, c), 0) >= lax.broadcasted_iota(jnp.int32, (c, c), 1)
    heads_per_group = SSM_HEADS // SSM_GROUPS
    for g in range(SSM_GROUPS):
        bm = conv[:, SSM_INNER + g * SSM_N:SSM_INNER + (g + 1) * SSM_N]
        cm = conv[:, SSM_INNER + (SSM_GROUPS + g) * SSM_N:SSM_INNER + (SSM_GROUPS + g + 1) * SSM_N]
        bm_b = bm.astype(BF16)
        cb = _dot_nt(cm.astype(BF16), bm_b)
        for hh in range(heads_per_group):
            h = g * heads_per_group + hh
            x_h = xs[:, h * SSM_P:(h + 1) * SSM_P]
            diff = cum[:, h:h + 1] - cum_t[h:h + 1, :]
            seg = jnp.exp(jnp.where(tril, diff, NEG_BIG))
            xd = (dt[:, h:h + 1] * x_h).astype(BF16)
            cme = (cm * e_cum[:, h:h + 1]).astype(BF16)
            h_h = h_scr[h]
            y_h = _dot((cb * seg).astype(BF16), xd) + _dot_nt(cme, h_h.astype(BF16))
            xw = (w_all[:, h:h + 1] * x_h).astype(BF16)
            h_scr[h] = h_h * e_last[:, h:h + 1] + _dot_tn(xw, bm_b)
            y_scr[:, h * SSM_P:(h + 1) * SSM_P] = y_h

    y = (y_scr[...] + dsk_ref[...] * xs) * _silu(z_ref[...])
    gw = SSM_INNER // SSM_GROUPS
    for g in range(SSM_GROUPS):
        yg = y[:, g * gw:(g + 1) * gw]
        ms = jnp.mean(yg * yg, axis=-1, keepdims=True)
        y_ref[:, g * gw:(g + 1) * gw] = (yg * lax.rsqrt(ms + EPS) * nw_ref[:, g * gw:(g + 1) * gw]).astype(y_ref.dtype)


def ssd_scan(z, xbc, dt, conv_buf, h0, conv_w, conv_b, dt_bias, a_log, d_skip, norm_w, *, b, t, c, t_valid,
             n_seq=1):
    n_chunks = t // c
    assert t % c == 0 and (t_valid == c or n_chunks == 1) and z.shape[0] >= b * t
    assert b % n_seq == 0 and (n_seq == 1 or n_chunks == 1)
    tg3, u3 = _ssd_consts(c)
    tg3 = jnp.asarray(tg3, BF16)
    u3 = jnp.asarray(u3, BF16)

    def pad_heads(v):
        return jnp.pad(v.astype(F32), (0, LANES - SSM_HEADS)).reshape(1, LANES)

    def tok(w):
        return pl.BlockSpec((n_seq * c, w), lambda bi, ci: (bi * n_chunks + ci, 0))

    def const(shape):
        return pl.BlockSpec(shape, lambda bi, ci: (0,) * len(shape))

    cst = pl.BlockSpec((n_seq, CONV_W - 1, CONV_DIM), lambda bi, ci: (bi, 0, 0))
    hst = pl.BlockSpec((n_seq, SSM_HEADS, SSM_P, SSM_N), lambda bi, ci: (bi, 0, 0, 0))
    kern = functools.partial(_ssd_kernel, c=c, t_valid=t_valid, n_chunks=n_chunks, n_seq=n_seq)
    return pl.pallas_call(
        kern,
        grid=(b // n_seq, n_chunks),
        in_specs=[tok(SSM_INNER), tok(CONV_DIM), tok(LANES), cst,
                  const((CONV_W, CONV_DIM)), const((1, CONV_DIM)), const((1, LANES)), const((1, LANES)),
                  const((1, SSM_INNER)), const((1, SSM_INNER)), const(tg3.shape), const(u3.shape), hst],
        out_specs=[tok(SSM_INNER), cst, hst],
        out_shape=[jax.ShapeDtypeStruct((b * t, SSM_INNER), BF16),
                   jax.ShapeDtypeStruct((b, CONV_W - 1, CONV_DIM), F32),
                   jax.ShapeDtypeStruct((b, SSM_HEADS, SSM_P, SSM_N), F32)],
        scratch_shapes=[pltpu.VMEM((n_seq, CONV_HEAD + c, CONV_DIM), F32),
                        pltpu.VMEM((n_seq, SSM_HEADS, SSM_P, SSM_N), F32),
                        pltpu.VMEM((n_seq, c, SSM_INNER), F32)],
        compiler_params=_cparams("parallel", "arbitrary"),
        name="ssd_scan",
    )(z, xbc, dt, conv_buf, conv_w, conv_b.reshape(1, CONV_DIM), pad_heads(dt_bias), pad_heads(a_log),
      jnp.repeat(d_skip.astype(F32), SSM_P).reshape(1, SSM_INNER), norm_w.reshape(1, SSM_INNER), tg3, u3, h0)


N_GROUPS = 4
EXPERTS_PER_GROUP = 8
N_EXPERTS = N_GROUPS * EXPERTS_PER_GROUP
TOP_K = 2
D_EXPERT = 256
MOE_TM = 256


def _router_kernel(x_ref, g_ref, w_ref, b_ref, tri_ref, xn_ref, rw_ref, ri_ref, cnt_ref, carry_scr):
    @pl.when(pl.program_id(0) == 0)
    def _():
        carry_scr[...] = jnp.zeros_like(carry_scr)

    x = x_ref[...]
    ms = jnp.mean(x * x, axis=-1, keepdims=True)
    xn = x * lax.rsqrt(ms + EPS) * g_ref[...]
    for s in range(SUBLANES):
        xn_ref[pl.ds(s, x.shape[0], stride=SUBLANES), :] = xn[:, s * LANES:(s + 1) * LANES]
    logits = jnp.dot(xn, w_ref[...], precision=lax.Precision.HIGHEST, preferred_element_type=F32) + b_ref[...]
    lane = lax.broadcasted_iota(jnp.int32, logits.shape, 1)
    is_g = (lane >= N_EXPERTS) & (lane < N_EXPERTS + N_GROUPS)
    gl = jnp.where(is_g, logits, NEG_BIG)
    gmax = jnp.max(gl, axis=-1, keepdims=True)
    g_sel = jnp.min(jnp.where(gl == gmax, lane, 4 * LANES), axis=-1, keepdims=True) - N_EXPERTS
    g_w = 1.0 / jnp.sum(jnp.where(is_g, jnp.exp(gl - gmax), 0.0), axis=-1, keepdims=True)
    lo = g_sel * EXPERTS_PER_GROUP
    in_grp = (lane >= lo) & (lane < lo + EXPERTS_PER_GROUP)
    el = jnp.where(in_grp, logits, NEG_BIG)
    emax = jnp.max(el, axis=-1, keepdims=True)
    ee = jnp.where(in_grp, jnp.exp(el - emax), 0.0)
    p = ee / jnp.sum(ee, axis=-1, keepdims=True)
    p = jnp.where(in_grp, p, -1.0)
    p1 = jnp.max(p, axis=-1, keepdims=True)
    i1 = jnp.min(jnp.where(p == p1, lane, 4 * LANES), axis=-1, keepdims=True)
    p_rest = jnp.where(lane == i1, -1.0, p)
    p2 = jnp.max(p_rest, axis=-1, keepdims=True)
    i2 = jnp.min(jnp.where(p_rest == p2, lane, 4 * LANES), axis=-1, keepdims=True)
    w1 = p1 / (p1 + p2) * g_w
    w2 = p2 / (p1 + p2) * g_w
    rw_ref[...] = jnp.where(lane == 0, w1, jnp.where(lane == 1, w2, 0.0))
    hit1 = lane == i1
    hit2 = lane == i2
    onehot = (hit1 | hit2).astype(BF16)
    before = _dot(tri_ref[...], onehot) + carry_scr[0:1, :]
    r1 = jnp.sum(jnp.where(hit1, before, 0.0), axis=-1, keepdims=True).astype(jnp.int32)
    r2 = jnp.sum(jnp.where(hit2, before, 0.0), axis=-1, keepdims=True).astype(jnp.int32)
    ri_ref[...] = jnp.where(lane == 0, i1, jnp.where(lane == 1, i2, jnp.where(lane == 2, r1, jnp.where(lane == 3, r2, 0))))
    carry_scr[...] = carry_scr[...] + jnp.sum(onehot.astype(F32), axis=0, keepdims=True)
    cnt_ref[...] = carry_scr[...]


def moe_router(x, gain, w_rg, b_rg, w_re, b_re, *, tm):
    n, d = x.shape
    assert d == SUBLANES * LANES
    w = jnp.zeros((d, LANES), F32).at[:, :N_EXPERTS].set(w_re).at[:, N_EXPERTS:N_EXPERTS + N_GROUPS].set(w_rg)
    b = jnp.zeros((1, LANES), F32).at[0, :N_EXPERTS].set(b_re).at[0, N_EXPERTS:N_EXPERTS + N_GROUPS].set(b_rg)
    t = np.arange(tm)
    tri = jnp.asarray(t[None, :] < t[:, None], BF16)
    return pl.pallas_call(
        _router_kernel,
        grid=(n // tm,),
        in_specs=[pl.BlockSpec((tm, d), lambda i: (i, 0)),
                  pl.BlockSpec((1, d), lambda i: (0, 0)),
                  pl.BlockSpec((d, LANES), lambda i: (0, 0)),
                  pl.BlockSpec((1, LANES), lambda i: (0, 0)),
                  pl.BlockSpec((tm, tm), lambda i: (0, 0))],
        out_specs=[pl.BlockSpec((tm * SUBLANES, LANES), lambda i: (i, 0)),
                   pl.BlockSpec((tm, LANES), lambda i: (i, 0)),
                   pl.BlockSpec((tm, LANES), lambda i: (i, 0)),
                   pl.BlockSpec((SUBLANES, LANES), lambda i: (0, 0))],
        out_shape=[jax.ShapeDtypeStruct((n * SUBLANES, LANES), F32),
                   jax.ShapeDtypeStruct((n, LANES), F32),
                   jax.ShapeDtypeStruct((n, LANES), jnp.int32),
                   jax.ShapeDtypeStruct((SUBLANES, LANES), F32)],
        scratch_shapes=[pltpu.VMEM((SUBLANES, LANES), F32)],
        compiler_params=_cparams("arbitrary"),
        name="moe_router",
    )(x, gain.reshape(1, d), w, b, tri)


def _token_rows(idx):
    return pl.ds(pl.multiple_of(idx * SUBLANES, SUBLANES), SUBLANES)


def _moe_dispatch_kernel(dest_ref, xn_ref, zero_ref, xs_ref, sem, *, tm):
    del zero_ref
    base = pl.program_id(0) * (tm * TOP_K)

    def start(r, carry):
        for k in range(TOP_K):
            pltpu.make_async_copy(xn_ref.at[_token_rows(r)], xs_ref.at[_token_rows(dest_ref[base + r * TOP_K + k])],
                                  sem).start()
        return carry

    lax.fori_loop(0, tm, start, 0, unroll=8)

    def wait(r, carry):
        for k in range(TOP_K):
            pltpu.make_async_copy(xn_ref.at[_token_rows(0)], xs_ref.at[_token_rows(0)], sem).wait()
        return carry

    lax.fori_loop(0, tm, wait, 0, unroll=8)


def moe_dispatch(dest, xn, n_rows, *, tm):
    n = xn.shape[0] // SUBLANES
    grid_spec = pltpu.PrefetchScalarGridSpec(
        num_scalar_prefetch=1,
        grid=(n // tm,),
        in_specs=[pl.BlockSpec((tm * SUBLANES, LANES), lambda i, d: (i, 0)),
                  pl.BlockSpec(memory_space=pl.ANY)],
        out_specs=pl.BlockSpec(memory_space=pl.ANY),
        scratch_shapes=[pltpu.SemaphoreType.DMA(())],
    )
    return pl.pallas_call(
        functools.partial(_moe_dispatch_kernel, tm=tm),
        grid_spec=grid_spec,
        out_shape=jax.ShapeDtypeStruct((n_rows * SUBLANES, LANES), F32),
        input_output_aliases={2: 0},
        compiler_params=_cparams("arbitrary"),
        name="moe_dispatch",
    )(dest, xn, jnp.zeros((n_rows * SUBLANES, LANES), F32))


def _moe_ffn_kernel(te_ref, nu_ref, x_ref, wg_ref, wu_ref, wd_ref, y_ref):
    i = pl.program_id(0)

    @pl.when(i < nu_ref[0])
    def _():
        x = jnp.concatenate([x_ref[pl.ds(s, MOE_TM, stride=SUBLANES), :] for s in range(SUBLANES)], axis=1).astype(BF16)
        a = _dot(x, wg_ref[...].astype(BF16))
        b = _dot(x, wu_ref[...].astype(BF16))
        hid = (_silu(a) * b).astype(BF16)
        y = _dot(hid, wd_ref[...].astype(BF16))
        for s in range(SUBLANES):
            y_ref[pl.ds(s, MOE_TM, stride=SUBLANES), :] = y[:, s * LANES:(s + 1) * LANES]

    @pl.when(i >= nu_ref[0])
    def _():
        y_ref[...] = jnp.zeros_like(y_ref)


def moe_ffn(tile_expert, n_used, x_sorted, w_gate, w_up, w_down, layer):
    n_tiles = x_sorted.shape[0] // (MOE_TM * SUBLANES)
    d = w_gate.shape[2]
    tile = pl.BlockSpec((MOE_TM * SUBLANES, LANES), lambda i, te, nu: (i, 0))
    grid_spec = pltpu.PrefetchScalarGridSpec(
        num_scalar_prefetch=2,
        grid=(n_tiles,),
        in_specs=[tile,
                  pl.BlockSpec((None, None, d, D_EXPERT), lambda i, te, nu: (layer, te[i], 0, 0)),
                  pl.BlockSpec((None, None, d, D_EXPERT), lambda i, te, nu: (layer, te[i], 0, 0)),
                  pl.BlockSpec((None, None, D_EXPERT, d), lambda i, te, nu: (layer, te[i], 0, 0))],
        out_specs=tile,
    )
    return pl.pallas_call(
        _moe_ffn_kernel,
        grid_spec=grid_spec,
        out_shape=jax.ShapeDtypeStruct(x_sorted.shape, F32),
        compiler_params=_cparams("arbitrary"),
        name="moe_ffn",
    )(tile_expert, n_used, x_sorted, w_gate, w_up, w_down)


def _moe_combine_kernel(dest_ref, x_ref, rw_ref, fg_ref, ys_ref, *rest, tm, n_steps, final_norm, head_steps):
    out_refs, (ybuf, sem) = rest[:-2], rest[-2:]
    i = pl.program_id(0)
    slot = i % 2
    slot_rows = tm * TOP_K * SUBLANES

    def issue(step, to_slot):
        base = step * (tm * TOP_K)

        def body(r, carry):
            for k in range(TOP_K):
                dst = pl.ds(pl.multiple_of(to_slot * slot_rows + (r * TOP_K + k) * SUBLANES, SUBLANES), SUBLANES)
                pltpu.make_async_copy(ys_ref.at[_token_rows(dest_ref[base + r * TOP_K + k])], ybuf.at[dst],
                                      sem.at[to_slot]).start()
            return carry

        lax.fori_loop(0, tm, body, 0, unroll=8)

    @pl.when(i == 0)
    def _():
        issue(0, 0)

    @pl.when(i + 1 < n_steps)
    def _():
        issue(i + 1, 1 - slot)

    def wait(r, carry):
        for k in range(TOP_K):
            pltpu.make_async_copy(ys_ref.at[_token_rows(0)], ybuf.at[_token_rows(0)], sem.at[slot]).wait()
        return carry

    lax.fori_loop(0, tm, wait, 0, unroll=8)

    rw = rw_ref[...]
    g0 = rw[:, 0:1]
    g1 = rw[:, 1:2]
    first = pl.multiple_of(slot * slot_rows, SUBLANES)
    pieces = []
    for s in range(SUBLANES):
        y0 = ybuf[pl.ds(first + s, tm, stride=TOP_K * SUBLANES), :]
        y1 = ybuf[pl.ds(first + SUBLANES + s, tm, stride=TOP_K * SUBLANES), :]
        pieces.append(x_ref[:, s * LANES:(s + 1) * LANES] + (g0 * y0 + g1 * y1))
    if final_norm:
        sq = sum(jnp.sum(p * p, axis=-1, keepdims=True) for p in pieces)
        scale = lax.rsqrt(sq * (1.0 / (SUBLANES * LANES)) + EPS)
        pieces = [p * scale * fg_ref[:, s * LANES:(s + 1) * LANES] for s, p in enumerate(pieces)]
    def write(o_ref):
        for s, p in enumerate(pieces):
            o_ref[:, s * LANES:(s + 1) * LANES] = p

    if head_steps is None:
        write(out_refs[0])
    else:
        @pl.when(i < head_steps)
        def _():
            write(out_refs[0])

        @pl.when(i >= head_steps)
        def _():
            write(out_refs[1])


def moe_combine(dest, x, rw, y_sorted, final_gain, *, tm, final_norm, head_rows=None):
    n, d = x.shape
    n_steps = n // tm
    if head_rows is None:
        head_steps = None
        out_specs = [pl.BlockSpec((tm, d), lambda i, dd: (i, 0))]
        out_shape = [jax.ShapeDtypeStruct((n, d), F32)]
    else:
        assert head_rows % tm == 0 and 0 < head_rows < n
        head_steps = head_rows // tm
        out_specs = [pl.BlockSpec((tm, d), lambda i, dd: (jnp.minimum(i, head_steps - 1), 0)),
                     pl.BlockSpec((tm, d), lambda i, dd: (jnp.maximum(i - head_steps, 0), 0))]
        out_shape = [jax.ShapeDtypeStruct((head_rows, d), F32), jax.ShapeDtypeStruct((n - head_rows, d), F32)]
    grid_spec = pltpu.PrefetchScalarGridSpec(
        num_scalar_prefetch=1,
        grid=(n_steps,),
        in_specs=[pl.BlockSpec((tm, d), lambda i, dd: (i, 0)),
                  pl.BlockSpec((tm, LANES), lambda i, dd: (i, 0)),
                  pl.BlockSpec((1, d), lambda i, dd: (0, 0)),
                  pl.BlockSpec(memory_space=pl.ANY)],
        out_specs=out_specs,
        scratch_shapes=[pltpu.VMEM((2 * tm * TOP_K * SUBLANES, LANES), F32),
                        pltpu.SemaphoreType.DMA((2,))],
    )
    outs = pl.pallas_call(
        functools.partial(_moe_combine_kernel, tm=tm, n_steps=n_steps, final_norm=final_norm, head_steps=head_steps),
        grid_spec=grid_spec,
        out_shape=out_shape,
        compiler_params=_cparams("arbitrary"),
        name="moe_combine",
    )(dest, x, rw, final_gain.reshape(1, d), y_sorted)
    return outs[0] if head_rows is None else tuple(outs)


def _moe_plan(ri, counts, n):
    eid = ri[:, :TOP_K]
    rank = ri[:, TOP_K:2 * TOP_K]
    counts = counts[0, :N_EXPERTS].astype(jnp.int32)
    tiles = (counts + MOE_TM - 1) // MOE_TM
    tile_end = jnp.cumsum(tiles)
    tile_start = tile_end - tiles
    onehot = eid[:, :, None] == jnp.arange(N_EXPERTS, dtype=jnp.int32)[None, None, :]
    dest = jnp.sum(jnp.where(onehot, tile_start[None, None, :], 0), axis=-1) * MOE_TM + rank
    n_rows = TOP_K * n + N_EXPERTS * MOE_TM
    n_tiles = n_rows // MOE_TM
    n_used = tile_end[-1]
    t_idx = jnp.minimum(jnp.arange(n_tiles, dtype=jnp.int32), n_used - 1)
    tile_expert = jnp.sum((t_idx[:, None] >= tile_end[None, :]).astype(jnp.int32), axis=1)
    return dest.reshape(-1).astype(jnp.int32), tile_expert.astype(jnp.int32), n_used.reshape(1).astype(jnp.int32), n_rows


def hier_moe_block(x, gain, w_rg, b_rg, w_re, b_re, w_gate, w_up, w_down, layer, final_gain, final_norm,
                   head_rows=None):
    n, d = x.shape
    xn, rw, ri, counts = moe_router(x, gain, w_rg, b_rg, w_re, b_re, tm=512)
    dest, tile_expert, n_used, n_rows = _moe_plan(ri, counts, n)
    x_sorted = moe_dispatch(dest, xn, n_rows, tm=512)
    y_sorted = moe_ffn(tile_expert, n_used, x_sorted, w_gate, w_up, w_down, layer)
    return moe_combine(dest, x, rw, y_sorted, final_gain, tm=256, final_norm=final_norm, head_rows=head_rows)


MLA_HEADS = 16
MLA_LORA = 256
MLA_NOPE = 64
MLA_ROPE = 32
MLA_V = 64
MLA_SCALE = (MLA_NOPE + MLA_ROPE) ** -0.5
Q_SCALE = MLA_SCALE * math.log2(math.e)
ROPE_THETA = 10000.0
HEAD_PAD = 128
ROPE_AT = MLA_NOPE
Q_DEC = 384
V_ROWS = 80


def _rope_tables(pos):
    half = MLA_ROPE // 2
    inv = ROPE_THETA ** (-jnp.arange(half, dtype=F32) / half)
    ang = pos.astype(F32)[:, None] * inv[None, :]
    cos, sin = jnp.cos(ang), jnp.sin(ang)
    n = pos.shape[0]
    ones = jnp.ones((n, MLA_NOPE), F32)
    zeros_n = jnp.zeros((n, MLA_NOPE), F32)
    zeros_p = jnp.zeros((n, HEAD_PAD - MLA_NOPE - MLA_ROPE), F32)
    ctab = jnp.concatenate([ones, cos, cos, zeros_p], axis=1)
    stab = jnp.concatenate([zeros_n, sin, sin, zeros_p], axis=1)
    return ctab, stab


def _mla_weights(w_in, w_uq, w_uk, w_uv):
    d = w_in.shape[0]
    half = MLA_ROPE // 2
    w_kr = w_in[:, 2 * MLA_LORA:]
    zl = jnp.zeros((d, ROPE_AT), F32)
    zr = jnp.zeros((d, HEAD_PAD - ROPE_AT - MLA_ROPE), F32)
    kr_a = jnp.concatenate([zl, w_kr, zr], axis=1)
    kr_b = jnp.concatenate([zl, -w_kr[:, half:], w_kr[:, :half], zr], axis=1)
    w_in_p = jnp.concatenate([w_in[:, :2 * MLA_LORA], kr_a, kr_b], axis=1).astype(BF16)
    wq = w_uq.reshape(MLA_LORA, MLA_HEADS, MLA_NOPE + MLA_ROPE)
    nope, x1, x2 = wq[..., :MLA_NOPE], wq[..., MLA_NOPE:MLA_NOPE + half], wq[..., MLA_NOPE + half:]
    zp = jnp.zeros((MLA_LORA, MLA_HEADS, HEAD_PAD - MLA_NOPE - MLA_ROPE), F32)
    wq_a = jnp.concatenate([nope, x1, x2, zp], axis=-1).reshape(MLA_LORA, MLA_HEADS * HEAD_PAD).astype(BF16)
    wq_b = jnp.concatenate([jnp.zeros_like(nope), -x2, x1, zp], axis=-1).reshape(MLA_LORA, MLA_HEADS * HEAD_PAD).astype(BF16)
    zk = jnp.zeros((MLA_LORA, MLA_HEADS, HEAD_PAD - MLA_NOPE), F32)
    wuk_p = jnp.concatenate([w_uk, zk], axis=-1).reshape(MLA_LORA, MLA_HEADS * HEAD_PAD).astype(BF16)
    wuv = w_uv.reshape(MLA_LORA, MLA_HEADS * MLA_V).astype(BF16)
    absorb = jnp.transpose(w_uk, (1, 2, 0))
    sel = jnp.zeros((MLA_ROPE, Q_DEC - MLA_LORA), F32).at[jnp.arange(MLA_ROPE), jnp.arange(MLA_ROPE)].set(1.0)
    top = jnp.concatenate([absorb, jnp.zeros((MLA_HEADS, MLA_NOPE, Q_DEC - MLA_LORA), F32)], axis=-1)
    mid = jnp.broadcast_to(jnp.concatenate([jnp.zeros((MLA_ROPE, MLA_LORA), F32), sel], axis=-1)[None],
                           (MLA_HEADS, MLA_ROPE, Q_DEC))
    bot = jnp.zeros((MLA_HEADS, HEAD_PAD - MLA_NOPE - MLA_ROPE, Q_DEC), F32)
    w_dec = jnp.concatenate([top, mid, bot], axis=1).astype(BF16)
    return w_in_p, wq_a, wq_b, wuk_p, wuv, w_dec


def _mla_q_kernel(cq_ref, ckv_ref, kra_ref, krb_ref, ct_ref, st_ref, ctt_ref, stt_ref, qn_ref, kvn_ref,
                  wa_ref, wb_ref, qt_ref, ckvn_ref, krot_ref):
    cq = cq_ref[...]
    ms = jnp.mean(cq * cq, axis=-1, keepdims=True)
    cqn = (cq * lax.rsqrt(ms + EPS) * qn_ref[...]).astype(BF16)
    ckv = ckv_ref[...]
    ms2 = jnp.mean(ckv * ckv, axis=-1, keepdims=True)
    ckvn_ref[...] = ckv * lax.rsqrt(ms2 + EPS) * kvn_ref[...]
    krot_ref[...] = kra_ref[...] * ct_ref[...] + krb_ref[...] * st_ref[...]
    ctt = ctt_ref[...]
    stt = stt_ref[...]
    for h in range(MLA_HEADS):
        sl = slice(h * HEAD_PAD, (h + 1) * HEAD_PAD)
        qh = _dot_nt(wa_ref[sl, :], cqn) * ctt + _dot_nt(wb_ref[sl, :], cqn) * stt
        qt_ref[sl, :] = (qh * Q_SCALE).astype(qt_ref.dtype)


def mla_q(cq, ckv, kr_a, kr_b, ctab, stab, q_norm, kv_norm, wq_at, wq_bt, *, tm):
    n = cq.shape[0]
    hw = MLA_HEADS * HEAD_PAD

    def tok(w):
        return pl.BlockSpec((tm, w), lambda i: (i, 0))

    def tok_t(w):
        return pl.BlockSpec((w, tm), lambda i: (0, i))

    def const(shape):
        return pl.BlockSpec(shape, lambda i: (0, 0))

    return pl.pallas_call(
        _mla_q_kernel,
        grid=(n // tm,),
        in_specs=[tok(MLA_LORA), tok(MLA_LORA), tok(HEAD_PAD), tok(HEAD_PAD), tok(HEAD_PAD), tok(HEAD_PAD),
                  tok_t(HEAD_PAD), tok_t(HEAD_PAD),
                  const((1, MLA_LORA)), const((1, MLA_LORA)), const(wq_at.shape), const(wq_bt.shape)],
        out_specs=[tok_t(hw), tok(MLA_LORA), tok(HEAD_PAD)],
        out_shape=[jax.ShapeDtypeStruct((hw, n), BF16),
                   jax.ShapeDtypeStruct((n, MLA_LORA), F32),
                   jax.ShapeDtypeStruct((n, HEAD_PAD), F32)],
        compiler_params=_cparams("parallel"),
        name="mla_q",
    )(cq, ckv, kr_a, kr_b, ctab, stab, ctab.T, stab.T, q_norm.reshape(1, MLA_LORA), kv_norm.reshape(1, MLA_LORA),
      wq_at, wq_bt)


def _mla_kv_kernel(ckvn_ref, krot_ref, wk_ref, wvt_ref, k_ref, vt_ref):
    c = ckvn_ref[...].astype(BF16)
    krot = krot_ref[...]
    for h in range(MLA_HEADS):
        sl = slice(h * HEAD_PAD, (h + 1) * HEAD_PAD)
        k_ref[:, sl] = (_dot(c, wk_ref[:, sl]) + krot).astype(k_ref.dtype)
    vt = _dot_nt(wvt_ref[...], c)
    row = lax.broadcasted_iota(jnp.int32, vt.shape, 0)
    vt_ref[...] = jnp.where(row % V_ROWS == MLA_V, 1.0, vt).astype(vt_ref.dtype)


def mla_kv(ckvn, krot, wuk_p, wuv_t, *, n, tm):
    hw = MLA_HEADS * HEAD_PAD
    vw = MLA_HEADS * V_ROWS
    wuv_t = jnp.pad(wuv_t.reshape(MLA_HEADS, MLA_V, MLA_LORA), ((0, 0), (0, V_ROWS - MLA_V), (0, 0))).reshape(vw, MLA_LORA)
    return pl.pallas_call(
        _mla_kv_kernel,
        grid=(n // tm,),
        in_specs=[pl.BlockSpec((tm, MLA_LORA), lambda i: (i, 0)),
                  pl.BlockSpec((tm, HEAD_PAD), lambda i: (i, 0)),
                  pl.BlockSpec(wuk_p.shape, lambda i: (0, 0)),
                  pl.BlockSpec(wuv_t.shape, lambda i: (0, 0))],
        out_specs=[pl.BlockSpec((tm, hw), lambda i: (i, 0)),
                   pl.BlockSpec((None, vw, tm), lambda i: (i, 0, 0))],
        out_shape=[jax.ShapeDtypeStruct((n, hw), BF16),
                   jax.ShapeDtypeStruct((n // tm, vw, tm), BF16)],
        compiler_params=_cparams("parallel"),
        name="mla_kv",
    )(ckvn, krot, wuk_p, wuv_t)


def _flash_kernel(qt_ref, k_ref, vt_ref, ot_ref, m_scr, acc_scr, p_scr, alpha_scr, *, tq, tkb):
    qi = pl.program_id(2)
    sub = tq // tkb
    m_scr[...] = jnp.full_like(m_scr, NEG_BIG)
    acc_scr[...] = jnp.zeros_like(acc_scr)
    p_scr[...] = jnp.zeros_like(p_scr)
    alpha_scr[...] = jnp.ones_like(alpha_scr)

    def retire(kb_prev):
        for hh in range(2):
            pv = _dot(vt_ref[kb_prev, hh * V_ROWS:(hh + 1) * V_ROWS, :], p_scr[hh])
            acc_scr[hh] = alpha_scr[hh] * acc_scr[hh] + pv

    strips = [(hh, st) for hh in range(2) for st in range(tq // FLASH_STRIP)]

    def score(kb, hh, st):
        rows = pl.ds(pl.multiple_of(kb * tkb, tkb), tkb)
        return _dot(k_ref[rows, hh * HEAD_PAD:(hh + 1) * HEAD_PAD],
                    qt_ref[hh * HEAD_PAD:(hh + 1) * HEAD_PAD, st * FLASH_STRIP:(st + 1) * FLASH_STRIP])

    def key_block(kb, diag_off):
        scores = [score(kb, hh, st) for hh, st in strips]
        retire(jnp.maximum(kb - 1, 0))
        for (hh, st), s in zip(strips, scores):
            cols = slice(st * FLASH_STRIP, (st + 1) * FLASH_STRIP)
            if diag_off is not None:
                key = lax.broadcasted_iota(jnp.int32, s.shape, 0) + diag_off
                qry = lax.broadcasted_iota(jnp.int32, s.shape, 1) + st * FLASH_STRIP
                s = jnp.where(key <= qry, s, NEG_BIG)
            m_old = m_scr[hh, :, cols]
            m_new = jnp.maximum(m_old, jnp.max(s, axis=0, keepdims=True))
            alpha = jnp.exp2(m_old - m_new)
            m_scr[hh, :, cols] = m_new
            p_scr[hh, :, cols] = jnp.exp2(s - m_new).astype(BF16)
            alpha_scr[hh, :, cols] = alpha

    def full_tile(j, carry):
        for d in range(sub):
            key_block(j * sub + d, None)
        return carry

    lax.fori_loop(0, qi, full_tile, 0)
    for d in range(sub):
        key_block(qi * sub + d, d * tkb)
    retire(qi * sub + sub - 1)
    for hh in range(2):
        acc = acc_scr[hh]
        ot_ref[hh * MLA_V:(hh + 1) * MLA_V, :] = (acc[:MLA_V] / acc[MLA_V:MLA_V + 1]).astype(ot_ref.dtype)


def flash_attention(qt, k, vt, *, b, t, tq):
    tkb = vt.shape[2]
    nq = t // tq
    nkb = t // tkb
    assert tq % tkb == 0 and t % tq == 0
    n_hp = MLA_HEADS // 2
    return pl.pallas_call(
        functools.partial(_flash_kernel, tq=tq, tkb=tkb),
        grid=(b, n_hp, nq),
        in_specs=[pl.BlockSpec((2 * HEAD_PAD, tq), lambda bi, hp, qi: (hp, bi * nq + qi)),
                  pl.BlockSpec((t, 2 * HEAD_PAD), lambda bi, hp, qi: (bi, hp)),
                  pl.BlockSpec((nkb, 2 * V_ROWS, tkb), lambda bi, hp, qi: (bi, hp, 0))],
        out_specs=pl.BlockSpec((2 * MLA_V, tq), lambda bi, hp, qi: (hp, bi * nq + qi)),
        out_shape=jax.ShapeDtypeStruct((MLA_HEADS * MLA_V, b * t), BF16),
        scratch_shapes=[pltpu.VMEM((2, 1, tq), F32),
                        pltpu.VMEM((2, V_ROWS, tq), F32), pltpu.VMEM((2, tkb, tq), BF16),
                        pltpu.VMEM((2, 1, tq), F32)],
        compiler_params=_cparams("parallel", "parallel", "arbitrary"),
        name="mla_flash",
    )(qt, k, vt)


PAGE = 128
DEC_PB = 32
DEC_SUB = 2048
NEW_PAD = 8


def _q_dec_kernel(qt_ref, w_ref, o_ref):
    for h in range(MLA_HEADS):
        o_ref[:, h * Q_DEC:(h + 1) * Q_DEC] = _dot_tn(qt_ref[h * HEAD_PAD:(h + 1) * HEAD_PAD, :], w_ref[h]).astype(o_ref.dtype)


def mla_q_dec(qt, w_dec):
    n = qt.shape[1]
    return pl.pallas_call(
        _q_dec_kernel,
        grid=(1,),
        in_specs=[pl.BlockSpec(qt.shape, lambda i: (0, 0)), pl.BlockSpec(w_dec.shape, lambda i: (0, 0, 0))],
        out_specs=pl.BlockSpec((n, MLA_HEADS * Q_DEC), lambda i: (0, 0)),
        out_shape=jax.ShapeDtypeStruct((n, MLA_HEADS * Q_DEC), BF16),
        compiler_params=_cparams("arbitrary"),
        name="mla_q_dec",
    )(qt, w_dec)


def _decode_kernel(pt_ref, q_ref, cnew_ref, rnew_ref, cache_c, cache_rt, o_ref, cbuf, rbuf, sem,
                   *, layer, n_pages, t_new, n_seq):
    b = pl.program_id(0)
    n_blk = n_pages // DEC_PB
    rows = q_ref.shape[0]

    def page_copies(pg, slot, i):
        off = pl.ds(pl.multiple_of(i * PAGE, PAGE), PAGE)
        cc = pltpu.make_async_copy(cache_c.at[layer, pg], cbuf.at[slot, off], sem.at[0, slot])
        cr = pltpu.make_async_copy(cache_rt.at[layer, pg], rbuf.at[slot, :, off], sem.at[1, slot])
        return cc, cr

    def start_block(seq, j, slot):
        def body(i, carry):
            cc, cr = page_copies(pt_ref[seq, j * DEC_PB + i], slot, i)
            cc.start()
            cr.start()
            return carry
        lax.fori_loop(0, DEC_PB, body, 0, unroll=8)

    def wait_block(slot):
        def body(i, carry):
            cc, cr = page_copies(0, slot, 0)
            cc.wait()
            cr.wait()
            return carry
        lax.fori_loop(0, DEC_PB, body, 0, unroll=8)

    q = q_ref[...]
    ql = q[:, :MLA_LORA]
    qr = q[:, MLA_LORA:MLA_LORA + MLA_ROPE]

    @pl.when(b == 0)
    def _():
        start_block(0, 0, 0)

    m = jnp.full((rows, 1), NEG_BIG, F32)
    l = jnp.zeros((rows, 1), F32)
    acc = jnp.zeros((rows, MLA_LORA), F32)
    pending = None
    for j in range(n_blk):
        slot = j % 2
        if j + 1 < n_blk:
            start_block(b, j + 1, 1 - slot)
        else:
            @pl.when(b + 1 < n_seq)
            def _():
                start_block(b + 1, 0, 0)
        wait_block(slot)
        for u in range(DEC_PB * PAGE // DEC_SUB):
            keys = slice(u * DEC_SUB, (u + 1) * DEC_SUB)
            kc = cbuf[slot, keys, :].astype(BF16)
            krt = rbuf[slot, :, keys].astype(BF16)
            s = _dot_nt(ql, kc) + _dot(qr, krt)
            if pending is not None:
                p_prev, alpha_prev, kc_prev = pending
                acc = alpha_prev * acc + _dot(p_prev, kc_prev)
            m_new = jnp.maximum(m, jnp.max(s, axis=-1, keepdims=True))
            alpha = jnp.exp2(m - m_new)
            p = jnp.exp2(s - m_new)
            l = alpha * l + jnp.sum(p, axis=-1, keepdims=True)
            m = m_new
            pending = (p.astype(BF16), alpha, kc)

    cn = cnew_ref[...].astype(BF16)
    rn = rnew_ref[...].astype(BF16)
    s = _dot_nt(ql, cn) + _dot_nt(qr, rn)
    p_prev, alpha_prev, kc_prev = pending
    acc = alpha_prev * acc + _dot(p_prev, kc_prev)
    t_row = lax.broadcasted_iota(jnp.int32, s.shape, 0) // MLA_HEADS
    col = lax.broadcasted_iota(jnp.int32, s.shape, 1)
    s = jnp.where((col <= t_row) & (col < t_new), s, NEG_BIG)
    m_new = jnp.maximum(m, jnp.max(s, axis=-1, keepdims=True))
    alpha = jnp.exp2(m - m_new)
    p = jnp.exp2(s - m_new)
    l = alpha * l + jnp.sum(p, axis=-1, keepdims=True)
    acc = alpha * acc + _dot(p.astype(BF16), cn)
    o_ref[...] = (acc / l).astype(o_ref.dtype)


def mla_decode(page_table, q_dec, c_new, r_new, cache_ckv, cache_krope_t, *, layer, t_new):
    b, rows, _ = q_dec.shape
    n_pages = page_table.shape[1]
    assert n_pages % (2 * DEC_PB) == 0
    grid_spec = pltpu.PrefetchScalarGridSpec(
        num_scalar_prefetch=1,
        grid=(b,),
        in_specs=[pl.BlockSpec((None, rows, Q_DEC), lambda i, pt: (i, 0, 0)),
                  pl.BlockSpec((None, NEW_PAD, MLA_LORA), lambda i, pt: (i, 0, 0)),
                  pl.BlockSpec((None, NEW_PAD, MLA_ROPE), lambda i, pt: (i, 0, 0)),
                  pl.BlockSpec(memory_space=pl.ANY),
                  pl.BlockSpec(memory_space=pl.ANY)],
        out_specs=pl.BlockSpec((None, rows, MLA_LORA), lambda i, pt: (i, 0, 0)),
        scratch_shapes=[pltpu.VMEM((2, DEC_PB * PAGE, MLA_LORA), F32),
                        pltpu.VMEM((2, MLA_ROPE, DEC_PB * PAGE), F32),
                        pltpu.SemaphoreType.DMA((2, 2))],
    )
    return pl.pallas_call(
        functools.partial(_decode_kernel, layer=layer, n_pages=n_pages, t_new=t_new, n_seq=b),
        grid_spec=grid_spec,
        out_shape=jax.ShapeDtypeStruct((b, rows, MLA_LORA), BF16),
        compiler_params=_cparams("arbitrary"),
        name="mla_decode",
    )(page_table, q_dec, c_new, r_new, cache_ckv, cache_krope_t)


def _sample_v_kernel(o_ref, wt_ref, yt_ref):
    for h in range(MLA_HEADS):
        yt_ref[h * MLA_V:(h + 1) * MLA_V, :] = _dot_nt(wt_ref[h * MLA_V:(h + 1) * MLA_V, :],
                                                       o_ref[:, h * MLA_LORA:(h + 1) * MLA_LORA]).astype(yt_ref.dtype)


def mla_sample_v(o_lat, wuv_t):
    n = o_lat.shape[0]
    return pl.pallas_call(
        _sample_v_kernel,
        grid=(1,),
        in_specs=[pl.BlockSpec(o_lat.shape, lambda i: (0, 0)), pl.BlockSpec(wuv_t.shape, lambda i: (0, 0))],
        out_specs=pl.BlockSpec((MLA_HEADS * MLA_V, n), lambda i: (0, 0)),
        out_shape=jax.ShapeDtypeStruct((MLA_HEADS * MLA_V, n), BF16),
        compiler_params=_cparams("arbitrary"),
        name="mla_sample_v",
    )(o_lat, wuv_t)


HG_CHUNK = 64
SSD_CHUNK = 128
SAMPLE_PAD = 8
SAMPLE_SEQS = 4
FLASH_QUERIES = 512
FLASH_KEYS = 512
FLASH_STRIP = 256
ROW_TILE = 256
WIDE_TILE = 512


def _ab_layer(x, n_p, bp, tp, bs, ts, norm_w, lb, st_hg, st_ssm, st_conv, w_in, w_out, hg_norm, conv_w, conv_b,
              dt_bias, a_log, d_skip, ssm_norm):
    sizes = [HG_W, HG_W, HG_W, HG_W, SSM_INNER, CONV_DIM, SSM_HEADS]
    offs = np.concatenate([[0], np.cumsum(sizes)])
    w_pad = jnp.pad(w_in, ((0, 0), (0, LANES - SSM_HEADS))).astype(BF16)
    splits = [(int(offs[j]), int(offs[j + 1])) for j in range(6)] + [(int(offs[6]), int(offs[6]) + LANES)]
    q, f, i_in, g, z, xbc, dt = norm_matmul(x, norm_w, w_pad, splits, [F32] * 7, tm=ROW_TILE)

    def grp(a, prompt):
        if prompt:
            return a
        a = a[n_p:].reshape(bs, ts, a.shape[1])
        return jnp.pad(a, ((0, 0), (0, SAMPLE_PAD - ts), (0, 0))).reshape(bs * SAMPLE_PAD, a.shape[2])

    outs = []
    for prompt in (True, False):
        if prompt:
            b_, t_, c_hg, c_ssd, tv_hg, tv_ssd = bp, tp, HG_CHUNK, SSD_CHUNK, HG_CHUNK, SSD_CHUNK
            s_hg = jnp.zeros((bp, HG_HEADS, HG_D, HG_D), F32)
            s_ssm = jnp.zeros((bp, SSM_HEADS, SSM_P, SSM_N), F32)
            s_conv = jnp.zeros((bp, CONV_W - 1, CONV_DIM), F32)
        else:
            b_, t_, c_hg, c_ssd, tv_hg, tv_ssd = bs, SAMPLE_PAD, SAMPLE_PAD, SAMPLE_PAD, ts, ts
            s_hg, s_ssm, s_conv = st_hg, st_ssm, st_conv
        n_seq = 1 if prompt else math.gcd(bs, SAMPLE_SEQS)
        o_hg, hg_new = hgrn_scan(grp(q, prompt), grp(f, prompt), grp(i_in, prompt), grp(g, prompt), lb, hg_norm,
                                 s_hg, b=b_, t=t_, c=c_hg, t_valid=tv_hg, n_seq=n_seq)
        y, conv_new, ssm_new = ssd_scan(grp(z, prompt), grp(xbc, prompt), grp(dt, prompt), s_conv, s_ssm,
                                        conv_w, conv_b, dt_bias, a_log, d_skip, ssm_norm,
                                        b=b_, t=t_, c=c_ssd, t_valid=tv_ssd, n_seq=n_seq)
        if not prompt:
            o_hg = o_hg.reshape(bs, SAMPLE_PAD, HG_W)[:, :ts].reshape(bs * ts, HG_W)
            y = y.reshape(bs, SAMPLE_PAD, SSM_INNER)[:, :ts].reshape(bs * ts, SSM_INNER)
        outs.append((o_hg, y, hg_new, ssm_new, conv_new))
    o_hg = (outs[0][0], outs[1][0])
    y = (outs[0][1], outs[1][1])
    w_out_b = w_out.astype(BF16)
    x = matmul_residual(x, [(o_hg, w_out_b[:HG_W], False), (y, w_out_b[HG_W:], False)], tm=WIDE_TILE)
    return x, outs[0][2:], outs[1][2:]


def _mla_layer(x, n_p, bp, tp, bs, ts, past_len, norm_w, cache_ckv, cache_krope, page_table, layer_c,
               w_in, q_norm, kv_norm, w_uq, w_uk, w_uv, w_out):
    w_in_p, wq_a, wq_b, wuk_p, wuv, w_dec = _mla_weights(w_in, w_uq, w_uk, w_uv)
    wuv_t = wuv.T
    splits = [(0, MLA_LORA), (MLA_LORA, 2 * MLA_LORA), (2 * MLA_LORA, 2 * MLA_LORA + HEAD_PAD),
              (2 * MLA_LORA + HEAD_PAD, 2 * MLA_LORA + 2 * HEAD_PAD)]
    cq, ckv, kr_a, kr_b = norm_matmul(x, norm_w, w_in_p, splits, [F32] * 4, tm=WIDE_TILE)
    pos = jnp.concatenate([jnp.tile(jnp.arange(tp, dtype=jnp.int32), bp),
                           jnp.tile(past_len + jnp.arange(ts, dtype=jnp.int32), bs)])
    ctab, stab = _rope_tables(pos)
    qt, ckvn, krot = mla_q(cq, ckv, kr_a, kr_b, ctab, stab, q_norm, kv_norm, wq_a.T, wq_b.T, tm=WIDE_TILE)
    krope = krot[:, ROPE_AT:ROPE_AT + MLA_ROPE]
    k_p, vt_p = mla_kv(ckvn, krot, wuk_p, wuv_t, n=n_p, tm=math.gcd(tp, FLASH_KEYS))
    ot_p = flash_attention(qt, k_p, vt_p, b=bp, t=tp, tq=math.gcd(tp, FLASH_QUERIES))
    q_dec = mla_q_dec(qt[:, n_p:], w_dec).reshape(bs, ts * MLA_HEADS, Q_DEC)
    c_new = jnp.pad(ckvn[n_p:].reshape(bs, ts, MLA_LORA), ((0, 0), (0, NEW_PAD - ts), (0, 0)))
    r_new = jnp.pad(krope[n_p:].reshape(bs, ts, MLA_ROPE), ((0, 0), (0, NEW_PAD - ts), (0, 0)))
    cache_krope_t = jnp.swapaxes(cache_krope, 2, 3)
    o_lat = mla_decode(page_table, q_dec, c_new, r_new, cache_ckv, cache_krope_t, layer=layer_c, t_new=ts)
    ot_s = mla_sample_v(o_lat.reshape(bs * ts, MLA_HEADS * MLA_LORA), wuv_t)
    x = matmul_residual(x, [((ot_p, ot_s), w_out.astype(BF16), True)], tm=WIDE_TILE)
    return x, (ckvn[:n_p].reshape(bp, tp, MLA_LORA), krope[:n_p].reshape(bp, tp, MLA_ROPE)), \
        (ckvn[n_p:].reshape(bs, ts, MLA_LORA), krope[n_p:].reshape(bs, ts, MLA_ROPE))


def kernel(x_prompt, x_sample, state_hgrn, state_ssm, state_conv, cache_ckv, cache_krope, page_table,
           norm_mix, norm_ffn, norm_final, w_in_ab, w_out_ab, hgrn_lb, hgrn_norm, conv_w, conv_b,
           dt_bias, a_log, d_skip, ssm_norm, w_in_c, q_norm, kv_norm, w_uq, w_uk, w_uv, w_out_c,
           w_route_group, b_route_group, w_route_expert, b_route_expert, w_gate, w_up, w_down):
    bp, tp, d = x_prompt.shape
    bs, ts, _ = x_sample.shape
    n_p = bp * tp
    depth = norm_mix.shape[0]
    n_a = w_in_ab.shape[0]
    past_len = page_table.shape[1] * cache_ckv.shape[2]
    lb_all = jnp.cumsum(jax.nn.softmax(hgrn_lb.astype(F32), axis=0), axis=0)[:n_a]
    x = (x_prompt.reshape(n_p, d), x_sample.reshape(bs * ts, d))
    a_p, a_s, c_p, c_s = [], [], [], []
    for layer in range(depth):
        j = layer // 2
        if layer % 2 == 0:
            x, sp, ss = _ab_layer(x, n_p, bp, tp, bs, ts, norm_mix[layer], lb_all[j], state_hgrn[j], state_ssm[j],
                                  state_conv[j], w_in_ab[j], w_out_ab[j], hgrn_norm[j], conv_w[j], conv_b[j],
                                  dt_bias[j], a_log[j], d_skip[j], ssm_norm[j])
            a_p.append(sp)
            a_s.append(ss)
        else:
            x, cp, cs = _mla_layer(x, n_p, bp, tp, bs, ts, past_len, norm_mix[layer], cache_ckv, cache_krope,
                                   page_table, j, w_in_c[j], q_norm[j], kv_norm[j], w_uq[j], w_uk[j], w_uv[j],
                                   w_out_c[j])
            c_p.append(cp)
            c_s.append(cs)
        x = hier_moe_block(x, norm_ffn[layer], w_route_group[layer], b_route_group[layer], w_route_expert[layer],
                           b_route_expert[layer], w_gate, w_up, w_down, layer, norm_final,
                           final_norm=(layer == depth - 1), head_rows=(n_p if layer == depth - 1 else None))
    y_p, y_s = x

    def stack(items, k):
        return jnp.stack([it[k] for it in items])

    return (y_p.reshape(bp, tp, d), y_s.reshape(bs, ts, d),
            stack(a_p, 0), stack(a_s, 0), stack(a_p, 1), stack(a_s, 1), stack(a_p, 2), stack(a_s, 2),
            stack(c_p, 0), stack(c_s, 0), stack(c_p, 1), stack(c_s, 1))
```

```python
import functools
import math

import jax
import jax.numpy as jnp
import numpy as np
from jax import lax
from jax.experimental import pallas as pl
from jax.experimental.pallas import tpu as pltpu

F32 = jnp.float32
BF16 = jnp.bfloat16

EPS = 1e-6
D_MODEL = 1024
HG_HEADS = 4
HG_D = 128
HG_W = HG_HEADS * HG_D
SSM_HEADS = 16
SSM_P = 64
SSM_N = 128
SSM_GROUPS = 2
SSM_INNER = SSM_HEADS * SSM_P
CONV_W = 4
CONV_DIM = SSM_INNER + 2 * SSM_GROUPS * SSM_N
LANES = 128
SUBLANES = 8
VMEM_LIMIT = 48 * 1024 * 1024
NEG_BIG = -1e30


def _cparams(*sem):
    return pltpu.CompilerParams(dimension_semantics=sem, vmem_limit_bytes=VMEM_LIMIT)


def _dot(a, b):
    return jnp.dot(a, b, preferred_element_type=F32)


def _dot_nt(a, b):
    return lax.dot_general(a, b, (((1,), (1,)), ((), ())), preferred_element_type=F32)


def _dot_tn(a, b):
    return lax.dot_general(a, b, (((0,), (0,)), ((), ())), preferred_element_type=F32)


def _split3(x):
    hi = x.astype(BF16)
    r1 = x - hi.astype(F32)
    mid = r1.astype(BF16)
    lo = (r1 - mid.astype(F32)).astype(BF16)
    return hi, mid, lo


def _silu(x):
    return x * (1.0 / (1.0 + jnp.exp(-x)))


def _sigmoid(x):
    return 1.0 / (1.0 + jnp.exp(-x))


def _as_parts(a):
    return tuple(a) if isinstance(a, (tuple, list)) else (a,)


def _part_specs(parts, tm, transposed=False):
    specs, bounds, start = [], [], 0
    for p in parts:
        n_t = (p.shape[1] if transposed else p.shape[0]) // tm

        def imap(i, lo=start, n_t=n_t):
            j = jnp.clip(i - lo, 0, n_t - 1)
            return (0, j) if transposed else (j, 0)

        specs.append(pl.BlockSpec((p.shape[0], tm) if transposed else (tm, p.shape[1]), imap))
        start += n_t
        bounds.append(start)
    return specs, tuple(bounds)


def _pick_part(i, bounds, refs):
    val = refs[-1][...]
    for k in reversed(range(len(refs) - 1)):
        val = jnp.where(i < bounds[k], refs[k][...], val)
    return val


def _norm_matmul_kernel(*refs, x_bounds, splits, normalize):
    n_x = len(x_bounds)
    g_ref, w_ref = refs[n_x], refs[n_x + 1]
    out_refs = refs[n_x + 2:]
    x = _pick_part(pl.program_id(0), x_bounds, refs[:n_x]).astype(F32)
    if normalize:
        ms = jnp.mean(x * x, axis=-1, keepdims=True)
        x = x * lax.rsqrt(ms + EPS) * g_ref[...]
    h = x.astype(BF16)
    for (a, b), o_ref in zip(splits, out_refs):
        o_ref[...] = _dot(h, w_ref[:, a:b]).astype(o_ref.dtype)


def norm_matmul(x, gain, w, splits, out_dtypes, *, tm, normalize=True):
    x_parts = _as_parts(x)
    k = x_parts[0].shape[1]
    x_specs, x_bounds = _part_specs(x_parts, tm)
    n = x_bounds[-1] * tm
    assert all(p.shape[0] % tm == 0 for p in x_parts)
    kern = functools.partial(_norm_matmul_kernel, x_bounds=x_bounds, splits=tuple(splits), normalize=normalize)
    out_shape = [jax.ShapeDtypeStruct((n, b - a), dt) for (a, b), dt in zip(splits, out_dtypes)]
    out_specs = [pl.BlockSpec((tm, b - a), lambda i: (i, 0)) for (a, b) in splits]
    return pl.pallas_call(
        kern,
        grid=(n // tm,),
        in_specs=x_specs + [pl.BlockSpec((1, k), lambda i: (0, 0)),
                            pl.BlockSpec(w.shape, lambda i: (0, 0))],
        out_specs=out_specs,
        out_shape=out_shape,
        compiler_params=_cparams("parallel"),
        name="norm_matmul",
    )(*x_parts, gain.reshape(1, k), w)


def _matmul_residual_kernel(*refs, res_bounds, a_bounds, transposed):
    i = pl.program_id(0)
    pos = len(res_bounds)
    acc = _pick_part(i, res_bounds, refs[:pos])
    for bounds, tr in zip(a_bounds, transposed):
        a = _pick_part(i, bounds, refs[pos:pos + len(bounds)]).astype(BF16)
        w_ref = refs[pos + len(bounds)]
        pos += len(bounds) + 1
        acc = acc + (_dot_tn(a, w_ref[...]) if tr else _dot(a, w_ref[...]))
    refs[pos][...] = acc


def matmul_residual(res, pairs, *, tm):
    res_parts = _as_parts(res)
    d = res_parts[0].shape[1]
    in_specs, res_bounds = _part_specs(res_parts, tm)
    n = res_bounds[-1] * tm
    args = list(res_parts)
    a_bounds = []
    for a, w, tr in pairs:
        a_parts = _as_parts(a)
        specs, bounds = _part_specs(a_parts, tm, transposed=tr)
        assert bounds[-1] == res_bounds[-1]
        in_specs += specs + [pl.BlockSpec(w.shape, lambda i: (0, 0))]
        args += list(a_parts) + [w]
        a_bounds.append(bounds)
    kern = functools.partial(_matmul_residual_kernel, res_bounds=res_bounds, a_bounds=tuple(a_bounds),
                             transposed=tuple(bool(p[2]) for p in pairs))
    return pl.pallas_call(
        kern,
        grid=(n // tm,),
        in_specs=in_specs,
        out_specs=pl.BlockSpec((tm, d), lambda i: (i, 0)),
        out_shape=jax.ShapeDtypeStruct((n, d), F32),
        compiler_params=_cparams("parallel"),
        name="matmul_residual",
    )(*args)


def _hgrn_level_halves(c):
    halves = []
    b = c // 2
    while b >= 1:
        halves.append(b)
        b //= 2
    return halves


@functools.lru_cache(maxsize=None)
def _hgrn_consts(c):
    t = np.arange(c)
    rows = [t[None, :] <= t[:, None], t[None, :] > t[:, None]]
    masks = []
    for b in _hgrn_level_halves(c):
        blk = t // b
        st = blk * b
        en = st + b - 1
        odd = blk % 2 == 1
        even = ~odd
        rows.append(odd[:, None] & (t[None, :] >= st[:, None]) & (t[None, :] <= t[:, None]))
        rows.append(even[:, None] & (t[None, :] > t[:, None]) & (t[None, :] <= en[:, None]))
        masks.append((t[:, None] // (2 * b) == t[None, :] // (2 * b)) & odd[:, None] & even[None, :])
    masks.append(np.eye(c, dtype=bool))
    rows.append(np.ones((SUBLANES, c), dtype=bool))
    dg = np.concatenate(rows, axis=0).astype(np.float32)
    dg3 = np.concatenate([dg, dg, dg], axis=1)
    mk = np.stack(masks).astype(np.float32)
    return dg3, mk


def _hgrn_kernel(q_ref, f_ref, i_ref, g_ref, lb_ref, nw_ref, dg_ref, mk_ref, s0_ref,
                 o_ref, sfin_ref, s_scr, *, c, t_valid, n_chunks, n_seq):
    ci = pl.program_id(1)

    @pl.when(ci == 0)
    def _():
        s_scr[...] = s0_ref[...]

    for si in range(n_seq):
        rows = slice(si * c, (si + 1) * c)
        _hgrn_chunk(q_ref.at[rows], f_ref.at[rows], i_ref.at[rows], g_ref.at[rows], lb_ref, nw_ref, dg_ref, mk_ref,
                    o_ref.at[rows], s_scr.at[si], c=c, t_valid=t_valid)

    @pl.when(ci == n_chunks - 1)
    def _():
        sfin_ref[...] = s_scr[...]


def _hgrn_chunk(q_ref, f_ref, i_ref, g_ref, lb_ref, nw_ref, dg_ref, mk_ref, o_ref, s_scr, *, c, t_valid):
    lb = lb_ref[...]
    fl = f_ref[...]
    sig = _sigmoid(fl)
    logf = jnp.log(lb + (1.0 - lb) * sig)
    kk = (1.0 - lb) * (1.0 - sig)
    if t_valid < c:
        row = lax.broadcasted_iota(jnp.int32, (c, HG_W), 0)
        live = row < t_valid
        logf = jnp.where(live, logf, 0.0)
        kk = jnp.where(live, kk, 0.0)
    qa = _silu(q_ref[...])
    vv = i_ref[...].astype(BF16)

    lf3 = jnp.concatenate(_split3(logf), axis=0)
    ex = jnp.exp(_dot(dg_ref[...], lf3))
    halves = _hgrn_level_halves(c)
    n_lv = len(halves)
    e_cum = ex[0:c]
    e_rev = ex[c:2 * c]
    q_state = (qa * e_cum).astype(BF16)
    k_state = (kk * e_rev).astype(BF16)
    q_lv = [(qa * ex[(2 + 2 * l) * c:(3 + 2 * l) * c]).astype(BF16) for l in range(n_lv)]
    k_lv = [(kk * ex[(3 + 2 * l) * c:(4 + 2 * l) * c]).astype(BF16) for l in range(n_lv)]
    q_lv.append(qa.astype(BF16))
    k_lv.append(kk.astype(BF16))
    ones_t = jnp.ones((3 * c, HG_D), BF16)
    gate = _silu(g_ref[...])
    nw = nw_ref[...]

    for h in range(HG_HEADS):
        sl = slice(h * HG_D, (h + 1) * HG_D)
        sc = jnp.zeros((c, c), F32)
        for l in range(n_lv + 1):
            sc = sc + mk_ref[l] * _dot_nt(q_lv[l][:, sl], k_lv[l][:, sl])
        s_h = s_scr[h]
        o_h = _dot(sc.astype(BF16), vv[:, sl]) + _dot(q_state[:, sl], s_h.astype(BF16))
        dec = jnp.exp(_dot_tn(lf3[:, sl], ones_t))
        s_scr[h] = s_h * dec + _dot_tn(k_state[:, sl], vv[:, sl])
        ms = jnp.mean(o_h * o_h, axis=-1, keepdims=True)
        o_h = o_h * lax.rsqrt(ms + EPS) * nw[:, sl] * gate[:, sl]
        o_ref[:, sl] = o_h.astype(o_ref.dtype)


def hgrn_scan(q, f, i_in, g, lb, norm_w, s0, *, b, t, c, t_valid, n_seq=1):
    n_chunks = t // c
    assert t % c == 0 and (t_valid == c or n_chunks == 1) and q.shape[0] >= b * t
    assert b % n_seq == 0 and (n_seq == 1 or n_chunks == 1)
    dg3, mk = _hgrn_consts(c)
    dg3 = jnp.asarray(dg3, BF16)
    mk = jnp.asarray(mk, F32)
    tok = pl.BlockSpec((n_seq * c, HG_W), lambda bi, ci: (bi * n_chunks + ci, 0))
    st = pl.BlockSpec((n_seq, HG_HEADS, HG_D, HG_D), lambda bi, ci: (bi, 0, 0, 0))
    row = pl.BlockSpec((1, HG_W), lambda bi, ci: (0, 0))
    kern = functools.partial(_hgrn_kernel, c=c, t_valid=t_valid, n_chunks=n_chunks, n_seq=n_seq)
    return pl.pallas_call(
        kern,
        grid=(b // n_seq, n_chunks),
        in_specs=[tok, tok, tok, tok, row, row,
                  pl.BlockSpec(dg3.shape, lambda bi, ci: (0, 0)),
                  pl.BlockSpec(mk.shape, lambda bi, ci: (0, 0, 0)),
                  st],
        out_specs=[tok, st],
        out_shape=[jax.ShapeDtypeStruct((b * t, HG_W), BF16),
                   jax.ShapeDtypeStruct((b, HG_HEADS, HG_D, HG_D), F32)],
        scratch_shapes=[pltpu.VMEM((n_seq, HG_HEADS, HG_D, HG_D), F32)],
        compiler_params=_cparams("parallel", "arbitrary"),
        name="hgrn_scan",
    )(q, f, i_in, g, lb.reshape(1, HG_W), norm_w.reshape(1, HG_W), dg3, mk, s0)


CONV_HEAD = 8


@functools.lru_cache(maxsize=None)
def _ssd_consts(c):
    t = np.arange(c)
    tri = (t[None, :] <= t[:, None]).astype(np.float32)
    rev = (t[None, :] > t[:, None]).astype(np.float32)
    ones = np.ones((SUBLANES, c), np.float32)
    tg = np.concatenate([tri, rev, ones], axis=0)
    tg3 = np.concatenate([tg, tg, tg], axis=1)
    u3 = np.concatenate([tri.T, tri.T, tri.T], axis=0)
    return tg3, u3


def _ssd_kernel(z_ref, xbc_ref, dt_ref, cbuf_ref, cw_ref, cb_ref, dtb_ref, alog_ref, dsk_ref, nw_ref,
                tg_ref, u_ref, h0_ref,
                y_ref, cnew_ref, hfin_ref, xp_scr, h_scr, y_scr, *, c, t_valid, n_chunks, n_seq):
    ci = pl.program_id(1)

    @pl.when(ci == 0)
    def _():
        h_scr[...] = h0_ref[...]
        xp_scr[:, CONV_HEAD - (CONV_W - 1):CONV_HEAD, :] = cbuf_ref[...]

    for si in range(n_seq):
        rows = slice(si * c, (si + 1) * c)
        _ssd_chunk(ci, z_ref.at[rows], xbc_ref.at[rows], dt_ref.at[rows], cw_ref, cb_ref, dtb_ref, alog_ref, dsk_ref,
                   nw_ref, tg_ref, u_ref, y_ref.at[rows], cnew_ref.at[si], xp_scr.at[si], h_scr.at[si], y_scr.at[si],
                   c=c, t_valid=t_valid, n_chunks=n_chunks)

    @pl.when(ci == n_chunks - 1)
    def _():
        hfin_ref[...] = h_scr[...]


def _ssd_chunk(ci, z_ref, xbc_ref, dt_ref, cw_ref, cb_ref, dtb_ref, alog_ref, dsk_ref, nw_ref, tg_ref, u_ref,
               y_ref, cnew_ref, xp_scr, h_scr, y_scr, *, c, t_valid, n_chunks):
    xp_scr[CONV_HEAD:CONV_HEAD + c, :] = xbc_ref[...]
    conv = cb_ref[...]
    for i in range(CONV_W):
        off = CONV_HEAD - (CONV_W - 1) + i
        conv = conv + xp_scr[off:off + c, :] * cw_ref[i:i + 1, :]
    conv = _silu(conv)

    @pl.when(ci == n_chunks - 1)
    def _():
        cnew_ref[...] = xp_scr[CONV_HEAD + t_valid - (CONV_W - 1):CONV_HEAD + t_valid, :]

    xp_scr[CONV_HEAD - (CONV_W - 1):CONV_HEAD, :] = xp_scr[CONV_HEAD + c - (CONV_W - 1):CONV_HEAD + c, :]

    xs = conv[:, :SSM_INNER]
    dt_raw = dt_ref[...] + dtb_ref[...]
    dt = jnp.maximum(dt_raw, 0.0) + jnp.log(1.0 + jnp.exp(-jnp.abs(dt_raw)))
    if t_valid < c:
        row = lax.broadcasted_iota(jnp.int32, (c, LANES), 0)
        dt = jnp.where(row < t_valid, dt, 0.0)
    a = -jnp.exp(alog_ref[...])
    da = dt * a
    da3 = jnp.concatenate(_split3(da), axis=0)
    xx = _dot(tg_ref[...], da3)
    cum = xx[0:c]
    e_cum = jnp.exp(cum)
    w_all = jnp.exp(xx[c:2 * c]) * dt
    e_last = jnp.exp(xx[2 * c:2 * c + 1])
    cum_t = _dot_tn(da3, u_ref[...])

    tril = lax.broadcasted_iota(jnp.int32, (c, c), 0) >= lax.broadcasted_iota(jnp.int32, (c, c), 1)
    heads_per_group = SSM_HEADS // SSM_GROUPS
    for g in range(SSM_GROUPS):
        bm = conv[:, SSM_INNER + g * SSM_N:SSM_INNER + (g + 1) * SSM_N]
        cm = conv[:, SSM_INNER + (SSM_GROUPS + g) * SSM_N:SSM_INNER + (SSM_GROUPS + g + 1) * SSM_N]
        bm_b = bm.astype(BF16)
        cb = _dot_nt(cm.astype(BF16), bm_b)
        for hh in range(heads_per_group):
            h = g * heads_per_group + hh
            x_h = xs[:, h * SSM_P:(h + 1) * SSM_P]
            diff = cum[:, h:h + 1] - cum_t[h:h + 1, :]
            seg = jnp.exp(jnp.where(tril, diff, NEG_BIG))
            xd = (dt[:, h:h + 1] * x_h).astype(BF16)
            cme = (cm * e_cum[:, h:h + 1]).astype(BF16)
            h_h = h_scr[h]
            y_h = _dot((cb * seg).astype(BF16), xd) + _dot_nt(cme, h_h.astype(BF16))
            xw = (w_all[:, h:h + 1] * x_h).astype(BF16)
            h_scr[h] = h_h * e_last[:, h:h + 1] + _dot_tn(xw, bm_b)
            y_scr[:, h * SSM_P:(h + 1) * SSM_P] = y_h

    y = (y_scr[...] + dsk_ref[...] * xs) * _silu(z_ref[...])
    gw = SSM_INNER // SSM_GROUPS
    for g in range(SSM_GROUPS):
        yg = y[:, g * gw:(g + 1) * gw]
        ms = jnp.mean(yg * yg, axis=-1, keepdims=True)
        y_ref[:, g * gw:(g + 1) * gw] = (yg * lax.rsqrt(ms + EPS) * nw_ref[:, g * gw:(g + 1) * gw]).astype(y_ref.dtype)


def ssd_scan(z, xbc, dt, conv_buf, h0, conv_w, conv_b, dt_bias, a_log, d_skip, norm_w, *, b, t, c, t_valid,
             n_seq=1):
    n_chunks = t // c
    assert t % c == 0 and (t_valid == c or n_chunks == 1) and z.shape[0] >= b * t
    assert b % n_seq == 0 and (n_seq == 1 or n_chunks == 1)
    tg3, u3 = _ssd_consts(c)
    tg3 = jnp.asarray(tg3, BF16)
    u3 = jnp.asarray(u3, BF16)

    def pad_heads(v):
        return jnp.pad(v.astype(F32), (0, LANES - SSM_HEADS)).reshape(1, LANES)

    def tok(w):
        return pl.BlockSpec((n_seq * c, w), lambda bi, ci: (bi * n_chunks + ci, 0))

    def const(shape):
        return pl.BlockSpec(shape, lambda bi, ci: (0,) * len(shape))

    cst = pl.BlockSpec((n_seq, CONV_W - 1, CONV_DIM), lambda bi, ci: (bi, 0, 0))
    hst = pl.BlockSpec((n_seq, SSM_HEADS, SSM_P, SSM_N), lambda bi, ci: (bi, 0, 0, 0))
    kern = functools.partial(_ssd_kernel, c=c, t_valid=t_valid, n_chunks=n_chunks, n_seq=n_seq)
    return pl.pallas_call(
        kern,
        grid=(b // n_seq, n_chunks),
        in_specs=[tok(SSM_INNER), tok(CONV_DIM), tok(LANES), cst,
                  const((CONV_W, CONV_DIM)), const((1, CONV_DIM)), const((1, LANES)), const((1, LANES)),
                  const((1, SSM_INNER)), const((1, SSM_INNER)), const(tg3.shape), const(u3.shape), hst],
        out_specs=[tok(SSM_INNER), cst, hst],
        out_shape=[jax.ShapeDtypeStruct((b * t, SSM_INNER), BF16),
                   jax.ShapeDtypeStruct((b, CONV_W - 1, CONV_DIM), F32),
                   jax.ShapeDtypeStruct((b, SSM_HEADS, SSM_P, SSM_N), F32)],
        scratch_shapes=[pltpu.VMEM((n_seq, CONV_HEAD + c, CONV_DIM), F32),
                        pltpu.VMEM((n_seq, SSM_HEADS, SSM_P, SSM_N), F32),
                        pltpu.VMEM((n_seq, c, SSM_INNER), F32)],
        compiler_params=_cparams("parallel", "arbitrary"),
        name="ssd_scan",
    )(z, xbc, dt, conv_buf, conv_w, conv_b.reshape(1, CONV_DIM), pad_heads(dt_bias), pad_heads(a_log),
      jnp.repeat(d_skip.astype(F32), SSM_P).reshape(1, SSM_INNER), norm_w.reshape(1, SSM_INNER), tg3, u3, h0)


N_GROUPS = 4
EXPERTS_PER_GROUP = 8
N_EXPERTS = N_GROUPS * EXPERTS_PER_GROUP
TOP_K = 2
D_EXPERT = 256
MOE_TM = 256


def _router_kernel(x_ref, g_ref, w_ref, b_ref, tri_ref, xn_ref, rw_ref, ri_ref, cnt_ref, carry_scr):
    @pl.when(pl.program_id(0) == 0)
    def _():
        carry_scr[...] = jnp.zeros_like(carry_scr)

    x = x_ref[...]
    ms = jnp.mean(x * x, axis=-1, keepdims=True)
    xn = x * lax.rsqrt(ms + EPS) * g_ref[...]
    for s in range(SUBLANES):
        xn_ref[pl.ds(s, x.shape[0], stride=SUBLANES), :] = xn[:, s * LANES:(s + 1) * LANES]
    logits = jnp.dot(xn, w_ref[...], precision=lax.Precision.HIGHEST, preferred_element_type=F32) + b_ref[...]
    lane = lax.broadcasted_iota(jnp.int32, logits.shape, 1)
    is_g = (lane >= N_EXPERTS) & (lane < N_EXPERTS + N_GROUPS)
    gl = jnp.where(is_g, logits, NEG_BIG)
    gmax = jnp.max(gl, axis=-1, keepdims=True)
    g_sel = jnp.min(jnp.where(gl == gmax, lane, 4 * LANES), axis=-1, keepdims=True) - N_EXPERTS
    g_w = 1.0 / jnp.sum(jnp.where(is_g, jnp.exp(gl - gmax), 0.0), axis=-1, keepdims=True)
    lo = g_sel * EXPERTS_PER_GROUP
    in_grp = (lane >= lo) & (lane < lo + EXPERTS_PER_GROUP)
    el = jnp.where(in_grp, logits, NEG_BIG)
    emax = jnp.max(el, axis=-1, keepdims=True)
    ee = jnp.where(in_grp, jnp.exp(el - emax), 0.0)
    p = ee / jnp.sum(ee, axis=-1, keepdims=True)
    p = jnp.where(in_grp, p, -1.0)
    p1 = jnp.max(p, axis=-1, keepdims=True)
    i1 = jnp.min(jnp.where(p == p1, lane, 4 * LANES), axis=-1, keepdims=True)
    p_rest = jnp.where(lane == i1, -1.0, p)
    p2 = jnp.max(p_rest, axis=-1, keepdims=True)
    i2 = jnp.min(jnp.where(p_rest == p2, lane, 4 * LANES), axis=-1, keepdims=True)
    w1 = p1 / (p1 + p2) * g_w
    w2 = p2 / (p1 + p2) * g_w
    rw_ref[...] = jnp.where(lane == 0, w1, jnp.where(lane == 1, w2, 0.0))
    hit1 = lane == i1
    hit2 = lane == i2
    onehot = (hit1 | hit2).astype(BF16)
    before = _dot(tri_ref[...], onehot) + carry_scr[0:1, :]
    r1 = jnp.sum(jnp.where(hit1, before, 0.0), axis=-1, keepdims=True).astype(jnp.int32)
    r2 = jnp.sum(jnp.where(hit2, before, 0.0), axis=-1, keepdims=True).astype(jnp.int32)
    ri_ref[...] = jnp.where(lane == 0, i1, jnp.where(lane == 1, i2, jnp.where(lane == 2, r1, jnp.where(lane == 3, r2, 0))))
    carry_scr[...] = carry_scr[...] + jnp.sum(onehot.astype(F32), axis=0, keepdims=True)
    cnt_ref[...] = carry_scr[...]


def moe_router(x, gain, w_rg, b_rg, w_re, b_re, *, tm):
    n, d = x.shape
    assert d == SUBLANES * LANES
    w = jnp.zeros((d, LANES), F32).at[:, :N_EXPERTS].set(w_re).at[:, N_EXPERTS:N_EXPERTS + N_GROUPS].set(w_rg)
    b = jnp.zeros((1, LANES), F32).at[0, :N_EXPERTS].set(b_re).at[0, N_EXPERTS:N_EXPERTS + N_GROUPS].set(b_rg)
    t = np.arange(tm)
    tri = jnp.asarray(t[None, :] < t[:, None], BF16)
    return pl.pallas_call(
        _router_kernel,
        grid=(n // tm,),
        in_specs=[pl.BlockSpec((tm, d), lambda i: (i, 0)),
                  pl.BlockSpec((1, d), lambda i: (0, 0)),
                  pl.BlockSpec((d, LANES), lambda i: (0, 0)),
                  pl.BlockSpec((1, LANES), lambda i: (0, 0)),
                  pl.BlockSpec((tm, tm), lambda i: (0, 0))],
        out_specs=[pl.BlockSpec((tm * SUBLANES, LANES), lambda i: (i, 0)),
                   pl.BlockSpec((tm, LANES), lambda i: (i, 0)),
                   pl.BlockSpec((tm, LANES), lambda i: (i, 0)),
                   pl.BlockSpec((SUBLANES, LANES), lambda i: (0, 0))],
        out_shape=[jax.ShapeDtypeStruct((n * SUBLANES, LANES), F32),
                   jax.ShapeDtypeStruct((n, LANES), F32),
                   jax.ShapeDtypeStruct((n, LANES), jnp.int32),
                   jax.ShapeDtypeStruct((SUBLANES, LANES), F32)],
        scratch_shapes=[pltpu.VMEM((SUBLANES, LANES), F32)],
        compiler_params=_cparams("arbitrary"),
        name="moe_router",
    )(x, gain.reshape(1, d), w, b, tri)


def _token_rows(idx):
    return pl.ds(pl.multiple_of(idx * SUBLANES, SUBLANES), SUBLANES)


def _moe_dispatch_kernel(dest_ref, xn_ref, zero_ref, xs_ref, sem, *, tm):
    del zero_ref
    base = pl.program_id(0) * (tm * TOP_K)

    def start(r, carry):
        for k in range(TOP_K):
            pltpu.make_async_copy(xn_ref.at[_token_rows(r)], xs_ref.at[_token_rows(dest_ref[base + r * TOP_K + k])],
                                  sem).start()
        return carry

    lax.fori_loop(0, tm, start, 0, unroll=8)

    def wait(r, carry):
        for k in range(TOP_K):
            pltpu.make_async_copy(xn_ref.at[_token_rows(0)], xs_ref.at[_token_rows(0)], sem).wait()
        return carry

    lax.fori_loop(0, tm, wait, 0, unroll=8)


def moe_dispatch(dest, xn, n_rows, *, tm):
    n = xn.shape[0] // SUBLANES
    grid_spec = pltpu.PrefetchScalarGridSpec(
        num_scalar_prefetch=1,
        grid=(n // tm,),
        in_specs=[pl.BlockSpec((tm * SUBLANES, LANES), lambda i, d: (i, 0)),
                  pl.BlockSpec(memory_space=pl.ANY)],
        out_specs=pl.BlockSpec(memory_space=pl.ANY),
        scratch_shapes=[pltpu.SemaphoreType.DMA(())],
    )
    return pl.pallas_call(
        functools.partial(_moe_dispatch_kernel, tm=tm),
        grid_spec=grid_spec,
        out_shape=jax.ShapeDtypeStruct((n_rows * SUBLANES, LANES), F32),
        input_output_aliases={2: 0},
        compiler_params=_cparams("arbitrary"),
        name="moe_dispatch",
    )(dest, xn, jnp.zeros((n_rows * SUBLANES, LANES), F32))


def _moe_ffn_kernel(te_ref, nu_ref, x_ref, wg_ref, wu_ref, wd_ref, y_ref):
    i = pl.program_id(0)

    @pl.when(i < nu_ref[0])
    def _():
        x = jnp.concatenate([x_ref[pl.ds(s, MOE_TM, stride=SUBLANES), :] for s in range(SUBLANES)], axis=1).astype(BF16)
        a = _dot(x, wg_ref[...].astype(BF16))
        b = _dot(x, wu_ref[...].astype(BF16))
        hid = (_silu(a) * b).astype(BF16)
        y = _dot(hid, wd_ref[...].astype(BF16))
        for s in range(SUBLANES):
            y_ref[pl.ds(s, MOE_TM, stride=SUBLANES), :] = y[:, s * LANES:(s + 1) * LANES]

    @pl.when(i >= nu_ref[0])
    def _():
        y_ref[...] = jnp.zeros_like(y_ref)


def moe_ffn(tile_expert, n_used, x_sorted, w_gate, w_up, w_down, layer):
    n_tiles = x_sorted.shape[0] // (MOE_TM * SUBLANES)
    d = w_gate.shape[2]
    tile = pl.BlockSpec((MOE_TM * SUBLANES, LANES), lambda i, te, nu: (i, 0))
    grid_spec = pltpu.PrefetchScalarGridSpec(
        num_scalar_prefetch=2,
        grid=(n_tiles,),
        in_specs=[tile,
                  pl.BlockSpec((None, None, d, D_EXPERT), lambda i, te, nu: (layer, te[i], 0, 0)),
                  pl.BlockSpec((None, None, d, D_EXPERT), lambda i, te, nu: (layer, te[i], 0, 0)),
                  pl.BlockSpec((None, None, D_EXPERT, d), lambda i, te, nu: (layer, te[i], 0, 0))],
        out_specs=tile,
    )
    return pl.pallas_call(
        _moe_ffn_kernel,
        grid_spec=grid_spec,
        out_shape=jax.ShapeDtypeStruct(x_sorted.shape, F32),
        compiler_params=_cparams("arbitrary"),
        name="moe_ffn",
    )(tile_expert, n_used, x_sorted, w_gate, w_up, w_down)


def _moe_combine_kernel(dest_ref, x_ref, rw_ref, fg_ref, ys_ref, *rest, tm, n_steps, final_norm, head_steps):
    out_refs, (ybuf, sem) = rest[:-2], rest[-2:]
    i = pl.program_id(0)
    slot = i % 2
    slot_rows = tm * TOP_K * SUBLANES

    def issue(step, to_slot):
        base = step * (tm * TOP_K)

        def body(r, carry):
            for k in range(TOP_K):
                dst = pl.ds(pl.multiple_of(to_slot * slot_rows + (r * TOP_K + k) * SUBLANES, SUBLANES), SUBLANES)
                pltpu.make_async_copy(ys_ref.at[_token_rows(dest_ref[base + r * TOP_K + k])], ybuf.at[dst],
                                      sem.at[to_slot]).start()
            return carry

        lax.fori_loop(0, tm, body, 0, unroll=8)

    @pl.when(i == 0)
    def _():
        issue(0, 0)

    @pl.when(i + 1 < n_steps)
    def _():
        issue(i + 1, 1 - slot)

    def wait(r, carry):
        for k in range(TOP_K):
            pltpu.make_async_copy(ys_ref.at[_token_rows(0)], ybuf.at[_token_rows(0)], sem.at[slot]).wait()
        return carry

    lax.fori_loop(0, tm, wait, 0, unroll=8)

    rw = rw_ref[...]
    g0 = rw[:, 0:1]
    g1 = rw[:, 1:2]
    first = pl.multiple_of(slot * slot_rows, SUBLANES)
    pieces = []
    for s in range(SUBLANES):
        y0 = ybuf[pl.ds(first + s, tm, stride=TOP_K * SUBLANES), :]
        y1 = ybuf[pl.ds(first + SUBLANES + s, tm, stride=TOP_K * SUBLANES), :]
        pieces.append(x_ref[:, s * LANES:(s + 1) * LANES] + (g0 * y0 + g1 * y1))
    if final_norm:
        sq = sum(jnp.sum(p * p, axis=-1, keepdims=True) for p in pieces)
        scale = lax.rsqrt(sq * (1.0 / (SUBLANES * LANES)) + EPS)
        pieces = [p * scale * fg_ref[:, s * LANES:(s + 1) * LANES] for s, p in enumerate(pieces)]
    def write(o_ref):
        for s, p in enumerate(pieces):
            o_ref[:, s * LANES:(s + 1) * LANES] = p

    if head_steps is None:
        write(out_refs[0])
    else:
        @pl.when(i < head_steps)
        def _():
            write(out_refs[0])

        @pl.when(i >= head_steps)
        def _():
            write(out_refs[1])


def moe_combine(dest, x, rw, y_sorted, final_gain, *, tm, final_norm, head_rows=None):
    n, d = x.shape
    n_steps = n // tm
    if head_rows is None:
        head_steps = None
        out_specs = [pl.BlockSpec((tm, d), lambda i, dd: (i, 0))]
        out_shape = [jax.ShapeDtypeStruct((n, d), F32)]
    else:
        assert head_rows % tm == 0 and 0 < head_rows < n
        head_steps = head_rows // tm
        out_specs = [pl.BlockSpec((tm, d), lambda i, dd: (jnp.minimum(i, head_steps - 1), 0)),
                     pl.BlockSpec((tm, d), lambda i, dd: (jnp.maximum(i - head_steps, 0), 0))]
        out_shape = [jax.ShapeDtypeStruct((head_rows, d), F32), jax.ShapeDtypeStruct((n - head_rows, d), F32)]
    grid_spec = pltpu.PrefetchScalarGridSpec(
        num_scalar_prefetch=1,
        grid=(n_steps,),
        in_specs=[pl.BlockSpec((tm, d), lambda i, dd: (i, 0)),
                  pl.BlockSpec((tm, LANES), lambda i, dd: (i, 0)),
                  pl.BlockSpec((1, d), lambda i, dd: (0, 0)),
                  pl.BlockSpec(memory_space=pl.ANY)],
        out_specs=out_specs,
        scratch_shapes=[pltpu.VMEM((2 * tm * TOP_K * SUBLANES, LANES), F32),
                        pltpu.SemaphoreType.DMA((2,))],
    )
    outs = pl.pallas_call(
        functools.partial(_moe_combine_kernel, tm=tm, n_steps=n_steps, final_norm=final_norm, head_steps=head_steps),
        grid_spec=grid_spec,
        out_shape=out_shape,
        compiler_params=_cparams("arbitrary"),
        name="moe_combine",
    )(dest, x, rw, final_gain.reshape(1, d), y_sorted)
    return outs[0] if head_rows is None else tuple(outs)


def _moe_plan(ri, counts, n):
    eid = ri[:, :TOP_K]
    rank = ri[:, TOP_K:2 * TOP_K]
    counts = counts[0, :N_EXPERTS].astype(jnp.int32)
    tiles = (counts + MOE_TM - 1) // MOE_TM
    tile_end = jnp.cumsum(tiles)
    tile_start = tile_end - tiles
    onehot = eid[:, :, None] == jnp.arange(N_EXPERTS, dtype=jnp.int32)[None, None, :]
    dest = jnp.sum(jnp.where(onehot, tile_start[None, None, :], 0), axis=-1) * MOE_TM + rank
    n_rows = TOP_K * n + N_EXPERTS * MOE_TM
    n_tiles = n_rows // MOE_TM
    n_used = tile_end[-1]
    t_idx = jnp.minimum(jnp.arange(n_tiles, dtype=jnp.int32), n_used - 1)
    tile_expert = jnp.sum((t_idx[:, None] >= tile_end[None, :]).astype(jnp.int32), axis=1)
    return dest.reshape(-1).astype(jnp.int32), tile_expert.astype(jnp.int32), n_used.reshape(1).astype(jnp.int32), n_rows


def hier_moe_block(x, gain, w_rg, b_rg, w_re, b_re, w_gate, w_up, w_down, layer, final_gain, final_norm,
                   head_rows=None):
    n, d = x.shape
    xn, rw, ri, counts = moe_router(x, gain, w_rg, b_rg, w_re, b_re, tm=512)
    dest, tile_expert, n_used, n_rows = _moe_plan(ri, counts, n)
    x_sorted = moe_dispatch(dest, xn, n_rows, tm=512)
    y_sorted = moe_ffn(tile_expert, n_used, x_sorted, w_gate, w_up, w_down, layer)
    return moe_combine(dest, x, rw, y_sorted, final_gain, tm=256, final_norm=final_norm, head_rows=head_rows)


MLA_HEADS = 16
MLA_LORA = 256
MLA_NOPE = 64
MLA_ROPE = 32
MLA_V = 64
MLA_SCALE = (MLA_NOPE + MLA_ROPE) ** -0.5
Q_SCALE = MLA_SCALE * math.log2(math.e)
ROPE_THETA = 10000.0
HEAD_PAD = 128
ROPE_AT = MLA_NOPE
Q_DEC = 384
V_ROWS = 80


def _rope_tables(pos):
    half = MLA_ROPE // 2
    inv = ROPE_THETA ** (-jnp.arange(half, dtype=F32) / half)
    ang = pos.astype(F32)[:, None] * inv[None, :]
    cos, sin = jnp.cos(ang), jnp.sin(ang)
    n = pos.shape[0]
    ones = jnp.ones((n, MLA_NOPE), F32)
    zeros_n = jnp.zeros((n, MLA_NOPE), F32)
    zeros_p = jnp.zeros((n, HEAD_PAD - MLA_NOPE - MLA_ROPE), F32)
    ctab = jnp.concatenate([ones, cos, cos, zeros_p], axis=1)
    stab = jnp.concatenate([zeros_n, sin, sin, zeros_p], axis=1)
    return ctab, stab


def _mla_weights(w_in, w_uq, w_uk, w_uv):
    d = w_in.shape[0]
    half = MLA_ROPE // 2
    w_kr = w_in[:, 2 * MLA_LORA:]
    zl = jnp.zeros((d, ROPE_AT), F32)
    zr = jnp.zeros((d, HEAD_PAD - ROPE_AT - MLA_ROPE), F32)
    kr_a = jnp.concatenate([zl, w_kr, zr], axis=1)
    kr_b = jnp.concatenate([zl, -w_kr[:, half:], w_kr[:, :half], zr], axis=1)
    w_in_p = jnp.concatenate([w_in[:, :2 * MLA_LORA], kr_a, kr_b], axis=1).astype(BF16)
    wq = w_uq.reshape(MLA_LORA, MLA_HEADS, MLA_NOPE + MLA_ROPE)
    nope, x1, x2 = wq[..., :MLA_NOPE], wq[..., MLA_NOPE:MLA_NOPE + half], wq[..., MLA_NOPE + half:]
    zp = jnp.zeros((MLA_LORA, MLA_HEADS, HEAD_PAD - MLA_NOPE - MLA_ROPE), F32)
    wq_a = jnp.concatenate([nope, x1, x2, zp], axis=-1).reshape(MLA_LORA, MLA_HEADS * HEAD_PAD).astype(BF16)
    wq_b = jnp.concatenate([jnp.zeros_like(nope), -x2, x1, zp], axis=-1).reshape(MLA_LORA, MLA_HEADS * HEAD_PAD).astype(BF16)
    zk = jnp.zeros((MLA_LORA, MLA_HEADS, HEAD_PAD - MLA_NOPE), F32)
    wuk_p = jnp.concatenate([w_uk, zk], axis=-1).reshape(MLA_LORA, MLA_HEADS * HEAD_PAD).astype(BF16)
    wuv = w_uv.reshape(MLA_LORA, MLA_HEADS * MLA_V).astype(BF16)
    absorb = jnp.transpose(w_uk, (1, 2, 0))
    sel = jnp.zeros((MLA_ROPE, Q_DEC - MLA_LORA), F32).at[jnp.arange(MLA_ROPE), jnp.arange(MLA_ROPE)].set(1.0)
    top = jnp.concatenate([absorb, jnp.zeros((MLA_HEADS, MLA_NOPE, Q_DEC - MLA_LORA), F32)], axis=-1)
    mid = jnp.broadcast_to(jnp.concatenate([jnp.zeros((MLA_ROPE, MLA_LORA), F32), sel], axis=-1)[None],
                           (MLA_HEADS, MLA_ROPE, Q_DEC))
    bot = jnp.zeros((MLA_HEADS, HEAD_PAD - MLA_NOPE - MLA_ROPE, Q_DEC), F32)
    w_dec = jnp.concatenate([top, mid, bot], axis=1).astype(BF16)
    return w_in_p, wq_a, wq_b, wuk_p, wuv, w_dec


def _mla_q_kernel(cq_ref, ckv_ref, kra_ref, krb_ref, ct_ref, st_ref, ctt_ref, stt_ref, qn_ref, kvn_ref,
                  wa_ref, wb_ref, qt_ref, ckvn_ref, krot_ref):
    cq = cq_ref[...]
    ms = jnp.mean(cq * cq, axis=-1, keepdims=True)
    cqn = (cq * lax.rsqrt(ms + EPS) * qn_ref[...]).astype(BF16)
    ckv = ckv_ref[...]
    ms2 = jnp.mean(ckv * ckv, axis=-1, keepdims=True)
    ckvn_ref[...] = ckv * lax.rsqrt(ms2 + EPS) * kvn_ref[...]
    krot_ref[...] = kra_ref[...] * ct_ref[...] + krb_ref[...] * st_ref[...]
    ctt = ctt_ref[...]
    stt = stt_ref[...]
    for h in range(MLA_HEADS):
        sl = slice(h * HEAD_PAD, (h + 1) * HEAD_PAD)
        qh = _dot_nt(wa_ref[sl, :], cqn) * ctt + _dot_nt(wb_ref[sl, :], cqn) * stt
        qt_ref[sl, :] = (qh * Q_SCALE).astype(qt_ref.dtype)


def mla_q(cq, ckv, kr_a, kr_b, ctab, stab, q_norm, kv_norm, wq_at, wq_bt, *, tm):
    n = cq.shape[0]
    hw = MLA_HEADS * HEAD_PAD

    def tok(w):
        return pl.BlockSpec((tm, w), lambda i: (i, 0))

    def tok_t(w):
        return pl.BlockSpec((w, tm), lambda i: (0, i))

    def const(shape):
        return pl.BlockSpec(shape, lambda i: (0, 0))

    return pl.pallas_call(
        _mla_q_kernel,
        grid=(n // tm,),
        in_specs=[tok(MLA_LORA), tok(MLA_LORA), tok(HEAD_PAD), tok(HEAD_PAD), tok(HEAD_PAD), tok(HEAD_PAD),
                  tok_t(HEAD_PAD), tok_t(HEAD_PAD),
                  const((1, MLA_LORA)), const((1, MLA_LORA)), const(wq_at.shape), const(wq_bt.shape)],
        out_specs=[tok_t(hw), tok(MLA_LORA), tok(HEAD_PAD)],
        out_shape=[jax.ShapeDtypeStruct((hw, n), BF16),
                   jax.ShapeDtypeStruct((n, MLA_LORA), F32),
                   jax.ShapeDtypeStruct((n, HEAD_PAD), F32)],
        compiler_params=_cparams("parallel"),
        name="mla_q",
    )(cq, ckv, kr_a, kr_b, ctab, stab, ctab.T, stab.T, q_norm.reshape(1, MLA_LORA), kv_norm.reshape(1, MLA_LORA),
      wq_at, wq_bt)


def _mla_kv_kernel(ckvn_ref, krot_ref, wk_ref, wvt_ref, k_ref, vt_ref):
    c = ckvn_ref[...].astype(BF16)
    krot = krot_ref[...]
    for h in range(MLA_HEADS):
        sl = slice(h * HEAD_PAD, (h + 1) * HEAD_PAD)
        k_ref[:, sl] = (_dot(c, wk_ref[:, sl]) + krot).astype(k_ref.dtype)
    vt = _dot_nt(wvt_ref[...], c)
    row = lax.broadcasted_iota(jnp.int32, vt.shape, 0)
    vt_ref[...] = jnp.where(row % V_ROWS == MLA_V, 1.0, vt).astype(vt_ref.dtype)


def mla_kv(ckvn, krot, wuk_p, wuv_t, *, n, tm):
    hw = MLA_HEADS * HEAD_PAD
    vw = MLA_HEADS * V_ROWS
    wuv_t = jnp.pad(wuv_t.reshape(MLA_HEADS, MLA_V, MLA_LORA), ((0, 0), (0, V_ROWS - MLA_V), (0, 0))).reshape(vw, MLA_LORA)
    return pl.pallas_call(
        _mla_kv_kernel,
        grid=(n // tm,),
        in_specs=[pl.BlockSpec((tm, MLA_LORA), lambda i: (i, 0)),
                  pl.BlockSpec((tm, HEAD_PAD), lambda i: (i, 0)),
                  pl.BlockSpec(wuk_p.shape, lambda i: (0, 0)),
                  pl.BlockSpec(wuv_t.shape, lambda i: (0, 0))],
        out_specs=[pl.BlockSpec((tm, hw), lambda i: (i, 0)),
                   pl.BlockSpec((None, vw, tm), lambda i: (i, 0, 0))],
        out_shape=[jax.ShapeDtypeStruct((n, hw), BF16),
                   jax.ShapeDtypeStruct((n // tm, vw, tm), BF16)],
        compiler_params=_cparams("parallel"),
        name="mla_kv",
    )(ckvn, krot, wuk_p, wuv_t)


def _flash_kernel(qt_ref, k_ref, vt_ref, ot_ref, m_scr, acc_scr, p_scr, alpha_scr, *, tq, tkb):
    qi = pl.program_id(2)
    sub = tq // tkb
    m_scr[...] = jnp.full_like(m_scr, NEG_BIG)
    acc_scr[...] = jnp.zeros_like(acc_scr)
    p_scr[...] = jnp.zeros_like(p_scr)
    alpha_scr[...] = jnp.ones_like(alpha_scr)

    def retire(kb_prev):
        for hh in range(FLASH_HEADS):
            pv = _dot(vt_ref[kb_prev, hh * V_ROWS:(hh + 1) * V_ROWS, :], p_scr[hh])
            acc_scr[hh] = alpha_scr[hh] * acc_scr[hh] + pv

    strips = [(hh, st) for hh in range(FLASH_HEADS) for st in range(tq // FLASH_STRIP)]

    def score(kb, hh, st):
        rows = pl.ds(pl.multiple_of(kb * tkb, tkb), tkb)
        return _dot(k_ref[rows, hh * HEAD_PAD:(hh + 1) * HEAD_PAD],
                    qt_ref[hh * HEAD_PAD:(hh + 1) * HEAD_PAD, st * FLASH_STRIP:(st + 1) * FLASH_STRIP])

    def key_block(kb, diag_off):
        scores = [score(kb, hh, st) for hh, st in strips]
        retire(jnp.maximum(kb - 1, 0))
        for (hh, st), s in zip(strips, scores):
            cols = slice(st * FLASH_STRIP, (st + 1) * FLASH_STRIP)
            if diag_off is not None:
                key = lax.broadcasted_iota(jnp.int32, s.shape, 0) + diag_off
                qry = lax.broadcasted_iota(jnp.int32, s.shape, 1) + st * FLASH_STRIP
                s = jnp.where(key <= qry, s, NEG_BIG)
            m_old = m_scr[hh, :, cols]
            m_new = jnp.maximum(m_old, jnp.max(s, axis=0, keepdims=True))
            alpha = jnp.exp2(m_old - m_new)
            m_scr[hh, :, cols] = m_new
            p_scr[hh, :, cols] = jnp.exp2(s - m_new).astype(BF16)
            alpha_scr[hh, :, cols] = alpha

    def full_tile(j, carry):
        for d in range(sub):
            key_block(j * sub + d, None)
        return carry

    lax.fori_loop(0, qi, full_tile, 0)
    for d in range(sub):
        key_block(qi * sub + d, d * tkb)
    retire(qi * sub + sub - 1)
    for hh in range(FLASH_HEADS):
        acc = acc_scr[hh]
        ot_ref[hh * MLA_V:(hh + 1) * MLA_V, :] = (acc[:MLA_V] / acc[MLA_V:MLA_V + 1]).astype(ot_ref.dtype)


def flash_attention(qt, k, vt, *, b, t, tq):
    tkb = vt.shape[2]
    nq = t // tq
    nkb = t // tkb
    assert tq % tkb == 0 and t % tq == 0
    fh = FLASH_HEADS
    n_hp = MLA_HEADS // fh
    return pl.pallas_call(
        functools.partial(_flash_kernel, tq=tq, tkb=tkb),
        grid=(b, n_hp, nq),
        in_specs=[pl.BlockSpec((fh * HEAD_PAD, tq), lambda bi, hp, qi: (hp, bi * nq + qi)),
                  pl.BlockSpec((t, fh * HEAD_PAD), lambda bi, hp, qi: (bi, hp)),
                  pl.BlockSpec((nkb, fh * V_ROWS, tkb), lambda bi, hp, qi: (bi, hp, 0))],
        out_specs=pl.BlockSpec((fh * MLA_V, tq), lambda bi, hp, qi: (hp, bi * nq + qi)),
        out_shape=jax.ShapeDtypeStruct((MLA_HEADS * MLA_V, b * t), BF16),
        scratch_shapes=[pltpu.VMEM((fh, 1, tq), F32),
                        pltpu.VMEM((fh, V_ROWS, tq), F32), pltpu.VMEM((fh, tkb, tq), BF16),
                        pltpu.VMEM((fh, 1, tq), F32)],
        compiler_params=_cparams("parallel", "parallel", "arbitrary"),
        name="mla_flash",
    )(qt, k, vt)


PAGE = 128
DEC_PB = 32
DEC_SUB = 2048
DEC_NBUF = 4
DEC_AHEAD = 2
NEW_PAD = 8


def _q_dec_kernel(qt_ref, w_ref, o_ref):
    for h in range(MLA_HEADS):
        o_ref[:, h * Q_DEC:(h + 1) * Q_DEC] = _dot_tn(qt_ref[h * HEAD_PAD:(h + 1) * HEAD_PAD, :], w_ref[h]).astype(o_ref.dtype)


def mla_q_dec(qt, w_dec):
    n = qt.shape[1]
    return pl.pallas_call(
        _q_dec_kernel,
        grid=(1,),
        in_specs=[pl.BlockSpec(qt.shape, lambda i: (0, 0)), pl.BlockSpec(w_dec.shape, lambda i: (0, 0, 0))],
        out_specs=pl.BlockSpec((n, MLA_HEADS * Q_DEC), lambda i: (0, 0)),
        out_shape=jax.ShapeDtypeStruct((n, MLA_HEADS * Q_DEC), BF16),
        compiler_params=_cparams("arbitrary"),
        name="mla_q_dec",
    )(qt, w_dec)


def _decode_kernel(pt_ref, q_ref, cnew_ref, rnew_ref, cache_c, cache_rt, o_ref, cbuf, rbuf, sem,
                   *, layer, n_pages, t_new, n_seq):
    b = pl.program_id(0)
    n_blk = n_pages // DEC_PB
    rows = q_ref.shape[0]

    def page_copies(pg, slot, i):
        off = pl.ds(pl.multiple_of(i * PAGE, PAGE), PAGE)
        cc = pltpu.make_async_copy(cache_c.at[layer, pg], cbuf.at[slot, off], sem.at[0, slot])
        cr = pltpu.make_async_copy(cache_rt.at[layer, pg], rbuf.at[slot, :, off], sem.at[1, slot])
        return cc, cr

    def start_block(seq, j, slot):
        def body(i, carry):
            cc, cr = page_copies(pt_ref[seq, j * DEC_PB + i], slot, i)
            cc.start()
            cr.start()
            return carry
        lax.fori_loop(0, DEC_PB, body, 0, unroll=8)

    def wait_block(slot):
        def body(i, carry):
            cc, cr = page_copies(0, slot, 0)
            cc.wait()
            cr.wait()
            return carry
        lax.fori_loop(0, DEC_PB, body, 0, unroll=8)

    q = q_ref[...]
    ql = q[:, :MLA_LORA]
    qr = q[:, MLA_LORA:MLA_LORA + MLA_ROPE]

    @pl.when(b == 0)
    def _():
        for j in range(DEC_AHEAD):
            start_block(0, j, j % DEC_NBUF)

    m = jnp.full((rows, 1), NEG_BIG, F32)
    l = jnp.zeros((rows, 1), F32)
    acc = jnp.zeros((rows, MLA_LORA), F32)
    pending = None
    for j in range(n_blk):
        slot = j % DEC_NBUF
        ahead = j + DEC_AHEAD
        if ahead < n_blk:
            start_block(b, ahead, ahead % DEC_NBUF)
        else:
            @pl.when(b + 1 < n_seq)
            def _():
                start_block(b + 1, ahead - n_blk, (ahead - n_blk) % DEC_NBUF)
        wait_block(slot)
        for u in range(DEC_PB * PAGE // DEC_SUB):
            keys = slice(u * DEC_SUB, (u + 1) * DEC_SUB)
            kc = cbuf[slot, keys, :].astype(BF16)
            krt = rbuf[slot, :, keys].astype(BF16)
            s = _dot_nt(ql, kc) + _dot(qr, krt)
            if pending is not None:
                p_prev, alpha_prev, kc_prev = pending
                acc = alpha_prev * acc + _dot(p_prev, kc_prev)
            m_new = jnp.maximum(m, jnp.max(s, axis=-1, keepdims=True))
            alpha = jnp.exp2(m - m_new)
            p = jnp.exp2(s - m_new)
            l = alpha * l + jnp.sum(p, axis=-1, keepdims=True)
            m = m_new
            pending = (p.astype(BF16), alpha, kc)

    cn = cnew_ref[...].astype(BF16)
    rn = rnew_ref[...].astype(BF16)
    s = _dot_nt(ql, cn) + _dot_nt(qr, rn)
    p_prev, alpha_prev, kc_prev = pending
    acc = alpha_prev * acc + _dot(p_prev, kc_prev)
    t_row = lax.broadcasted_iota(jnp.int32, s.shape, 0) // MLA_HEADS
    col = lax.broadcasted_iota(jnp.int32, s.shape, 1)
    s = jnp.where((col <= t_row) & (col < t_new), s, NEG_BIG)
    m_new = jnp.maximum(m, jnp.max(s, axis=-1, keepdims=True))
    alpha = jnp.exp2(m - m_new)
    p = jnp.exp2(s - m_new)
    l = alpha * l + jnp.sum(p, axis=-1, keepdims=True)
    acc = alpha * acc + _dot(p.astype(BF16), cn)
    o_ref[...] = (acc / l).astype(o_ref.dtype)


def mla_decode(page_table, q_dec, c_new, r_new, cache_ckv, cache_krope_t, *, layer, t_new):
    b, rows, _ = q_dec.shape
    n_pages = page_table.shape[1]
    assert n_pages % (DEC_NBUF * DEC_PB) == 0 and DEC_AHEAD < DEC_NBUF
    grid_spec = pltpu.PrefetchScalarGridSpec(
        num_scalar_prefetch=1,
        grid=(b,),
        in_specs=[pl.BlockSpec((None, rows, Q_DEC), lambda i, pt: (i, 0, 0)),
                  pl.BlockSpec((None, NEW_PAD, MLA_LORA), lambda i, pt: (i, 0, 0)),
                  pl.BlockSpec((None, NEW_PAD, MLA_ROPE), lambda i, pt: (i, 0, 0)),
                  pl.BlockSpec(memory_space=pl.ANY),
                  pl.BlockSpec(memory_space=pl.ANY)],
        out_specs=pl.BlockSpec((None, rows, MLA_LORA), lambda i, pt: (i, 0, 0)),
        scratch_shapes=[pltpu.VMEM((DEC_NBUF, DEC_PB * PAGE, MLA_LORA), F32),
                        pltpu.VMEM((DEC_NBUF, MLA_ROPE, DEC_PB * PAGE), F32),
                        pltpu.SemaphoreType.DMA((2, DEC_NBUF))],
    )
    return pl.pallas_call(
        functools.partial(_decode_kernel, layer=layer, n_pages=n_pages, t_new=t_new, n_seq=b),
        grid_spec=grid_spec,
        out_shape=jax.ShapeDtypeStruct((b, rows, MLA_LORA), BF16),
        compiler_params=_cparams("arbitrary"),
        name="mla_decode",
    )(page_table, q_dec, c_new, r_new, cache_ckv, cache_krope_t)


def _sample_v_kernel(o_ref, wt_ref, yt_ref):
    for h in range(MLA_HEADS):
        yt_ref[h * MLA_V:(h + 1) * MLA_V, :] = _dot_nt(wt_ref[h * MLA_V:(h + 1) * MLA_V, :],
                                                       o_ref[:, h * MLA_LORA:(h + 1) * MLA_LORA]).astype(yt_ref.dtype)


def mla_sample_v(o_lat, wuv_t):
    n = o_lat.shape[0]
    return pl.pallas_call(
        _sample_v_kernel,
        grid=(1,),
        in_specs=[pl.BlockSpec(o_lat.shape, lambda i: (0, 0)), pl.BlockSpec(wuv_t.shape, lambda i: (0, 0))],
        out_specs=pl.BlockSpec((MLA_HEADS * MLA_V, n), lambda i: (0, 0)),
        out_shape=jax.ShapeDtypeStruct((MLA_HEADS * MLA_V, n), BF16),
        compiler_params=_cparams("arbitrary"),
        name="mla_sample_v",
    )(o_lat, wuv_t)


HG_CHUNK = 64
SSD_CHUNK = 128
SAMPLE_PAD = 8
SAMPLE_SEQS = 4
FLASH_QUERIES = 512
FLASH_KEYS = 512
FLASH_STRIP = 256
FLASH_HEADS = 2
ROW_TILE = 256
WIDE_TILE = 512


def _ab_layer(x, n_p, bp, tp, bs, ts, norm_w, lb, st_hg, st_ssm, st_conv, w_in, w_out, hg_norm, conv_w, conv_b,
              dt_bias, a_log, d_skip, ssm_norm):
    sizes = [HG_W, HG_W, HG_W, HG_W, SSM_INNER, CONV_DIM, SSM_HEADS]
    offs = np.concatenate([[0], np.cumsum(sizes)])
    w_pad = jnp.pad(w_in, ((0, 0), (0, LANES - SSM_HEADS))).astype(BF16)
    splits = [(int(offs[j]), int(offs[j + 1])) for j in range(6)] + [(int(offs[6]), int(offs[6]) + LANES)]
    q, f, i_in, g, z, xbc, dt = norm_matmul(x, norm_w, w_pad, splits, [F32] * 7, tm=ROW_TILE)

    def grp(a, prompt):
        if prompt:
            return a
        a = a[n_p:].reshape(bs, ts, a.shape[1])
        return jnp.pad(a, ((0, 0), (0, SAMPLE_PAD - ts), (0, 0))).reshape(bs * SAMPLE_PAD, a.shape[2])

    outs = []
    for prompt in (True, False):
        if prompt:
            b_, t_, c_hg, c_ssd, tv_hg, tv_ssd = bp, tp, HG_CHUNK, SSD_CHUNK, HG_CHUNK, SSD_CHUNK
            s_hg = jnp.zeros((bp, HG_HEADS, HG_D, HG_D), F32)
            s_ssm = jnp.zeros((bp, SSM_HEADS, SSM_P, SSM_N), F32)
            s_conv = jnp.zeros((bp, CONV_W - 1, CONV_DIM), F32)
        else:
            b_, t_, c_hg, c_ssd, tv_hg, tv_ssd = bs, SAMPLE_PAD, SAMPLE_PAD, SAMPLE_PAD, ts, ts
            s_hg, s_ssm, s_conv = st_hg, st_ssm, st_conv
        n_seq = 1 if prompt else math.gcd(bs, SAMPLE_SEQS)
        o_hg, hg_new = hgrn_scan(grp(q, prompt), grp(f, prompt), grp(i_in, prompt), grp(g, prompt), lb, hg_norm,
                                 s_hg, b=b_, t=t_, c=c_hg, t_valid=tv_hg, n_seq=n_seq)
        y, conv_new, ssm_new = ssd_scan(grp(z, prompt), grp(xbc, prompt), grp(dt, prompt), s_conv, s_ssm,
                                        conv_w, conv_b, dt_bias, a_log, d_skip, ssm_norm,
                                        b=b_, t=t_, c=c_ssd, t_valid=tv_ssd, n_seq=n_seq)
        if not prompt:
            o_hg = o_hg.reshape(bs, SAMPLE_PAD, HG_W)[:, :ts].reshape(bs * ts, HG_W)
            y = y.reshape(bs, SAMPLE_PAD, SSM_INNER)[:, :ts].reshape(bs * ts, SSM_INNER)
        outs.append((o_hg, y, hg_new, ssm_new, conv_new))
    o_hg = (outs[0][0], outs[1][0])
    y = (outs[0][1], outs[1][1])
    w_out_b = w_out.astype(BF16)
    x = matmul_residual(x, [(o_hg, w_out_b[:HG_W], False), (y, w_out_b[HG_W:], False)], tm=WIDE_TILE)
    return x, outs[0][2:], outs[1][2:]


def _mla_layer(x, n_p, bp, tp, bs, ts, past_len, norm_w, cache_ckv, cache_krope, page_table, layer_c,
               w_in, q_norm, kv_norm, w_uq, w_uk, w_uv, w_out):
    w_in_p, wq_a, wq_b, wuk_p, wuv, w_dec = _mla_weights(w_in, w_uq, w_uk, w_uv)
    wuv_t = wuv.T
    splits = [(0, MLA_LORA), (MLA_LORA, 2 * MLA_LORA), (2 * MLA_LORA, 2 * MLA_LORA + HEAD_PAD),
              (2 * MLA_LORA + HEAD_PAD, 2 * MLA_LORA + 2 * HEAD_PAD)]
    cq, ckv, kr_a, kr_b = norm_matmul(x, norm_w, w_in_p, splits, [F32] * 4, tm=WIDE_TILE)
    pos = jnp.concatenate([jnp.tile(jnp.arange(tp, dtype=jnp.int32), bp),
                           jnp.tile(past_len + jnp.arange(ts, dtype=jnp.int32), bs)])
    ctab, stab = _rope_tables(pos)
    qt, ckvn, krot = mla_q(cq, ckv, kr_a, kr_b, ctab, stab, q_norm, kv_norm, wq_a.T, wq_b.T, tm=WIDE_TILE)
    krope = krot[:, ROPE_AT:ROPE_AT + MLA_ROPE]
    k_p, vt_p = mla_kv(ckvn, krot, wuk_p, wuv_t, n=n_p, tm=math.gcd(tp, FLASH_KEYS))
    ot_p = flash_attention(qt, k_p, vt_p, b=bp, t=tp, tq=math.gcd(tp, FLASH_QUERIES))
    q_dec = mla_q_dec(qt[:, n_p:], w_dec).reshape(bs, ts * MLA_HEADS, Q_DEC)
    c_new = jnp.pad(ckvn[n_p:].reshape(bs, ts, MLA_LORA), ((0, 0), (0, NEW_PAD - ts), (0, 0)))
    r_new = jnp.pad(krope[n_p:].reshape(bs, ts, MLA_ROPE), ((0, 0), (0, NEW_PAD - ts), (0, 0)))
    cache_krope_t = jnp.swapaxes(cache_krope, 2, 3)
    o_lat = mla_decode(page_table, q_dec, c_new, r_new, cache_ckv, cache_krope_t, layer=layer_c, t_new=ts)
    ot_s = mla_sample_v(o_lat.reshape(bs * ts, MLA_HEADS * MLA_LORA), wuv_t)
    x = matmul_residual(x, [((ot_p, ot_s), w_out.astype(BF16), True)], tm=WIDE_TILE)
    return x, (ckvn[:n_p].reshape(bp, tp, MLA_LORA), krope[:n_p].reshape(bp, tp, MLA_ROPE)), \
        (ckvn[n_p:].reshape(bs, ts, MLA_LORA), krope[n_p:].reshape(bs, ts, MLA_ROPE))


def kernel(x_prompt, x_sample, state_hgrn, state_ssm, state_conv, cache_ckv, cache_krope, page_table,
           norm_mix, norm_ffn, norm_final, w_in_ab, w_out_ab, hgrn_lb, hgrn_norm, conv_w, conv_b,
           dt_bias, a_log, d_skip, ssm_norm, w_in_c, q_norm, kv_norm, w_uq, w_uk, w_uv, w_out_c,
           w_route_group, b_route_group, w_route_expert, b_route_expert, w_gate, w_up, w_down):
    bp, tp, d = x_prompt.shape
    bs, ts, _ = x_sample.shape
    n_p = bp * tp
    depth = norm_mix.shape[0]
    n_a = w_in_ab.shape[0]
    past_len = page_table.shape[1] * cache_ckv.shape[2]
    lb_all = jnp.cumsum(jax.nn.softmax(hgrn_lb.astype(F32), axis=0), axis=0)[:n_a]
    x = (x_prompt.reshape(n_p, d), x_sample.reshape(bs * ts, d))
    a_p, a_s, c_p, c_s = [], [], [], []
    for layer in range(depth):
        j = layer // 2
        if layer % 2 == 0:
            x, sp, ss = _ab_layer(x, n_p, bp, tp, bs, ts, norm_mix[layer], lb_all[j], state_hgrn[j], state_ssm[j],
                                  state_conv[j], w_in_ab[j], w_out_ab[j], hgrn_norm[j], conv_w[j], conv_b[j],
                                  dt_bias[j], a_log[j], d_skip[j], ssm_norm[j])
            a_p.append(sp)
            a_s.append(ss)
        else:
            x, cp, cs = _mla_layer(x, n_p, bp, tp, bs, ts, past_len, norm_mix[layer], cache_ckv, cache_krope,
                                   page_table, j, w_in_c[j], q_norm[j], kv_norm[j], w_uq[j], w_uk[j], w_uv[j],
                                   w_out_c[j])
            c_p.append(cp)
            c_s.append(cs)
        x = hier_moe_block(x, norm_ffn[layer], w_route_group[layer], b_route_group[layer], w_route_expert[layer],
                           b_route_expert[layer], w_gate, w_up, w_down, layer, norm_final,
                           final_norm=(layer == depth - 1), head_rows=(n_p if layer == depth - 1 else None))
    y_p, y_s = x

    def stack(items, k):
        return jnp.stack([it[k] for it in items])

    return (y_p.reshape(bp, tp, d), y_s.reshape(bs, ts, d),
            stack(a_p, 0), stack(a_s, 0), stack(a_p, 1), stack(a_s, 1), stack(a_p, 2), stack(a_s, 2),
            stack(c_p, 0), stack(c_s, 0), stack(c_p, 1), stack(c_s, 1))
```

```python
import functools
import math

import jax
import jax.numpy as jnp
import numpy as np
from jax import lax
from jax.experimental import pallas as pl
from jax.experimental.pallas import tpu as pltpu

F32 = jnp.float32
BF16 = jnp.bfloat16

EPS = 1e-6
D_MODEL = 1024
HG_HEADS = 4
HG_D = 128
HG_W = HG_HEADS * HG_D
SSM_HEADS = 16
SSM_P = 64
SSM_N = 128
SSM_GROUPS = 2
SSM_INNER = SSM_HEADS * SSM_P
CONV_W = 4
CONV_DIM = SSM_INNER + 2 * SSM_GROUPS * SSM_N
LANES = 128
SUBLANES = 8
VMEM_LIMIT = 48 * 1024 * 1024
NEG_BIG = -1e30


def _cparams(*sem):
    return pltpu.CompilerParams(dimension_semantics=sem, vmem_limit_bytes=VMEM_LIMIT)


def _dot(a, b):
    return jnp.dot(a, b, preferred_element_type=F32)


def _dot_nt(a, b):
    return lax.dot_general(a, b, (((1,), (1,)), ((), ())), preferred_element_type=F32)


def _dot_tn(a, b):
    return lax.dot_general(a, b, (((0,), (0,)), ((), ())), preferred_element_type=F32)


def _split3(x):
    hi = x.astype(BF16)
    r1 = x - hi.astype(F32)
    mid = r1.astype(BF16)
    lo = (r1 - mid.astype(F32)).astype(BF16)
    return hi, mid, lo


def _silu(x):
    return x * (1.0 / (1.0 + jnp.exp(-x)))


def _sigmoid(x):
    return 1.0 / (1.0 + jnp.exp(-x))


def _as_parts(a):
    return tuple(a) if isinstance(a, (tuple, list)) else (a,)


def _part_specs(parts, tm, transposed=False):
    specs, bounds, start = [], [], 0
    for p in parts:
        n_t = (p.shape[1] if transposed else p.shape[0]) // tm

        def imap(i, lo=start, n_t=n_t):
            j = jnp.clip(i - lo, 0, n_t - 1)
            return (0, j) if transposed else (j, 0)

        specs.append(pl.BlockSpec((p.shape[0], tm) if transposed else (tm, p.shape[1]), imap))
        start += n_t
        bounds.append(start)
    return specs, tuple(bounds)


def _pick_part(i, bounds, refs):
    val = refs[-1][...]
    for k in reversed(range(len(refs) - 1)):
        val = jnp.where(i < bounds[k], refs[k][...], val)
    return val


def _norm_matmul_kernel(*refs, x_bounds, splits, normalize):
    n_x = len(x_bounds)
    g_ref, w_ref = refs[n_x], refs[n_x + 1]
    out_refs = refs[n_x + 2:]
    x = _pick_part(pl.program_id(0), x_bounds, refs[:n_x]).astype(F32)
    if normalize:
        ms = jnp.mean(x * x, axis=-1, keepdims=True)
        x = x * lax.rsqrt(ms + EPS) * g_ref[...]
    h = x.astype(BF16)
    for (a, b), o_ref in zip(splits, out_refs):
        o_ref[...] = _dot(h, w_ref[:, a:b]).astype(o_ref.dtype)


def norm_matmul(x, gain, w, splits, out_dtypes, *, tm, normalize=True):
    x_parts = _as_parts(x)
    k = x_parts[0].shape[1]
    x_specs, x_bounds = _part_specs(x_parts, tm)
    n = x_bounds[-1] * tm
    assert all(p.shape[0] % tm == 0 for p in x_parts)
    kern = functools.partial(_norm_matmul_kernel, x_bounds=x_bounds, splits=tuple(splits), normalize=normalize)
    out_shape = [jax.ShapeDtypeStruct((n, b - a), dt) for (a, b), dt in zip(splits, out_dtypes)]
    out_specs = [pl.BlockSpec((tm, b - a), lambda i: (i, 0)) for (a, b) in splits]
    return pl.pallas_call(
        kern,
        grid=(n // tm,),
        in_specs=x_specs + [pl.BlockSpec((1, k), lambda i: (0, 0)),
                            pl.BlockSpec(w.shape, lambda i: (0, 0))],
        out_specs=out_specs,
        out_shape=out_shape,
        compiler_params=_cparams("parallel"),
        name="norm_matmul",
    )(*x_parts, gain.reshape(1, k), w)


def _matmul_residual_kernel(*refs, res_bounds, a_bounds, transposed):
    i = pl.program_id(0)
    pos = len(res_bounds)
    acc = _pick_part(i, res_bounds, refs[:pos])
    for bounds, tr in zip(a_bounds, transposed):
        a = _pick_part(i, bounds, refs[pos:pos + len(bounds)]).astype(BF16)
        w_ref = refs[pos + len(bounds)]
        pos += len(bounds) + 1
        acc = acc + (_dot_tn(a, w_ref[...]) if tr else _dot(a, w_ref[...]))
    refs[pos][...] = acc


def matmul_residual(res, pairs, *, tm):
    res_parts = _as_parts(res)
    d = res_parts[0].shape[1]
    in_specs, res_bounds = _part_specs(res_parts, tm)
    n = res_bounds[-1] * tm
    args = list(res_parts)
    a_bounds = []
    for a, w, tr in pairs:
        a_parts = _as_parts(a)
        specs, bounds = _part_specs(a_parts, tm, transposed=tr)
        assert bounds[-1] == res_bounds[-1]
        in_specs += specs + [pl.BlockSpec(w.shape, lambda i: (0, 0))]
        args += list(a_parts) + [w]
        a_bounds.append(bounds)
    kern = functools.partial(_matmul_residual_kernel, res_bounds=res_bounds, a_bounds=tuple(a_bounds),
                             transposed=tuple(bool(p[2]) for p in pairs))
    return pl.pallas_call(
        kern,
        grid=(n // tm,),
        in_specs=in_specs,
        out_specs=pl.BlockSpec((tm, d), lambda i: (i, 0)),
        out_shape=jax.ShapeDtypeStruct((n, d), F32),
        compiler_params=_cparams("parallel"),
        name="matmul_residual",
    )(*args)


def _hgrn_level_halves(c):
    halves = []
    b = c // 2
    while b >= 1:
        halves.append(b)
        b //= 2
    return halves


@functools.lru_cache(maxsize=None)
def _hgrn_consts(c):
    t = np.arange(c)
    rows = [t[None, :] <= t[:, None], t[None, :] > t[:, None]]
    masks = []
    for b in _hgrn_level_halves(c):
        blk = t // b
        st = blk * b
        en = st + b - 1
        odd = blk % 2 == 1
        even = ~odd
        rows.append(odd[:, None] & (t[None, :] >= st[:, None]) & (t[None, :] <= t[:, None]))
        rows.append(even[:, None] & (t[None, :] > t[:, None]) & (t[None, :] <= en[:, None]))
        masks.append((t[:, None] // (2 * b) == t[None, :] // (2 * b)) & odd[:, None] & even[None, :])
    masks.append(np.eye(c, dtype=bool))
    rows.append(np.ones((SUBLANES, c), dtype=bool))
    dg = np.concatenate(rows, axis=0).astype(np.float32)
    dg3 = np.concatenate([dg, dg, dg], axis=1)
    mk = np.stack(masks).astype(np.float32)
    return dg3, mk


def _hgrn_kernel(q_ref, f_ref, i_ref, g_ref, lb_ref, nw_ref, dg_ref, mk_ref, s0_ref,
                 o_ref, sfin_ref, s_scr, *, c, t_valid, n_chunks, n_seq):
    ci = pl.program_id(1)

    @pl.when(ci == 0)
    def _():
        s_scr[...] = s0_ref[...]

    for si in range(n_seq):
        rows = slice(si * c, (si + 1) * c)
        _hgrn_chunk(q_ref.at[rows], f_ref.at[rows], i_ref.at[rows], g_ref.at[rows], lb_ref, nw_ref, dg_ref, mk_ref,
                    o_ref.at[rows], s_scr.at[si], c=c, t_valid=t_valid)

    @pl.when(ci == n_chunks - 1)
    def _():
        sfin_ref[...] = s_scr[...]


def _hgrn_chunk(q_ref, f_ref, i_ref, g_ref, lb_ref, nw_ref, dg_ref, mk_ref, o_ref, s_scr, *, c, t_valid):
    lb = lb_ref[...]
    fl = f_ref[...]
    sig = _sigmoid(fl)
    logf = jnp.log(lb + (1.0 - lb) * sig)
    kk = (1.0 - lb) * (1.0 - sig)
    if t_valid < c:
        row = lax.broadcasted_iota(jnp.int32, (c, HG_W), 0)
        live = row < t_valid
        logf = jnp.where(live, logf, 0.0)
        kk = jnp.where(live, kk, 0.0)
    qa = _silu(q_ref[...])
    vv = i_ref[...].astype(BF16)

    lf3 = jnp.concatenate(_split3(logf), axis=0)
    ex = jnp.exp(_dot(dg_ref[...], lf3))
    halves = _hgrn_level_halves(c)
    n_lv = len(halves)
    e_cum = ex[0:c]
    e_rev = ex[c:2 * c]
    q_state = (qa * e_cum).astype(BF16)
    k_state = (kk * e_rev).astype(BF16)
    q_lv = [(qa * ex[(2 + 2 * l) * c:(3 + 2 * l) * c]).astype(BF16) for l in range(n_lv)]
    k_lv = [(kk * ex[(3 + 2 * l) * c:(4 + 2 * l) * c]).astype(BF16) for l in range(n_lv)]
    q_lv.append(qa.astype(BF16))
    k_lv.append(kk.astype(BF16))
    ones_t = jnp.ones((3 * c, HG_D), BF16)
    gate = _silu(g_ref[...])
    nw = nw_ref[...]

    for h in range(HG_HEADS):
        sl = slice(h * HG_D, (h + 1) * HG_D)
        sc = jnp.zeros((c, c), F32)
        for l in range(n_lv + 1):
            sc = sc + mk_ref[l] * _dot_nt(q_lv[l][:, sl], k_lv[l][:, sl])
        s_h = s_scr[h]
        o_h = _dot(sc.astype(BF16), vv[:, sl]) + _dot(q_state[:, sl], s_h.astype(BF16))
        dec = jnp.exp(_dot_tn(lf3[:, sl], ones_t))
        s_scr[h] = s_h * dec + _dot_tn(k_state[:, sl], vv[:, sl])
        ms = jnp.mean(o_h * o_h, axis=-1, keepdims=True)
        o_h = o_h * lax.rsqrt(ms + EPS) * nw[:, sl] * gate[:, sl]
        o_ref[:, sl] = o_h.astype(o_ref.dtype)


def hgrn_scan(q, f, i_in, g, lb, norm_w, s0, *, b, t, c, t_valid, n_seq=1):
    n_chunks = t // c
    assert t % c == 0 and (t_valid == c or n_chunks == 1) and q.shape[0] >= b * t
    assert b % n_seq == 0 and (n_seq == 1 or n_chunks == 1)
    dg3, mk = _hgrn_consts(c)
    dg3 = jnp.asarray(dg3, BF16)
    mk = jnp.asarray(mk, F32)
    tok = pl.BlockSpec((n_seq * c, HG_W), lambda bi, ci: (bi * n_chunks + ci, 0))
    st = pl.BlockSpec((n_seq, HG_HEADS, HG_D, HG_D), lambda bi, ci: (bi, 0, 0, 0))
    row = pl.BlockSpec((1, HG_W), lambda bi, ci: (0, 0))
    kern = functools.partial(_hgrn_kernel, c=c, t_valid=t_valid, n_chunks=n_chunks, n_seq=n_seq)
    return pl.pallas_call(
        kern,
        grid=(b // n_seq, n_chunks),
        in_specs=[tok, tok, tok, tok, row, row,
                  pl.BlockSpec(dg3.shape, lambda bi, ci: (0, 0)),
                  pl.BlockSpec(mk.shape, lambda bi, ci: (0, 0, 0)),
                  st],
        out_specs=[tok, st],
        out_shape=[jax.ShapeDtypeStruct((b * t, HG_W), BF16),
                   jax.ShapeDtypeStruct((b, HG_HEADS, HG_D, HG_D), F32)],
        scratch_shapes=[pltpu.VMEM((n_seq, HG_HEADS, HG_D, HG_D), F32)],
        compiler_params=_cparams("parallel", "arbitrary"),
        name="hgrn_scan",
    )(q, f, i_in, g, lb.reshape(1, HG_W), norm_w.reshape(1, HG_W), dg3, mk, s0)


CONV_HEAD = 8


@functools.lru_cache(maxsize=None)
def _ssd_consts(c):
    t = np.arange(c)
    tri = (t[None, :] <= t[:, None]).astype(np.float32)
    rev = (t[None, :] > t[:, None]).astype(np.float32)
    ones = np.ones((SUBLANES, c), np.float32)
    tg = np.concatenate([tri, rev, ones], axis=0)
    tg3 = np.concatenate([tg, tg, tg], axis=1)
    u3 = np.concatenate([tri.T, tri.T, tri.T], axis=0)
    return tg3, u3


def _ssd_kernel(z_ref, xbc_ref, dt_ref, cbuf_ref, cw_ref, cb_ref, dtb_ref, alog_ref, dsk_ref, nw_ref,
                tg_ref, u_ref, h0_ref,
                y_ref, cnew_ref, hfin_ref, xp_scr, h_scr, y_scr, *, c, t_valid, n_chunks, n_seq):
    ci = pl.program_id(1)

    @pl.when(ci == 0)
    def _():
        h_scr[...] = h0_ref[...]
        xp_scr[:, CONV_HEAD - (CONV_W - 1):CONV_HEAD, :] = cbuf_ref[...]

    for si in range(n_seq):
        rows = slice(si * c, (si + 1) * c)
        _ssd_chunk(ci, z_ref.at[rows], xbc_ref.at[rows], dt_ref.at[rows], cw_ref, cb_ref, dtb_ref, alog_ref, dsk_ref,
                   nw_ref, tg_ref, u_ref, y_ref.at[rows], cnew_ref.at[si], xp_scr.at[si], h_scr.at[si], y_scr.at[si],
                   c=c, t_valid=t_valid, n_chunks=n_chunks)

    @pl.when(ci == n_chunks - 1)
    def _():
        hfin_ref[...] = h_scr[...]


def _ssd_chunk(ci, z_ref, xbc_ref, dt_ref, cw_ref, cb_ref, dtb_ref, alog_ref, dsk_ref, nw_ref, tg_ref, u_ref,
               y_ref, cnew_ref, xp_scr, h_scr, y_scr, *, c, t_valid, n_chunks):
    xp_scr[CONV_HEAD:CONV_HEAD + c, :] = xbc_ref[...]
    conv = cb_ref[...]
    for i in range(CONV_W):
        off = CONV_HEAD - (CONV_W - 1) + i
        conv = conv + xp_scr[off:off + c, :] * cw_ref[i:i + 1, :]
    conv = _silu(conv)

    @pl.when(ci == n_chunks - 1)
    def _():
        cnew_ref[...] = xp_scr[CONV_HEAD + t_valid - (CONV_W - 1):CONV_HEAD + t_valid, :]

    xp_scr[CONV_HEAD - (CONV_W - 1):CONV_HEAD, :] = xp_scr[CONV_HEAD + c - (CONV_W - 1):CONV_HEAD + c, :]

    xs = conv[:, :SSM_INNER]
    dt_raw = dt_ref[...] + dtb_ref[...]
    dt = jnp.maximum(dt_raw, 0.0) + jnp.log(1.0 + jnp.exp(-jnp.abs(dt_raw)))
    if t_valid < c:
        row = lax.broadcasted_iota(jnp.int32, (c, LANES), 0)
        dt = jnp.where(row < t_valid, dt, 0.0)
    a = -jnp.exp(alog_ref[...])
    da = dt * a
    da3 = jnp.concatenate(_split3(da), axis=0)
    xx = _dot(tg_ref[...], da3)
    cum = xx[0:c]
    e_cum = jnp.exp(cum)
    w_all = jnp.exp(xx[c:2 * c]) * dt
    e_last = jnp.exp(xx[2 * c:2 * c + 1])
    cum_t = _dot_tn(da3, u_ref[...])

    tril = lax.broadcasted_iota(jnp.int32, (c, c), 0) >= lax.broadcasted_iota(jnp.int32, (c, c), 1)
    heads_per_group = SSM_HEADS // SSM_GROUPS
    for g in range(SSM_GROUPS):
        bm = conv[:, SSM_INNER + g * SSM_N:SSM_INNER + (g + 1) * SSM_N]
        cm = conv[:, SSM_INNER + (SSM_GROUPS + g) * SSM_N:SSM_INNER + (SSM_GROUPS + g + 1) * SSM_N]
        bm_b = bm.astype(BF16)
        cb = _dot_nt(cm.astype(BF16), bm_b)
        for hh in range(heads_per_group):
            h = g * heads_per_group + hh
            x_h = xs[:, h * SSM_P:(h + 1) * SSM_P]
            diff = cum[:, h:h + 1] - cum_t[h:h + 1, :]
            seg = jnp.exp(jnp.where(tril, diff, NEG_BIG))
            xd = (dt[:, h:h + 1] * x_h).astype(BF16)
            cme = (cm * e_cum[:, h:h + 1]).astype(BF16)
            h_h = h_scr[h]
            y_h = _dot((cb * seg).astype(BF16), xd) + _dot_nt(cme, h_h.astype(BF16))
            xw = (w_all[:, h:h + 1] * x_h).astype(BF16)
            h_scr[h] = h_h * e_last[:, h:h + 1] + _dot_tn(xw, bm_b)
            y_scr[:, h * SSM_P:(h + 1) * SSM_P] = y_h

    y = (y_scr[...] + dsk_ref[...] * xs) * _silu(z_ref[...])
    gw = SSM_INNER // SSM_GROUPS
    for g in range(SSM_GROUPS):
        yg = y[:, g * gw:(g + 1) * gw]
        ms = jnp.mean(yg * yg, axis=-1, keepdims=True)
        y_ref[:, g * gw:(g + 1) * gw] = (yg * lax.rsqrt(ms + EPS) * nw_ref[:, g * gw:(g + 1) * gw]).astype(y_ref.dtype)


def ssd_scan(z, xbc, dt, conv_buf, h0, conv_w, conv_b, dt_bias, a_log, d_skip, norm_w, *, b, t, c, t_valid,
             n_seq=1):
    n_chunks = t // c
    assert t % c == 0 and (t_valid == c or n_chunks == 1) and z.shape[0] >= b * t
    assert b % n_seq == 0 and (n_seq == 1 or n_chunks == 1)
    tg3, u3 = _ssd_consts(c)
    tg3 = jnp.asarray(tg3, BF16)
    u3 = jnp.asarray(u3, BF16)

    def pad_heads(v):
        return jnp.pad(v.astype(F32), (0, LANES - SSM_HEADS)).reshape(1, LANES)

    def tok(w):
        return pl.BlockSpec((n_seq * c, w), lambda bi, ci: (bi * n_chunks + ci, 0))

    def const(shape):
        return pl.BlockSpec(shape, lambda bi, ci: (0,) * len(shape))

    cst = pl.BlockSpec((n_seq, CONV_W - 1, CONV_DIM), lambda bi, ci: (bi, 0, 0))
    hst = pl.BlockSpec((n_seq, SSM_HEADS, SSM_P, SSM_N), lambda bi, ci: (bi, 0, 0, 0))
    kern = functools.partial(_ssd_kernel, c=c, t_valid=t_valid, n_chunks=n_chunks, n_seq=n_seq)
    return pl.pallas_call(
        kern,
        grid=(b // n_seq, n_chunks),
        in_specs=[tok(SSM_INNER), tok(CONV_DIM), tok(LANES), cst,
                  const((CONV_W, CONV_DIM)), const((1, CONV_DIM)), const((1, LANES)), const((1, LANES)),
                  const((1, SSM_INNER)), const((1, SSM_INNER)), const(tg3.shape), const(u3.shape), hst],
        out_specs=[tok(SSM_INNER), cst, hst],
        out_shape=[jax.ShapeDtypeStruct((b * t, SSM_INNER), BF16),
                   jax.ShapeDtypeStruct((b, CONV_W - 1, CONV_DIM), F32),
                   jax.ShapeDtypeStruct((b, SSM_HEADS, SSM_P, SSM_N), F32)],
        scratch_shapes=[pltpu.VMEM((n_seq, CONV_HEAD + c, CONV_DIM), F32),
                        pltpu.VMEM((n_seq, SSM_HEADS, SSM_P, SSM_N), F32),
                        pltpu.VMEM((n_seq, c, SSM_INNER), F32)],
        compiler_params=_cparams("parallel", "arbitrary"),
        name="ssd_scan",
    )(z, xbc, dt, conv_buf, conv_w, conv_b.reshape(1, CONV_DIM), pad_heads(dt_bias), pad_heads(a_log),
      jnp.repeat(d_skip.astype(F32), SSM_P).reshape(1, SSM_INNER), norm_w.reshape(1, SSM_INNER), tg3, u3, h0)


N_GROUPS = 4
EXPERTS_PER_GROUP = 8
N_EXPERTS = N_GROUPS * EXPERTS_PER_GROUP
TOP_K = 2
D_EXPERT = 256
MOE_TM = 256


def _router_kernel(x_ref, g_ref, w_ref, b_ref, tri_ref, xn_ref, rw_ref, ri_ref, cnt_ref, carry_scr):
    @pl.when(pl.program_id(0) == 0)
    def _():
        carry_scr[...] = jnp.zeros_like(carry_scr)

    x = x_ref[...]
    ms = jnp.mean(x * x, axis=-1, keepdims=True)
    xn = x * lax.rsqrt(ms + EPS) * g_ref[...]
    for s in range(SUBLANES):
        xn_ref[pl.ds(s, x.shape[0], stride=SUBLANES), :] = xn[:, s * LANES:(s + 1) * LANES]
    logits = jnp.dot(xn, w_ref[...], precision=lax.Precision.HIGHEST, preferred_element_type=F32) + b_ref[...]
    lane = lax.broadcasted_iota(jnp.int32, logits.shape, 1)
    is_g = (lane >= N_EXPERTS) & (lane < N_EXPERTS + N_GROUPS)
    gl = jnp.where(is_g, logits, NEG_BIG)
    gmax = jnp.max(gl, axis=-1, keepdims=True)
    g_sel = jnp.min(jnp.where(gl == gmax, lane, 4 * LANES), axis=-1, keepdims=True) - N_EXPERTS
    g_w = 1.0 / jnp.sum(jnp.where(is_g, jnp.exp(gl - gmax), 0.0), axis=-1, keepdims=True)
    lo = g_sel * EXPERTS_PER_GROUP
    in_grp = (lane >= lo) & (lane < lo + EXPERTS_PER_GROUP)
    el = jnp.where(in_grp, logits, NEG_BIG)
    emax = jnp.max(el, axis=-1, keepdims=True)
    ee = jnp.where(in_grp, jnp.exp(el - emax), 0.0)
    p = ee / jnp.sum(ee, axis=-1, keepdims=True)
    p = jnp.where(in_grp, p, -1.0)
    p1 = jnp.max(p, axis=-1, keepdims=True)
    i1 = jnp.min(jnp.where(p == p1, lane, 4 * LANES), axis=-1, keepdims=True)
    p_rest = jnp.where(lane == i1, -1.0, p)
    p2 = jnp.max(p_rest, axis=-1, keepdims=True)
    i2 = jnp.min(jnp.where(p_rest == p2, lane, 4 * LANES), axis=-1, keepdims=True)
    w1 = p1 / (p1 + p2) * g_w
    w2 = p2 / (p1 + p2) * g_w
    rw_ref[...] = jnp.where(lane == 0, w1, jnp.where(lane == 1, w2, 0.0))
    hit1 = lane == i1
    hit2 = lane == i2
    onehot = (hit1 | hit2).astype(BF16)
    before = _dot(tri_ref[...], onehot) + carry_scr[0:1, :]
    r1 = jnp.sum(jnp.where(hit1, before, 0.0), axis=-1, keepdims=True).astype(jnp.int32)
    r2 = jnp.sum(jnp.where(hit2, before, 0.0), axis=-1, keepdims=True).astype(jnp.int32)
    ri_ref[...] = jnp.where(lane == 0, i1, jnp.where(lane == 1, i2, jnp.where(lane == 2, r1, jnp.where(lane == 3, r2, 0))))
    carry_scr[...] = carry_scr[...] + jnp.sum(onehot.astype(F32), axis=0, keepdims=True)
    cnt_ref[...] = carry_scr[...]


def moe_router(x, gain, w_rg, b_rg, w_re, b_re, *, tm):
    n, d = x.shape
    assert d == SUBLANES * LANES
    w = jnp.zeros((d, LANES), F32).at[:, :N_EXPERTS].set(w_re).at[:, N_EXPERTS:N_EXPERTS + N_GROUPS].set(w_rg)
    b = jnp.zeros((1, LANES), F32).at[0, :N_EXPERTS].set(b_re).at[0, N_EXPERTS:N_EXPERTS + N_GROUPS].set(b_rg)
    t = np.arange(tm)
    tri = jnp.asarray(t[None, :] < t[:, None], BF16)
    return pl.pallas_call(
        _router_kernel,
        grid=(n // tm,),
        in_specs=[pl.BlockSpec((tm, d), lambda i: (i, 0)),
                  pl.BlockSpec((1, d), lambda i: (0, 0)),
                  pl.BlockSpec((d, LANES), lambda i: (0, 0)),
                  pl.BlockSpec((1, LANES), lambda i: (0, 0)),
                  pl.BlockSpec((tm, tm), lambda i: (0, 0))],
        out_specs=[pl.BlockSpec((tm * SUBLANES, LANES), lambda i: (i, 0)),
                   pl.BlockSpec((tm, LANES), lambda i: (i, 0)),
                   pl.BlockSpec((tm, LANES), lambda i: (i, 0)),
                   pl.BlockSpec((SUBLANES, LANES), lambda i: (0, 0))],
        out_shape=[jax.ShapeDtypeStruct((n * SUBLANES, LANES), F32),
                   jax.ShapeDtypeStruct((n, LANES), F32),
                   jax.ShapeDtypeStruct((n, LANES), jnp.int32),
                   jax.ShapeDtypeStruct((SUBLANES, LANES), F32)],
        scratch_shapes=[pltpu.VMEM((SUBLANES, LANES), F32)],
        compiler_params=_cparams("arbitrary"),
        name="moe_router",
    )(x, gain.reshape(1, d), w, b, tri)


def _token_rows(idx):
    return pl.ds(pl.multiple_of(idx * SUBLANES, SUBLANES), SUBLANES)


def _moe_dispatch_kernel(dest_ref, xn_ref, zero_ref, xs_ref, sem, *, tm):
    del zero_ref
    base = pl.program_id(0) * (tm * TOP_K)

    def start(r, carry):
        for k in range(TOP_K):
            pltpu.make_async_copy(xn_ref.at[_token_rows(r)], xs_ref.at[_token_rows(dest_ref[base + r * TOP_K + k])],
                                  sem).start()
        return carry

    lax.fori_loop(0, tm, start, 0, unroll=8)

    def wait(r, carry):
        for k in range(TOP_K):
            pltpu.make_async_copy(xn_ref.at[_token_rows(0)], xs_ref.at[_token_rows(0)], sem).wait()
        return carry

    lax.fori_loop(0, tm, wait, 0, unroll=8)


def moe_dispatch(dest, xn, n_rows, *, tm):
    n = xn.shape[0] // SUBLANES
    grid_spec = pltpu.PrefetchScalarGridSpec(
        num_scalar_prefetch=1,
        grid=(n // tm,),
        in_specs=[pl.BlockSpec((tm * SUBLANES, LANES), lambda i, d: (i, 0)),
                  pl.BlockSpec(memory_space=pl.ANY)],
        out_specs=pl.BlockSpec(memory_space=pl.ANY),
        scratch_shapes=[pltpu.SemaphoreType.DMA(())],
    )
    return pl.pallas_call(
        functools.partial(_moe_dispatch_kernel, tm=tm),
        grid_spec=grid_spec,
        out_shape=jax.ShapeDtypeStruct((n_rows * SUBLANES, LANES), F32),
        input_output_aliases={2: 0},
        compiler_params=_cparams("arbitrary"),
        name="moe_dispatch",
    )(dest, xn, jnp.zeros((n_rows * SUBLANES, LANES), F32))


def _moe_ffn_kernel(te_ref, nu_ref, x_ref, wg_ref, wu_ref, wd_ref, y_ref):
    i = pl.program_id(0)

    @pl.when(i < nu_ref[0])
    def _():
        x = jnp.concatenate([x_ref[pl.ds(s, MOE_TM, stride=SUBLANES), :] for s in range(SUBLANES)], axis=1).astype(BF16)
        a = _dot(x, wg_ref[...].astype(BF16))
        b = _dot(x, wu_ref[...].astype(BF16))
        hid = (_silu(a) * b).astype(BF16)
        y = _dot(hid, wd_ref[...].astype(BF16))
        for s in range(SUBLANES):
            y_ref[pl.ds(s, MOE_TM, stride=SUBLANES), :] = y[:, s * LANES:(s + 1) * LANES]

    @pl.when(i >= nu_ref[0])
    def _():
        y_ref[...] = jnp.zeros_like(y_ref)


def moe_ffn(tile_expert, n_used, x_sorted, w_gate, w_up, w_down, layer):
    n_tiles = x_sorted.shape[0] // (MOE_TM * SUBLANES)
    d = w_gate.shape[2]
    tile = pl.BlockSpec((MOE_TM * SUBLANES, LANES), lambda i, te, nu: (i, 0))
    grid_spec = pltpu.PrefetchScalarGridSpec(
        num_scalar_prefetch=2,
        grid=(n_tiles,),
        in_specs=[tile,
                  pl.BlockSpec((None, None, d, D_EXPERT), lambda i, te, nu: (layer, te[i], 0, 0)),
                  pl.BlockSpec((None, None, d, D_EXPERT), lambda i, te, nu: (layer, te[i], 0, 0)),
                  pl.BlockSpec((None, None, D_EXPERT, d), lambda i, te, nu: (layer, te[i], 0, 0))],
        out_specs=tile,
    )
    return pl.pallas_call(
        _moe_ffn_kernel,
        grid_spec=grid_spec,
        out_shape=jax.ShapeDtypeStruct(x_sorted.shape, F32),
        compiler_params=_cparams("arbitrary"),
        name="moe_ffn",
    )(tile_expert, n_used, x_sorted, w_gate, w_up, w_down)


def _moe_combine_kernel(dest_ref, x_ref, rw_ref, fg_ref, ys_ref, *rest, tm, n_steps, final_norm, head_steps):
    out_refs, (ybuf, sem) = rest[:-2], rest[-2:]
    i = pl.program_id(0)
    slot = i % 2
    slot_rows = tm * TOP_K * SUBLANES

    def issue(step, to_slot):
        base = step * (tm * TOP_K)

        def body(r, carry):
            for k in range(TOP_K):
                dst = pl.ds(pl.multiple_of(to_slot * slot_rows + (r * TOP_K + k) * SUBLANES, SUBLANES), SUBLANES)
                pltpu.make_async_copy(ys_ref.at[_token_rows(dest_ref[base + r * TOP_K + k])], ybuf.at[dst],
                                      sem.at[to_slot]).start()
            return carry

        lax.fori_loop(0, tm, body, 0, unroll=8)

    @pl.when(i == 0)
    def _():
        issue(0, 0)

    @pl.when(i + 1 < n_steps)
    def _():
        issue(i + 1, 1 - slot)

    def wait(r, carry):
        for k in range(TOP_K):
            pltpu.make_async_copy(ys_ref.at[_token_rows(0)], ybuf.at[_token_rows(0)], sem.at[slot]).wait()
        return carry

    lax.fori_loop(0, tm, wait, 0, unroll=8)

    rw = rw_ref[...]
    g0 = rw[:, 0:1]
    g1 = rw[:, 1:2]
    first = pl.multiple_of(slot * slot_rows, SUBLANES)
    pieces = []
    for s in range(SUBLANES):
        y0 = ybuf[pl.ds(first + s, tm, stride=TOP_K * SUBLANES), :]
        y1 = ybuf[pl.ds(first + SUBLANES + s, tm, stride=TOP_K * SUBLANES), :]
        pieces.append(x_ref[:, s * LANES:(s + 1) * LANES] + (g0 * y0 + g1 * y1))
    if final_norm:
        sq = sum(jnp.sum(p * p, axis=-1, keepdims=True) for p in pieces)
        scale = lax.rsqrt(sq * (1.0 / (SUBLANES * LANES)) + EPS)
        pieces = [p * scale * fg_ref[:, s * LANES:(s + 1) * LANES] for s, p in enumerate(pieces)]
    def write(o_ref):
        for s, p in enumerate(pieces):
            o_ref[:, s * LANES:(s + 1) * LANES] = p

    if head_steps is None:
        write(out_refs[0])
    else:
        @pl.when(i < head_steps)
        def _():
            write(out_refs[0])

        @pl.when(i >= head_steps)
        def _():
            write(out_refs[1])


def moe_combine(dest, x, rw, y_sorted, final_gain, *, tm, final_norm, head_rows=None):
    n, d = x.shape
    n_steps = n // tm
    if head_rows is None:
        head_steps = None
        out_specs = [pl.BlockSpec((tm, d), lambda i, dd: (i, 0))]
        out_shape = [jax.ShapeDtypeStruct((n, d), F32)]
    else:
        assert head_rows % tm == 0 and 0 < head_rows < n
        head_steps = head_rows // tm
        out_specs = [pl.BlockSpec((tm, d), lambda i, dd: (jnp.minimum(i, head_steps - 1), 0)),
                     pl.BlockSpec((tm, d), lambda i, dd: (jnp.maximum(i - head_steps, 0), 0))]
        out_shape = [jax.ShapeDtypeStruct((head_rows, d), F32), jax.ShapeDtypeStruct((n - head_rows, d), F32)]
    grid_spec = pltpu.PrefetchScalarGridSpec(
        num_scalar_prefetch=1,
        grid=(n_steps,),
        in_specs=[pl.BlockSpec((tm, d), lambda i, dd: (i, 0)),
                  pl.BlockSpec((tm, LANES), lambda i, dd: (i, 0)),
                  pl.BlockSpec((1, d), lambda i, dd: (0, 0)),
                  pl.BlockSpec(memory_space=pl.ANY)],
        out_specs=out_specs,
        scratch_shapes=[pltpu.VMEM((2 * tm * TOP_K * SUBLANES, LANES), F32),
                        pltpu.SemaphoreType.DMA((2,))],
    )
    outs = pl.pallas_call(
        functools.partial(_moe_combine_kernel, tm=tm, n_steps=n_steps, final_norm=final_norm, head_steps=head_steps),
        grid_spec=grid_spec,
        out_shape=out_shape,
        compiler_params=_cparams("arbitrary"),
        name="moe_combine",
    )(dest, x, rw, final_gain.reshape(1, d), y_sorted)
    return outs[0] if head_rows is None else tuple(outs)


def _moe_plan(ri, counts, n):
    eid = ri[:, :TOP_K]
    rank = ri[:, TOP_K:2 * TOP_K]
    counts = counts[0, :N_EXPERTS].astype(jnp.int32)
    tiles = (counts + MOE_TM - 1) // MOE_TM
    tile_end = jnp.cumsum(tiles)
    tile_start = tile_end - tiles
    onehot = eid[:, :, None] == jnp.arange(N_EXPERTS, dtype=jnp.int32)[None, None, :]
    dest = jnp.sum(jnp.where(onehot, tile_start[None, None, :], 0), axis=-1) * MOE_TM + rank
    n_rows = TOP_K * n + N_EXPERTS * MOE_TM
    n_tiles = n_rows // MOE_TM
    n_used = tile_end[-1]
    t_idx = jnp.minimum(jnp.arange(n_tiles, dtype=jnp.int32), n_used - 1)
    tile_expert = jnp.sum((t_idx[:, None] >= tile_end[None, :]).astype(jnp.int32), axis=1)
    return dest.reshape(-1).astype(jnp.int32), tile_expert.astype(jnp.int32), n_used.reshape(1).astype(jnp.int32), n_rows


def hier_moe_block(x, gain, w_rg, b_rg, w_re, b_re, w_gate, w_up, w_down, layer, final_gain, final_norm,
                   head_rows=None):
    n, d = x.shape
    xn, rw, ri, counts = moe_router(x, gain, w_rg, b_rg, w_re, b_re, tm=512)
    dest, tile_expert, n_used, n_rows = _moe_plan(ri, counts, n)
    x_sorted = moe_dispatch(dest, xn, n_rows, tm=512)
    y_sorted = moe_ffn(tile_expert, n_used, x_sorted, w_gate, w_up, w_down, layer)
    return moe_combine(dest, x, rw, y_sorted, final_gain, tm=256, final_norm=final_norm, head_rows=head_rows)


MLA_HEADS = 16
MLA_LORA = 256
MLA_NOPE = 64
MLA_ROPE = 32
MLA_V = 64
MLA_SCALE = (MLA_NOPE + MLA_ROPE) ** -0.5
Q_SCALE = MLA_SCALE * math.log2(math.e)
ROPE_THETA = 10000.0
HEAD_PAD = 128
ROPE_AT = MLA_NOPE
Q_DEC = 384
V_ROWS = 80


def _rope_tables(pos):
    half = MLA_ROPE // 2
    inv = ROPE_THETA ** (-jnp.arange(half, dtype=F32) / half)
    ang = pos.astype(F32)[:, None] * inv[None, :]
    cos, sin = jnp.cos(ang), jnp.sin(ang)
    n = pos.shape[0]
    ones = jnp.ones((n, MLA_NOPE), F32)
    zeros_n = jnp.zeros((n, MLA_NOPE), F32)
    zeros_p = jnp.zeros((n, HEAD_PAD - MLA_NOPE - MLA_ROPE), F32)
    ctab = jnp.concatenate([ones, cos, cos, zeros_p], axis=1)
    stab = jnp.concatenate([zeros_n, sin, sin, zeros_p], axis=1)
    return ctab, stab


def _mla_weights(w_in, w_uq, w_uk, w_uv):
    d = w_in.shape[0]
    half = MLA_ROPE // 2
    w_kr = w_in[:, 2 * MLA_LORA:]
    zl = jnp.zeros((d, ROPE_AT), F32)
    zr = jnp.zeros((d, HEAD_PAD - ROPE_AT - MLA_ROPE), F32)
    kr_a = jnp.concatenate([zl, w_kr, zr], axis=1)
    kr_b = jnp.concatenate([zl, -w_kr[:, half:], w_kr[:, :half], zr], axis=1)
    w_in_p = jnp.concatenate([w_in[:, :2 * MLA_LORA], kr_a, kr_b], axis=1).astype(BF16)
    wq = w_uq.reshape(MLA_LORA, MLA_HEADS, MLA_NOPE + MLA_ROPE)
    nope, x1, x2 = wq[..., :MLA_NOPE], wq[..., MLA_NOPE:MLA_NOPE + half], wq[..., MLA_NOPE + half:]
    zp = jnp.zeros((MLA_LORA, MLA_HEADS, HEAD_PAD - MLA_NOPE - MLA_ROPE), F32)
    wq_a = jnp.concatenate([nope, x1, x2, zp], axis=-1).reshape(MLA_LORA, MLA_HEADS * HEAD_PAD).astype(BF16)
    wq_b = jnp.concatenate([jnp.zeros_like(nope), -x2, x1, zp], axis=-1).reshape(MLA_LORA, MLA_HEADS * HEAD_PAD).astype(BF16)
    zk = jnp.zeros((MLA_LORA, MLA_HEADS, HEAD_PAD - MLA_NOPE), F32)
    wuk_p = jnp.concatenate([w_uk, zk], axis=-1).reshape(MLA_LORA, MLA_HEADS * HEAD_PAD).astype(BF16)
    wuv = w_uv.reshape(MLA_LORA, MLA_HEADS * MLA_V).astype(BF16)
    absorb = jnp.transpose(w_uk, (1, 2, 0))
    sel = jnp.zeros((MLA_ROPE, Q_DEC - MLA_LORA), F32).at[jnp.arange(MLA_ROPE), jnp.arange(MLA_ROPE)].set(1.0)
    top = jnp.concatenate([absorb, jnp.zeros((MLA_HEADS, MLA_NOPE, Q_DEC - MLA_LORA), F32)], axis=-1)
    mid = jnp.broadcast_to(jnp.concatenate([jnp.zeros((MLA_ROPE, MLA_LORA), F32), sel], axis=-1)[None],
                           (MLA_HEADS, MLA_ROPE, Q_DEC))
    bot = jnp.zeros((MLA_HEADS, HEAD_PAD - MLA_NOPE - MLA_ROPE, Q_DEC), F32)
    w_dec = jnp.concatenate([top, mid, bot], axis=1).astype(BF16)
    return w_in_p, wq_a, wq_b, wuk_p, wuv, w_dec


def _mla_q_kernel(cq_ref, ckv_ref, kra_ref, krb_ref, ct_ref, st_ref, ctt_ref, stt_ref, qn_ref, kvn_ref,
                  wa_ref, wb_ref, qt_ref, ckvn_ref, krot_ref):
    cq = cq_ref[...]
    ms = jnp.mean(cq * cq, axis=-1, keepdims=True)
    cqn = (cq * lax.rsqrt(ms + EPS) * qn_ref[...]).astype(BF16)
    ckv = ckv_ref[...]
    ms2 = jnp.mean(ckv * ckv, axis=-1, keepdims=True)
    ckvn_ref[...] = ckv * lax.rsqrt(ms2 + EPS) * kvn_ref[...]
    krot_ref[...] = kra_ref[...] * ct_ref[...] + krb_ref[...] * st_ref[...]
    ctt = ctt_ref[...]
    stt = stt_ref[...]
    for h in range(MLA_HEADS):
        sl = slice(h * HEAD_PAD, (h + 1) * HEAD_PAD)
        qh = _dot_nt(wa_ref[sl, :], cqn) * ctt + _dot_nt(wb_ref[sl, :], cqn) * stt
        qt_ref[sl, :] = (qh * Q_SCALE).astype(qt_ref.dtype)


def mla_q(cq, ckv, kr_a, kr_b, ctab, stab, q_norm, kv_norm, wq_at, wq_bt, *, tm):
    n = cq.shape[0]
    hw = MLA_HEADS * HEAD_PAD

    def tok(w):
        return pl.BlockSpec((tm, w), lambda i: (i, 0))

    def tok_t(w):
        return pl.BlockSpec((w, tm), lambda i: (0, i))

    def const(shape):
        return pl.BlockSpec(shape, lambda i: (0, 0))

    return pl.pallas_call(
        _mla_q_kernel,
        grid=(n // tm,),
        in_specs=[tok(MLA_LORA), tok(MLA_LORA), tok(HEAD_PAD), tok(HEAD_PAD), tok(HEAD_PAD), tok(HEAD_PAD),
                  tok_t(HEAD_PAD), tok_t(HEAD_PAD),
                  const((1, MLA_LORA)), const((1, MLA_LORA)), const(wq_at.shape), const(wq_bt.shape)],
        out_specs=[tok_t(hw), tok(MLA_LORA), tok(HEAD_PAD)],
        out_shape=[jax.ShapeDtypeStruct((hw, n), BF16),
                   jax.ShapeDtypeStruct((n, MLA_LORA), F32),
                   jax.ShapeDtypeStruct((n, HEAD_PAD), F32)],
        compiler_params=_cparams("parallel"),
        name="mla_q",
    )(cq, ckv, kr_a, kr_b, ctab, stab, ctab.T, stab.T, q_norm.reshape(1, MLA_LORA), kv_norm.reshape(1, MLA_LORA),
      wq_at, wq_bt)


def _mla_kv_kernel(ckvn_ref, krot_ref, wk_ref, wvt_ref, k_ref, vt_ref):
    c = ckvn_ref[...].astype(BF16)
    krot = krot_ref[...]
    for h in range(MLA_HEADS):
        sl = slice(h * HEAD_PAD, (h + 1) * HEAD_PAD)
        k_ref[:, sl] = (_dot(c, wk_ref[:, sl]) + krot).astype(k_ref.dtype)
    vt = _dot_nt(wvt_ref[...], c)
    row = lax.broadcasted_iota(jnp.int32, vt.shape, 0)
    vt_ref[...] = jnp.where(row % V_ROWS == MLA_V, 1.0, vt).astype(vt_ref.dtype)


def mla_kv(ckvn, krot, wuk_p, wuv_t, *, n, tm):
    hw = MLA_HEADS * HEAD_PAD
    vw = MLA_HEADS * V_ROWS
    wuv_t = jnp.pad(wuv_t.reshape(MLA_HEADS, MLA_V, MLA_LORA), ((0, 0), (0, V_ROWS - MLA_V), (0, 0))).reshape(vw, MLA_LORA)
    return pl.pallas_call(
        _mla_kv_kernel,
        grid=(n // tm,),
        in_specs=[pl.BlockSpec((tm, MLA_LORA), lambda i: (i, 0)),
                  pl.BlockSpec((tm, HEAD_PAD), lambda i: (i, 0)),
                  pl.BlockSpec(wuk_p.shape, lambda i: (0, 0)),
                  pl.BlockSpec(wuv_t.shape, lambda i: (0, 0))],
        out_specs=[pl.BlockSpec((tm, hw), lambda i: (i, 0)),
                   pl.BlockSpec((None, vw, tm), lambda i: (i, 0, 0))],
        out_shape=[jax.ShapeDtypeStruct((n, hw), BF16),
                   jax.ShapeDtypeStruct((n // tm, vw, tm), BF16)],
        compiler_params=_cparams("parallel"),
        name="mla_kv",
    )(ckvn, krot, wuk_p, wuv_t)


def _flash_kernel(qt_ref, k_ref, vt_ref, ot_ref, m_scr, acc_scr, p_scr, alpha_scr, *, tq, tkb):
    qi = pl.program_id(2)
    sub = tq // tkb
    m_scr[...] = jnp.full_like(m_scr, NEG_BIG)
    acc_scr[...] = jnp.zeros_like(acc_scr)
    p_scr[...] = jnp.zeros_like(p_scr)
    alpha_scr[...] = jnp.ones_like(alpha_scr)

    def retire(kb_prev):
        for hh in range(FLASH_HEADS):
            pv = _dot(vt_ref[kb_prev, hh * V_ROWS:(hh + 1) * V_ROWS, :], p_scr[hh])
            acc_scr[hh] = alpha_scr[hh] * acc_scr[hh] + pv

    strips = [(hh, st) for hh in range(FLASH_HEADS) for st in range(tq // FLASH_STRIP)]

    def score(kb, hh, st):
        rows = pl.ds(pl.multiple_of(kb * tkb, tkb), tkb)
        return _dot(k_ref[rows, hh * HEAD_PAD:(hh + 1) * HEAD_PAD],
                    qt_ref[hh * HEAD_PAD:(hh + 1) * HEAD_PAD, st * FLASH_STRIP:(st + 1) * FLASH_STRIP])

    def key_block(kb, diag_off):
        scores = [score(kb, hh, st) for hh, st in strips]
        retire(jnp.maximum(kb - 1, 0))
        for (hh, st), s in zip(strips, scores):
            cols = slice(st * FLASH_STRIP, (st + 1) * FLASH_STRIP)
            if diag_off is not None:
                key = lax.broadcasted_iota(jnp.int32, s.shape, 0) + diag_off
                qry = lax.broadcasted_iota(jnp.int32, s.shape, 1) + st * FLASH_STRIP
                s = jnp.where(key <= qry, s, NEG_BIG)
            m_old = m_scr[hh, :, cols]
            m_new = jnp.maximum(m_old, jnp.max(s, axis=0, keepdims=True))
            alpha = jnp.exp2(m_old - m_new)
            m_scr[hh, :, cols] = m_new
            p_scr[hh, :, cols] = jnp.exp2(s - m_new).astype(BF16)
            alpha_scr[hh, :, cols] = alpha

    def full_tile(j, carry):
        for d in range(sub):
            key_block(j * sub + d, None)
        return carry

    lax.fori_loop(0, qi, full_tile, 0)
    for d in range(sub):
        key_block(qi * sub + d, d * tkb)
    retire(qi * sub + sub - 1)
    for hh in range(FLASH_HEADS):
        acc = acc_scr[hh]
        ot_ref[hh * MLA_V:(hh + 1) * MLA_V, :] = (acc[:MLA_V] / acc[MLA_V:MLA_V + 1]).astype(ot_ref.dtype)


def flash_attention(qt, k, vt, *, b, t, tq):
    tkb = vt.shape[2]
    nq = t // tq
    nkb = t // tkb
    assert tq % tkb == 0 and t % tq == 0
    fh = FLASH_HEADS
    n_hp = MLA_HEADS // fh
    return pl.pallas_call(
        functools.partial(_flash_kernel, tq=tq, tkb=tkb),
        grid=(b, n_hp, nq),
        in_specs=[pl.BlockSpec((fh * HEAD_PAD, tq), lambda bi, hp, qi: (hp, bi * nq + qi)),
                  pl.BlockSpec((t, fh * HEAD_PAD), lambda bi, hp, qi: (bi, hp)),
                  pl.BlockSpec((nkb, fh * V_ROWS, tkb), lambda bi, hp, qi: (bi, hp, 0))],
        out_specs=pl.BlockSpec((fh * MLA_V, tq), lambda bi, hp, qi: (hp, bi * nq + qi)),
        out_shape=jax.ShapeDtypeStruct((MLA_HEADS * MLA_V, b * t), BF16),
        scratch_shapes=[pltpu.VMEM((fh, 1, tq), F32),
                        pltpu.VMEM((fh, V_ROWS, tq), F32), pltpu.VMEM((fh, tkb, tq), BF16),
                        pltpu.VMEM((fh, 1, tq), F32)],
        compiler_params=_cparams("parallel", "parallel", "arbitrary"),
        name="mla_flash",
    )(qt, k, vt)


PAGE = 128
DEC_PB = 32
DEC_SUB = 2048
DEC_NBUF = 4
DEC_AHEAD = 2
NEW_PAD = 8


def _q_dec_kernel(qt_ref, w_ref, o_ref):
    for h in range(MLA_HEADS):
        o_ref[:, h * Q_DEC:(h + 1) * Q_DEC] = _dot_tn(qt_ref[h * HEAD_PAD:(h + 1) * HEAD_PAD, :], w_ref[h]).astype(o_ref.dtype)


def mla_q_dec(qt, w_dec):
    n = qt.shape[1]
    return pl.pallas_call(
        _q_dec_kernel,
        grid=(1,),
        in_specs=[pl.BlockSpec(qt.shape, lambda i: (0, 0)), pl.BlockSpec(w_dec.shape, lambda i: (0, 0, 0))],
        out_specs=pl.BlockSpec((n, MLA_HEADS * Q_DEC), lambda i: (0, 0)),
        out_shape=jax.ShapeDtypeStruct((n, MLA_HEADS * Q_DEC), BF16),
        compiler_params=_cparams("arbitrary"),
        name="mla_q_dec",
    )(qt, w_dec)


def _decode_kernel(pt_ref, q_ref, cnew_ref, rnew_ref, cache_c, cache_rt, o_ref, cbuf, rbuf, sem,
                   *, layer, n_pages, t_new, n_seq):
    b = pl.program_id(0)
    n_blk = n_pages // DEC_PB
    rows = q_ref.shape[0]

    def page_copies(pg, slot, i):
        off = pl.ds(pl.multiple_of(i * PAGE, PAGE), PAGE)
        cc = pltpu.make_async_copy(cache_c.at[layer, pg], cbuf.at[slot, off], sem.at[0, slot])
        cr = pltpu.make_async_copy(cache_rt.at[layer, pg], rbuf.at[slot, :, off], sem.at[1, slot])
        return cc, cr

    def start_block(seq, j, slot):
        def body(i, carry):
            cc, cr = page_copies(pt_ref[seq, j * DEC_PB + i], slot, i)
            cc.start()
            cr.start()
            return carry
        lax.fori_loop(0, DEC_PB, body, 0, unroll=8)

    def wait_block(slot):
        def body(i, carry):
            cc, cr = page_copies(0, slot, 0)
            cc.wait()
            cr.wait()
            return carry
        lax.fori_loop(0, DEC_PB, body, 0, unroll=8)

    q = q_ref[...]
    ql = q[:, :MLA_LORA]
    qr = q[:, MLA_LORA:MLA_LORA + MLA_ROPE]

    @pl.when(b == 0)
    def _():
        for j in range(DEC_AHEAD):
            start_block(0, j, j % DEC_NBUF)

    m = jnp.full((rows, 1), NEG_BIG, F32)
    l = jnp.zeros((rows, 1), F32)
    acc = jnp.zeros((rows, MLA_LORA), F32)
    pending = None
    for j in range(n_blk):
        slot = j % DEC_NBUF
        ahead = j + DEC_AHEAD
        if ahead < n_blk:
            start_block(b, ahead, ahead % DEC_NBUF)
        else:
            @pl.when(b + 1 < n_seq)
            def _():
                start_block(b + 1, ahead - n_blk, (ahead - n_blk) % DEC_NBUF)
        wait_block(slot)
        for u in range(DEC_PB * PAGE // DEC_SUB):
            keys = slice(u * DEC_SUB, (u + 1) * DEC_SUB)
            kc = cbuf[slot, keys, :].astype(BF16)
            krt = rbuf[slot, :, keys].astype(BF16)
            s = _dot_nt(ql, kc) + _dot(qr, krt)
            if pending is not None:
                p_prev, alpha_prev, kc_prev = pending
                acc = alpha_prev * acc + _dot(p_prev, kc_prev)
            m_new = jnp.maximum(m, jnp.max(s, axis=-1, keepdims=True))
            alpha = jnp.exp2(m - m_new)
            p = jnp.exp2(s - m_new)
            l = alpha * l + jnp.sum(p, axis=-1, keepdims=True)
            m = m_new
            pending = (p.astype(BF16), alpha, kc)

    cn = cnew_ref[...].astype(BF16)
    rn = rnew_ref[...].astype(BF16)
    s = _dot_nt(ql, cn) + _dot_nt(qr, rn)
    p_prev, alpha_prev, kc_prev = pending
    acc = alpha_prev * acc + _dot(p_prev, kc_prev)
    t_row = lax.broadcasted_iota(jnp.int32, s.shape, 0) // MLA_HEADS
    col = lax.broadcasted_iota(jnp.int32, s.shape, 1)
    s = jnp.where((col <= t_row) & (col < t_new), s, NEG_BIG)
    m_new = jnp.maximum(m, jnp.max(s, axis=-1, keepdims=True))
    alpha = jnp.exp2(m - m_new)
    p = jnp.exp2(s - m_new)
    l = alpha * l + jnp.sum(p, axis=-1, keepdims=True)
    acc = alpha * acc + _dot(p.astype(BF16), cn)
    o_ref[...] = (acc / l).astype(o_ref.dtype)


def mla_decode(page_table, q_dec, c_new, r_new, cache_ckv, cache_krope_t, *, layer, t_new):
    b, rows, _ = q_dec.shape
    n_pages = page_table.shape[1]
    assert n_pages % (DEC_NBUF * DEC_PB) == 0 and DEC_AHEAD < DEC_NBUF
    grid_spec = pltpu.PrefetchScalarGridSpec(
        num_scalar_prefetch=1,
        grid=(b,),
        in_specs=[pl.BlockSpec((None, rows, Q_DEC), lambda i, pt: (i, 0, 0)),
                  pl.BlockSpec((None, NEW_PAD, MLA_LORA), lambda i, pt: (i, 0, 0)),
                  pl.BlockSpec((None, NEW_PAD, MLA_ROPE), lambda i, pt: (i, 0, 0)),
                  pl.BlockSpec(memory_space=pl.ANY),
                  pl.BlockSpec(memory_space=pl.ANY)],
        out_specs=pl.BlockSpec((None, rows, MLA_LORA), lambda i, pt: (i, 0, 0)),
        scratch_shapes=[pltpu.VMEM((DEC_NBUF, DEC_PB * PAGE, MLA_LORA), F32),
                        pltpu.VMEM((DEC_NBUF, MLA_ROPE, DEC_PB * PAGE), F32),
                        pltpu.SemaphoreType.DMA((2, DEC_NBUF))],
    )
    return pl.pallas_call(
        functools.partial(_decode_kernel, layer=layer, n_pages=n_pages, t_new=t_new, n_seq=b),
        grid_spec=grid_spec,
        out_shape=jax.ShapeDtypeStruct((b, rows, MLA_LORA), BF16),
        compiler_params=_cparams("arbitrary"),
        name="mla_decode",
    )(page_table, q_dec, c_new, r_new, cache_ckv, cache_krope_t)


def _sample_v_kernel(o_ref, wt_ref, yt_ref):
    for h in range(MLA_HEADS):
        yt_ref[h * MLA_V:(h + 1) * MLA_V, :] = _dot_nt(wt_ref[h * MLA_V:(h + 1) * MLA_V, :],
                                                       o_ref[:, h * MLA_LORA:(h + 1) * MLA_LORA]).astype(yt_ref.dtype)


def mla_sample_v(o_lat, wuv_t):
    n = o_lat.shape[0]
    return pl.pallas_call(
        _sample_v_kernel,
        grid=(1,),
        in_specs=[pl.BlockSpec(o_lat.shape, lambda i: (0, 0)), pl.BlockSpec(wuv_t.shape, lambda i: (0, 0))],
        out_specs=pl.BlockSpec((MLA_HEADS * MLA_V, n), lambda i: (0, 0)),
        out_shape=jax.ShapeDtypeStruct((MLA_HEADS * MLA_V, n), BF16),
        compiler_params=_cparams("arbitrary"),
        name="mla_sample_v",
    )(o_lat, wuv_t)


HG_CHUNK = 128
SSD_CHUNK = 256
SAMPLE_PAD = 8
SAMPLE_SEQS = 4
FLASH_QUERIES = 512
FLASH_KEYS = 512
FLASH_STRIP = 256
FLASH_HEADS = 2
ROW_TILE = 256
WIDE_TILE = 512


def _ab_layer(x, n_p, bp, tp, bs, ts, norm_w, lb, st_hg, st_ssm, st_conv, w_in, w_out, hg_norm, conv_w, conv_b,
              dt_bias, a_log, d_skip, ssm_norm):
    sizes = [HG_W, HG_W, HG_W, HG_W, SSM_INNER, CONV_DIM, SSM_HEADS]
    offs = np.concatenate([[0], np.cumsum(sizes)])
    w_pad = jnp.pad(w_in, ((0, 0), (0, LANES - SSM_HEADS))).astype(BF16)
    splits = [(int(offs[j]), int(offs[j + 1])) for j in range(6)] + [(int(offs[6]), int(offs[6]) + LANES)]
    q, f, i_in, g, z, xbc, dt = norm_matmul(x, norm_w, w_pad, splits, [F32] * 7, tm=ROW_TILE)

    def grp(a, prompt):
        if prompt:
            return a
        a = a[n_p:].reshape(bs, ts, a.shape[1])
        return jnp.pad(a, ((0, 0), (0, SAMPLE_PAD - ts), (0, 0))).reshape(bs * SAMPLE_PAD, a.shape[2])

    outs = []
    for prompt in (True, False):
        if prompt:
            b_, t_, c_hg, c_ssd, tv_hg, tv_ssd = bp, tp, HG_CHUNK, SSD_CHUNK, HG_CHUNK, SSD_CHUNK
            s_hg = jnp.zeros((bp, HG_HEADS, HG_D, HG_D), F32)
            s_ssm = jnp.zeros((bp, SSM_HEADS, SSM_P, SSM_N), F32)
            s_conv = jnp.zeros((bp, CONV_W - 1, CONV_DIM), F32)
        else:
            b_, t_, c_hg, c_ssd, tv_hg, tv_ssd = bs, SAMPLE_PAD, SAMPLE_PAD, SAMPLE_PAD, ts, ts
            s_hg, s_ssm, s_conv = st_hg, st_ssm, st_conv
        n_seq = 1 if prompt else math.gcd(bs, SAMPLE_SEQS)
        o_hg, hg_new = hgrn_scan(grp(q, prompt), grp(f, prompt), grp(i_in, prompt), grp(g, prompt), lb, hg_norm,
                                 s_hg, b=b_, t=t_, c=c_hg, t_valid=tv_hg, n_seq=n_seq)
        y, conv_new, ssm_new = ssd_scan(grp(z, prompt), grp(xbc, prompt), grp(dt, prompt), s_conv, s_ssm,
                                        conv_w, conv_b, dt_bias, a_log, d_skip, ssm_norm,
                                        b=b_, t=t_, c=c_ssd, t_valid=tv_ssd, n_seq=n_seq)
        if not prompt:
            o_hg = o_hg.reshape(bs, SAMPLE_PAD, HG_W)[:, :ts].reshape(bs * ts, HG_W)
            y = y.reshape(bs, SAMPLE_PAD, SSM_INNER)[:, :ts].reshape(bs * ts, SSM_INNER)
        outs.append((o_hg, y, hg_new, ssm_new, conv_new))
    o_hg = (outs[0][0], outs[1][0])
    y = (outs[0][1], outs[1][1])
    w_out_b = w_out.astype(BF16)
    x = matmul_residual(x, [(o_hg, w_out_b[:HG_W], False), (y, w_out_b[HG_W:], False)], tm=WIDE_TILE)
    return x, outs[0][2:], outs[1][2:]


def _mla_layer(x, n_p, bp, tp, bs, ts, past_len, norm_w, cache_ckv, cache_krope, page_table, layer_c,
               w_in, q_norm, kv_norm, w_uq, w_uk, w_uv, w_out):
    w_in_p, wq_a, wq_b, wuk_p, wuv, w_dec = _mla_weights(w_in, w_uq, w_uk, w_uv)
    wuv_t = wuv.T
    splits = [(0, MLA_LORA), (MLA_LORA, 2 * MLA_LORA), (2 * MLA_LORA, 2 * MLA_LORA + HEAD_PAD),
              (2 * MLA_LORA + HEAD_PAD, 2 * MLA_LORA + 2 * HEAD_PAD)]
    cq, ckv, kr_a, kr_b = norm_matmul(x, norm_w, w_in_p, splits, [F32] * 4, tm=WIDE_TILE)
    pos = jnp.concatenate([jnp.tile(jnp.arange(tp, dtype=jnp.int32), bp),
                           jnp.tile(past_len + jnp.arange(ts, dtype=jnp.int32), bs)])
    ctab, stab = _rope_tables(pos)
    qt, ckvn, krot = mla_q(cq, ckv, kr_a, kr_b, ctab, stab, q_norm, kv_norm, wq_a.T, wq_b.T, tm=WIDE_TILE)
    krope = krot[:, ROPE_AT:ROPE_AT + MLA_ROPE]
    k_p, vt_p = mla_kv(ckvn, krot, wuk_p, wuv_t, n=n_p, tm=math.gcd(tp, FLASH_KEYS))
    ot_p = flash_attention(qt, k_p, vt_p, b=bp, t=tp, tq=math.gcd(tp, FLASH_QUERIES))
    q_dec = mla_q_dec(qt[:, n_p:], w_dec).reshape(bs, ts * MLA_HEADS, Q_DEC)
    c_new = jnp.pad(ckvn[n_p:].reshape(bs, ts, MLA_LORA), ((0, 0), (0, NEW_PAD - ts), (0, 0)))
    r_new = jnp.pad(krope[n_p:].reshape(bs, ts, MLA_ROPE), ((0, 0), (0, NEW_PAD - ts), (0, 0)))
    cache_krope_t = jnp.swapaxes(cache_krope, 2, 3)
    o_lat = mla_decode(page_table, q_dec, c_new, r_new, cache_ckv, cache_krope_t, layer=layer_c, t_new=ts)
    ot_s = mla_sample_v(o_lat.reshape(bs * ts, MLA_HEADS * MLA_LORA), wuv_t)
    x = matmul_residual(x, [((ot_p, ot_s), w_out.astype(BF16), True)], tm=WIDE_TILE)
    return x, (ckvn[:n_p].reshape(bp, tp, MLA_LORA), krope[:n_p].reshape(bp, tp, MLA_ROPE)), \
        (ckvn[n_p:].reshape(bs, ts, MLA_LORA), krope[n_p:].reshape(bs, ts, MLA_ROPE))


def kernel(x_prompt, x_sample, state_hgrn, state_ssm, state_conv, cache_ckv, cache_krope, page_table,
           norm_mix, norm_ffn, norm_final, w_in_ab, w_out_ab, hgrn_lb, hgrn_norm, conv_w, conv_b,
           dt_bias, a_log, d_skip, ssm_norm, w_in_c, q_norm, kv_norm, w_uq, w_uk, w_uv, w_out_c,
           w_route_group, b_route_group, w_route_expert, b_route_expert, w_gate, w_up, w_down):
    bp, tp, d = x_prompt.shape
    bs, ts, _ = x_sample.shape
    n_p = bp * tp
    depth = norm_mix.shape[0]
    n_a = w_in_ab.shape[0]
    past_len = page_table.shape[1] * cache_ckv.shape[2]
    lb_all = jnp.cumsum(jax.nn.softmax(hgrn_lb.astype(F32), axis=0), axis=0)[:n_a]
    x = (x_prompt.reshape(n_p, d), x_sample.reshape(bs * ts, d))
    a_p, a_s, c_p, c_s = [], [], [], []
    for layer in range(depth):
        j = layer // 2
        if layer % 2 == 0:
            x, sp, ss = _ab_layer(x, n_p, bp, tp, bs, ts, norm_mix[layer], lb_all[j], state_hgrn[j], state_ssm[j],
                                  state_conv[j], w_in_ab[j], w_out_ab[j], hgrn_norm[j], conv_w[j], conv_b[j],
                                  dt_bias[j], a_log[j], d_skip[j], ssm_norm[j])
            a_p.append(sp)
            a_s.append(ss)
        else:
            x, cp, cs = _mla_layer(x, n_p, bp, tp, bs, ts, past_len, norm_mix[layer], cache_ckv, cache_krope,
                                   page_table, j, w_in_c[j], q_norm[j], kv_norm[j], w_uq[j], w_uk[j], w_uv[j],
                                   w_out_c[j])
            c_p.append(cp)
            c_s.append(cs)
        x = hier_moe_block(x, norm_ffn[layer], w_route_group[layer], b_route_group[layer], w_route_expert[layer],
                           b_route_expert[layer], w_gate, w_up, w_down, layer, norm_final,
                           final_norm=(layer == depth - 1), head_rows=(n_p if layer == depth - 1 else None))
    y_p, y_s = x

    def stack(items, k):
        return jnp.stack([it[k] for it in items])

    return (y_p.reshape(bp, tp, d), y_s.reshape(bs, ts, d),
            stack(a_p, 0), stack(a_s, 0), stack(a_p, 1), stack(a_s, 1), stack(a_p, 2), stack(a_s, 2),
            stack(c_p, 0), stack(c_s, 0), stack(c_p, 1), stack(c_s, 1))
```
